```python
import jax, jax.numpy as jnp
from jax import lax
import numpy as np

D_MODEL = 2048
BATCH = 1
SEQ = 8192
DEPTH = 2

HEAD_DIM = 128
N_HEADS_A = 6
N_HEADS_B = 6
N_HEADS_C = 4
W_A = N_HEADS_A * HEAD_DIM
W_B = N_HEADS_B * HEAD_DIM
W_C = N_HEADS_C * HEAD_DIM
Q_RANK_A = 512
KV_RANK_A = 256
N_IDX_HEADS = 16
IDX_DIM = 64
TOPK_A_MAX = 256
MOBA_BLOCK = 256
MOBA_TOPK = 3
N_MEM = 256
N_ALIBI = N_HEADS_A + N_HEADS_B
N_GROUPS = 4
EXPERTS_PER_GROUP = 8
N_EXPERTS = N_GROUPS * EXPERTS_PER_GROUP
TOPK_IN_GROUP = 2
D_EXPERT = 512
Q_BLOCK = 128
ALPHA = (2 * DEPTH) ** 0.25
BETA = (8 * DEPTH) ** -0.25
NORM_EPS = 1e-5
NEG = -1e30

kernel_name = 'hybrid_dsa_moba_mem_hmoe_deepnorm'


def layer_norm(x, g, b):
    xf = x.astype(jnp.float32)
    mu = jnp.mean(xf, -1, keepdims=True)
    var = jnp.mean(jnp.square(xf - mu), -1, keepdims=True)
    return ((xf - mu) * lax.rsqrt(var + NORM_EPS) * g.astype(jnp.float32) + b.astype(jnp.float32)).astype(x.dtype)


def rms_norm(x, g):
    xf = x.astype(jnp.float32)
    return (xf * lax.rsqrt(jnp.mean(jnp.square(xf), -1, keepdims=True) + NORM_EPS) * g.astype(jnp.float32)).astype(x.dtype)


def split_cols(h):
    sizes = [Q_RANK_A, KV_RANK_A, IDX_DIM, N_IDX_HEADS, W_B, W_B, W_B, W_C]
    return jnp.split(h, np.cumsum(sizes)[:-1].tolist(), axis=-1)


def dsa_attention(q, k, v, q_idx, k_idx, w_idx, slopes):
    B, T = q.shape[:2]
    top_k = min(TOPK_A_MAX, T // 4)
    s_pos = jnp.arange(T)

    def one_block(i):
        t0 = i * Q_BLOCK
        t_pos = t0 + jnp.arange(Q_BLOCK)
        qi = lax.dynamic_slice_in_dim(q_idx, t0, Q_BLOCK, axis=1)
        wi = lax.dynamic_slice_in_dim(w_idx, t0, Q_BLOCK, axis=1)
        rel = jax.nn.relu(jnp.einsum('bqhd,bsd->bqhs', qi, k_idx))
        score = jnp.einsum('bqh,bqhs->bqs', wi, rel).astype(jnp.float32)
        score = jnp.where((s_pos[None, :] <= t_pos[:, None])[None], score, NEG)
        _, sel = lax.top_k(score, top_k)
        ks = jax.vmap(lambda kk, ii: kk[ii])(k, sel)
        vs = jax.vmap(lambda vv, ii: vv[ii])(v, sel)
        qb = lax.dynamic_slice_in_dim(q, t0, Q_BLOCK, axis=1)
        s = jnp.einsum('bqhd,bqkhd->bhqk', qb, ks).astype(jnp.float32) * HEAD_DIM ** -0.5
        dist = (t_pos[None, :, None] - sel).astype(jnp.float32)
        s = s - slopes[None, :, None, None] * dist[:, None]
        s = jnp.where((sel <= t_pos[None, :, None])[:, None], s, NEG)
        p = jax.nn.softmax(s, axis=-1).astype(v.dtype)
        return jnp.einsum('bhqk,bqkhd->bqhd', p, vs)

    o = lax.map(one_block, jnp.arange(T // Q_BLOCK))
    return jnp.moveaxis(o, 0, 1).reshape(B, T, -1)


def moba_attention(q, k, v, slopes):
    B, T, H, Dh = q.shape
    n_kb = -(-T // MOBA_BLOCK)
    t_pad = n_kb * MOBA_BLOCK
    top_b = min(MOBA_TOPK, n_kb)
    pad = ((0, 0), (0, t_pad - T), (0, 0), (0, 0))
    k_p, v_p = jnp.pad(k, pad), jnp.pad(v, pad)
    k_blk = k_p.reshape(B, n_kb, MOBA_BLOCK, H, Dh)
    k_mean = jnp.mean(k_blk, axis=2)
    k_bh = k_blk.transpose(0, 3, 1, 2, 4)
    v_bh = v_p.reshape(B, n_kb, MOBA_BLOCK, H, Dh).transpose(0, 3, 1, 2, 4)
    h_ix = jnp.arange(H)[None, :, None]
    off = jnp.arange(MOBA_BLOCK)
    scale = Dh ** -0.5

    def one_block(i):
        t0 = i * Q_BLOCK
        t_pos = t0 + jnp.arange(Q_BLOCK)
        c = t0 // MOBA_BLOCK
        qb = lax.dynamic_slice_in_dim(q, t0, Q_BLOCK, axis=1)
        gate = jnp.einsum('bqhd,bnhd->bqhn', qb, k_mean).astype(jnp.float32)
        gate = jnp.where(jnp.arange(n_kb) < c, gate, NEG)
        _, blk = lax.top_k(gate, top_b)
        ks = jax.vmap(lambda kk, ii: kk[h_ix, ii])(k_bh, blk)
        vs = jax.vmap(lambda vv, ii: vv[h_ix, ii])(v_bh, blk)
        s_sel = jnp.einsum('bqhd,bqhjkd->bqhjk', qb, ks).astype(jnp.float32) * scale
        pos_sel = blk[..., None] * MOBA_BLOCK + off
        dist_sel = (t_pos[None, :, None, None, None] - pos_sel).astype(jnp.float32)
        s_sel = s_sel - slopes[None, None, :, None, None] * dist_sel
        s_sel = jnp.where((blk < c)[..., None], s_sel, NEG)
        k_own = lax.dynamic_slice_in_dim(k_p, c * MOBA_BLOCK, MOBA_BLOCK, axis=1)
        v_own = lax.dynamic_slice_in_dim(v_p, c * MOBA_BLOCK, MOBA_BLOCK, axis=1)
        s_own = jnp.einsum('bqhd,bkhd->bqhk', qb, k_own).astype(jnp.float32) * scale
        dist_own = t_pos[:, None] - (c * MOBA_BLOCK + off)[None, :]
        s_own = s_own - slopes[None, None, :, None] * dist_own.astype(jnp.float32)[None, :, None, :]
        s_own = jnp.where((dist_own >= 0)[None, :, None, :], s_own, NEG)
        s = jnp.concatenate([s_sel.reshape(B, Q_BLOCK, H, top_b * MOBA_BLOCK), s_own], axis=-1)
        p = jax.nn.softmax(s, axis=-1).astype(v.dtype)
        p_sel = p[..., :top_b * MOBA_BLOCK].reshape(B, Q_BLOCK, H, top_b, MOBA_BLOCK)
        p_own = p[..., top_b * MOBA_BLOCK:]
        return (jnp.einsum('bqhjk,bqhjkd->bqhd', p_sel, vs)
                + jnp.einsum('bqhk,bkhd->bqhd', p_own, v_own))

    o = lax.map(one_block, jnp.arange(T // Q_BLOCK))
    return jnp.moveaxis(o, 0, 1).reshape(B, T, -1)


def memory_attention(q, mk, mv):
    B, T = q.shape[:2]
    s = jnp.einsum('bthd,bmhd->bhtm', q, mk).astype(jnp.float32) * HEAD_DIM ** -0.5
    p = jax.nn.softmax(s, axis=-1).astype(mv.dtype)
    return jnp.einsum('bhtm,bmhd->bthd', p, mv).reshape(B, T, -1)


def hier_moe(x, w_grp, b_grp, w_rt, b_rt, w1, w3, w2):
    B, T, D = x.shape
    xt = x.reshape(B * T, D)
    n = xt.shape[0]
    g_logit = (xt @ w_grp + b_grp).astype(jnp.float32)
    g_sel = jnp.argmax(g_logit, axis=-1)
    p_g = jnp.take_along_axis(jax.nn.softmax(g_logit, -1), g_sel[:, None], -1)
    e_logit = (xt @ w_rt + b_rt).astype(jnp.float32).reshape(n, N_GROUPS, EXPERTS_PER_GROUP)
    e_logit = jnp.take_along_axis(e_logit, g_sel[:, None, None], axis=1)[:, 0]
    p_e, e_sel = lax.top_k(jax.nn.softmax(e_logit, -1), TOPK_IN_GROUP)
    p_e = p_e / jnp.sum(p_e, -1, keepdims=True)
    wts = p_g * p_e
    gidx = g_sel[:, None] * EXPERTS_PER_GROUP + e_sel
    comb = jnp.einsum('nk,nke->ne', wts, jax.nn.one_hot(gidx, N_EXPERTS, dtype=jnp.float32)).astype(x.dtype)
    out = jnp.zeros_like(xt)
    for e in range(N_EXPERTS):
        hid = jax.nn.silu(xt @ w1[e]) * (xt @ w3[e])
        out = out + comb[:, e:e + 1] * (hid @ w2[e])
    return out.reshape(B, T, D)


def setup_inputs(seed: int = 0) -> dict:
    key = jax.random.key(seed)
    keys = iter(jax.random.split(key, 64))

    def nrm(shape, fan_in, scale=1.0):
        return jax.random.normal(next(keys), shape, jnp.float32) * (scale * fan_in ** -0.5)

    def gain(shape):
        return 1.0 + 0.05 * jax.random.normal(next(keys), shape, jnp.float32)

    def bias(shape):
        return 0.01 * jax.random.normal(next(keys), shape, jnp.float32)

    L, D = DEPTH, D_MODEL
    x = jax.random.normal(next(keys), (BATCH, SEQ, D), jnp.float32)
    mem = jax.random.normal(next(keys), (BATCH, N_MEM, D), jnp.float32)
    w_mem_kv = jnp.concatenate([nrm((D, W_C), D), nrm((D, W_C), D, BETA)], -1)
    w_in = jnp.concatenate([
        nrm((L, D, Q_RANK_A), D), nrm((L, D, KV_RANK_A), D), nrm((L, D, IDX_DIM), D),
        nrm((L, D, N_IDX_HEADS), D), nrm((L, D, W_B), D), nrm((L, D, W_B), D),
        nrm((L, D, W_B), D, BETA), nrm((L, D, W_C), D)], -1)
    return {
        'x': x,
        'mem': mem,
        'w_mem_kv': w_mem_kv,
        'w_in': w_in,
        'g_cq': gain((L, Q_RANK_A)),
        'g_ckv': gain((L, KV_RANK_A)),
        'g_kidx': gain((L, IDX_DIM)),
        'b_kidx': bias((L, IDX_DIM)),
        'w_uq': nrm((L, Q_RANK_A, W_A), Q_RANK_A),
        'w_uqi': nrm((L, Q_RANK_A, N_IDX_HEADS * IDX_DIM), Q_RANK_A),
        'w_ukv': jnp.concatenate([nrm((L, KV_RANK_A, W_A), KV_RANK_A), nrm((L, KV_RANK_A, W_A), KV_RANK_A, BETA)], -1),
        'w_up_a': nrm((L, W_A, D), W_A, BETA),
        'w_up_b': nrm((L, W_B, D), W_B, BETA),
        'w_up_c': nrm((L, W_C, D), W_C, BETA),
        'w_gate': nrm((L, D, 3 * D), D),
        'b_gate': bias((L, 3 * D)),
        'w_o': nrm((L, D, D), D, BETA),
        'ln1_g': gain((L, D)),
        'ln1_b': bias((L, D)),
        'w_grp': nrm((L, D, N_GROUPS), D),
        'b_grp': bias((L, N_GROUPS)),
        'w_rt': nrm((L, D, N_EXPERTS), D),
        'b_rt': bias((L, N_EXPERTS)),
        'w1': nrm((L, N_EXPERTS, D, D_EXPERT), D),
        'w3': nrm((L, N_EXPERTS, D, D_EXPERT), D),
        'w2': nrm((L, N_EXPERTS, D_EXPERT, D), D_EXPERT, BETA),
        'ln2_g': gain((L, D)),
        'ln2_b': bias((L, D)),
    }


def reference(x, mem, w_mem_kv, w_in, g_cq, g_ckv, g_kidx, b_kidx, w_uq, w_uqi, w_ukv,
              w_up_a, w_up_b, w_up_c, w_gate, b_gate, w_o, ln1_g, ln1_b,
              w_grp, b_grp, w_rt, b_rt, w1, w3, w2, ln2_g, ln2_b):
    B, T, _ = x.shape
    slopes = 2.0 ** (-8.0 * jnp.arange(1, N_ALIBI + 1, dtype=jnp.float32) / N_ALIBI)
    slopes_a, slopes_b = slopes[0::2], slopes[1::2]
    mk, mv = jnp.split(mem @ w_mem_kv, 2, axis=-1)
    mk = mk.reshape(B, N_MEM, N_HEADS_C, HEAD_DIM)
    mv = mv.reshape(B, N_MEM, N_HEADS_C, HEAD_DIM)
    for l in range(DEPTH):
        h = x
        c_q, c_kv, k_i, w_i, q_b, k_b, v_b, q_c = split_cols(h @ w_in[l])
        c_q = rms_norm(c_q, g_cq[l])
        c_kv = rms_norm(c_kv, g_ckv[l])
        q_a = (c_q @ w_uq[l]).reshape(B, T, N_HEADS_A, HEAD_DIM)
        q_i = (c_q @ w_uqi[l]).reshape(B, T, N_IDX_HEADS, IDX_DIM) * IDX_DIM ** -0.5
        k_a, v_a = jnp.split(c_kv @ w_ukv[l], 2, axis=-1)
        k_a = k_a.reshape(B, T, N_HEADS_A, HEAD_DIM)
        v_a = v_a.reshape(B, T, N_HEADS_A, HEAD_DIM)
        k_i = layer_norm(k_i, g_kidx[l], b_kidx[l])
        w_i = w_i * N_IDX_HEADS ** -0.5
        o_a = dsa_attention(q_a, k_a, v_a, q_i, k_i, w_i, slopes_a)
        o_b = moba_attention(q_b.reshape(B, T, N_HEADS_B, HEAD_DIM), k_b.reshape(B, T, N_HEADS_B, HEAD_DIM),
                             v_b.reshape(B, T, N_HEADS_B, HEAD_DIM), slopes_b)
        o_c = memory_attention(q_c.reshape(B, T, N_HEADS_C, HEAD_DIM), mk, mv)
        g_a, g_b, g_c = jnp.split(jax.nn.sigmoid(h @ w_gate[l] + b_gate[l]), 3, axis=-1)
        merged = g_a * (o_a @ w_up_a[l]) + g_b * (o_b @ w_up_b[l]) + g_c * (o_c @ w_up_c[l])
        x = layer_norm(ALPHA * x + merged @ w_o[l], ln1_g[l], ln1_b[l])
        y = hier_moe(x, w_grp[l], b_grp[l], w_rt[l], b_rt[l], w1[l], w3[l], w2[l])
        x = layer_norm(ALPHA * x + y, ln2_g[l], ln2_b[l])
    return x
```

```python
import functools

import jax
import jax.numpy as jnp
from jax import lax
from jax.experimental import pallas as pl
from jax.experimental.pallas import tpu as pltpu

HEAD_DIM = 128
N_HEADS_A = 6
N_HEADS_B = 6
N_HEADS_C = 4
W_A = N_HEADS_A * HEAD_DIM
W_B = N_HEADS_B * HEAD_DIM
W_C = N_HEADS_C * HEAD_DIM
Q_RANK_A = 512
KV_RANK_A = 256
N_IDX_HEADS = 16
IDX_DIM = 64
TOPK_A_MAX = 256
MOBA_BLOCK = 256
MOBA_TOPK = 3
N_ALIBI = N_HEADS_A + N_HEADS_B
N_GROUPS = 4
EXPERTS_PER_GROUP = 8
N_EXPERTS = N_GROUPS * EXPERTS_PER_GROUP
D_EXPERT = 512
NORM_EPS = 1e-5
NEG = -1e30
INT_MIN = -(2 ** 31)

LANES = 128
SUBLANES = 8
TQ = 256
CK = 256
ROUTE_LANES = 128
MOE_TM = 256
VMEM_LIMIT = 56 * 1024 * 1024

F32 = jnp.float32
BF16 = jnp.bfloat16
I32 = jnp.int32

_NT = (((1,), (1,)), ((), ()))


def _cparams(*sem):
    return pltpu.CompilerParams(dimension_semantics=sem, vmem_limit_bytes=VMEM_LIMIT)


def _resident(shape):
    nd = len(shape)
    return pl.BlockSpec(shape, lambda *_: (0,) * nd, pipeline_mode=pl.Buffered(1))


def _mm_kernel(*refs, has_norm, has_bias):
    x_ref, w_ref = refs[0], refs[1]
    k = 2
    x = x_ref[...]
    if has_norm:
        g_ref = refs[k]
        k += 1
        xf = x.astype(F32)
        x = xf * lax.rsqrt(jnp.mean(xf * xf, axis=-1, keepdims=True) + NORM_EPS) * g_ref[...]
    acc = jnp.dot(x.astype(BF16), w_ref[...], preferred_element_type=F32)
    if has_bias:
        acc = acc + refs[k][...]
        k += 1
    o_ref = refs[k]
    o_ref[...] = acc.astype(o_ref.dtype)


def _mm(x, w, *, out_dtype, tm, tn, norm_g=None, bias=None):
    m, kdim = x.shape
    n = w.shape[1]
    assert m % tm == 0 and n % tn == 0
    in_specs = [pl.BlockSpec((tm, kdim), lambda i, j: (i, 0)),
                pl.BlockSpec((kdim, tn), lambda i, j: (0, j))]
    args = [x, w]
    if norm_g is not None:
        in_specs.append(pl.BlockSpec((1, kdim), lambda i, j: (0, 0)))
        args.append(norm_g.reshape(1, kdim).astype(F32))
    if bias is not None:
        in_specs.append(pl.BlockSpec((1, tn), lambda i, j: (0, j)))
        args.append(bias.reshape(1, n).astype(F32))
    return pl.pallas_call(
        functools.partial(_mm_kernel, has_norm=norm_g is not None, has_bias=bias is not None),
        grid=(m // tm, n // tn),
        in_specs=in_specs,
        out_specs=pl.BlockSpec((tm, tn), lambda i, j: (i, j)),
        out_shape=jax.ShapeDtypeStruct((m, n), out_dtype),
        compiler_params=_cparams("parallel", "parallel"),
        name="mm",
    )(*args)


def _kidx_kernel(x_ref, g_ref, b_ref, o_ref):
    x = x_ref[...]
    mu = jnp.mean(x, axis=-1, keepdims=True)
    xc = x - mu
    var = jnp.mean(xc * xc, axis=-1, keepdims=True)
    o_ref[...] = (xc * lax.rsqrt(var + NORM_EPS) * g_ref[...] + b_ref[...]).astype(o_ref.dtype)


def _kidx_norm(k_i, g, b):
    t, d = k_i.shape
    tm = min(t, 1024)
    return pl.pallas_call(
        _kidx_kernel,
        grid=(t // tm,),
        in_specs=[pl.BlockSpec((tm, d), lambda i: (i, 0)),
                  pl.BlockSpec((1, d), lambda i: (0, 0)),
                  pl.BlockSpec((1, d), lambda i: (0, 0))],
        out_specs=pl.BlockSpec((tm, d), lambda i: (i, 0)),
        out_shape=jax.ShapeDtypeStruct((t, d), BF16),
        compiler_params=_cparams("parallel"),
        name="kidx_norm",
    )(k_i, g.reshape(1, d), b.reshape(1, d))


def _attn_step(q_h, k_blk, vt_blk, bias, mask, carry):
    m, l, acc = carry
    s = lax.dot_general(k_blk, q_h, _NT, preferred_element_type=F32)
    s = jnp.where(mask, s + bias, NEG)
    m_new = jnp.maximum(m, jnp.max(s, axis=0, keepdims=True))
    alpha = jnp.exp(m - m_new)
    p = jnp.exp(s - m_new)
    l = alpha * l + jnp.sum(p, axis=0, keepdims=True)
    acc = alpha * acc + jnp.dot(vt_blk, p.astype(BF16), preferred_element_type=F32)
    return m_new, l, acc


def _attn_init():
    return (jnp.full((1, TQ), NEG, F32), jnp.zeros((1, TQ), F32), jnp.zeros((HEAD_DIM, TQ), F32))


def _attn_finish(carry):
    _, l, acc = carry
    return (acc * (1.0 / l)).T


def _key_offsets():
    return lax.broadcasted_iota(I32, (CK, TQ), 0), lax.broadcasted_iota(I32, (CK, TQ), 1)


def _sortable(x):
    b = pltpu.bitcast(x, I32)
    return b ^ ((b >> 31) & 0x7FFFFFFF)


def _dsa_kernel(slopes_ref, qi_ref, wi_ref, kidx_ref, qa_ref, ka_ref, vat_ref, o_ref, keys_ref, *, top_k):
    i = pl.program_id(0)
    rows, cols = _key_offsets()
    causal = rows <= cols

    def score_chunk(j, diagonal):
        start = pl.multiple_of(j * CK, CK)
        kc = kidx_ref[pl.ds(start, CK), :]
        acc = jnp.zeros((CK, TQ), F32)
        for h in range(N_IDX_HEADS):
            z = lax.dot_general(kc, qi_ref[0, h * TQ:(h + 1) * TQ, :], _NT,
                                preferred_element_type=F32)
            acc = acc + wi_ref[h:h + 1, :] * jnp.maximum(z, 0.0)
        if diagonal:
            acc = jnp.where(causal, acc, NEG)
        keys_ref[pl.ds(start, CK), :] = _sortable(acc)

    def score_body(j, c):
        score_chunk(j, False)
        return c

    lax.fori_loop(0, i, score_body, 0)
    score_chunk(i, True)

    def count_ge(cand):
        def body(j, cnt):
            blk = keys_ref[pl.ds(pl.multiple_of(j * CK, CK), CK), :]
            ge = jnp.where(blk >= cand, 1, 0).astype(I32)
            return cnt + jnp.sum(ge.reshape(CK // SUBLANES, SUBLANES, TQ), axis=0)
        cnt = lax.fori_loop(0, i + 1, body, jnp.zeros((SUBLANES, TQ), I32))
        return jnp.sum(cnt, axis=0, keepdims=True)

    def bit_body(b, thr):
        cand = thr + jnp.left_shift(jnp.int32(1), 31 - b)
        return jnp.where(count_ge(cand) >= top_k, cand, thr)

    thr = lax.fori_loop(0, 32, bit_body, jnp.full((1, TQ), INT_MIN, I32))

    rows_f = rows.astype(F32)
    for h in range(N_HEADS_A):
        q_h = qa_ref[h]
        slope = slopes_ref[h]

        def step(j, carry, diagonal, q_h=q_h, slope=slope, h=h):
            start = pl.multiple_of(j * CK, CK)
            k_blk = ka_ref[h, pl.ds(start, CK), :]
            vt_blk = vat_ref[h, :, pl.ds(start, CK)]
            mask = keys_ref[pl.ds(start, CK), :] >= thr
            if diagonal:
                mask = jnp.logical_and(mask, causal)
            bias = slope * (rows_f + ((j - i) * CK).astype(F32))
            return _attn_step(q_h, k_blk, vt_blk, bias, mask, carry)

        carry = lax.fori_loop(0, i, functools.partial(step, diagonal=False), _attn_init())
        carry = step(i, carry, True)
        o_ref[h] = _attn_finish(carry).astype(o_ref.dtype)


def _dsa(slopes, qi_r, wi_t, kidx, qa, ka, vat, top_k):
    h, t, dh = qa.shape
    nq = t // TQ
    return pl.pallas_call(
        functools.partial(_dsa_kernel, top_k=top_k),
        grid_spec=pltpu.PrefetchScalarGridSpec(
            num_scalar_prefetch=0,
            grid=(nq,),
            in_specs=[
                pl.BlockSpec(memory_space=pltpu.SMEM),
                pl.BlockSpec((1, N_IDX_HEADS * TQ, IDX_DIM), lambda i: (i, 0, 0)),
                pl.BlockSpec((N_IDX_HEADS, TQ), lambda i: (0, i)),
                _resident(kidx.shape),
                pl.BlockSpec((h, TQ, dh), lambda i: (0, i, 0)),
                _resident(ka.shape),
                _resident(vat.shape),
            ],
            out_specs=pl.BlockSpec((h, TQ, dh), lambda i: (0, i, 0)),
            scratch_shapes=[pltpu.VMEM((t, TQ), I32)],
        ),
        out_shape=jax.ShapeDtypeStruct((h, t, dh), BF16),
        compiler_params=_cparams("arbitrary"),
        name="dsa",
    )(slopes, qi_r, wi_t, kidx, qa, ka, vat)


def _moba_kernel(slopes_ref, qb_ref, kb_ref, vbt_ref, o_ref, kmean_ref, sel_ref, *, n_kb):
    i = pl.program_id(0)
    rows, cols = _key_offsets()
    causal = rows <= cols
    rows_f = rows.astype(F32)

    @pl.when(i == 0)
    def _():
        for h in range(N_HEADS_B):
            for n in range(n_kb):
                blk = kb_ref[h, n * CK:(n + 1) * CK, :].astype(F32)
                kmean_ref[h, n:n + 1, :] = jnp.mean(blk, axis=0, keepdims=True)

    blk_id = lax.broadcasted_iota(I32, (n_kb, TQ), 0)
    for h in range(N_HEADS_B):
        q_h = qb_ref[h]
        slope = slopes_ref[h]

        km = kmean_ref[h]
        km_hi = km.astype(BF16)
        km_lo = (km - km_hi.astype(F32)).astype(BF16)
        gate = (lax.dot_general(km_hi, q_h, _NT, preferred_element_type=F32)
                + lax.dot_general(km_lo, q_h, _NT, preferred_element_type=F32))
        gate = jnp.where(blk_id < i, gate, NEG)
        sel = jnp.zeros((n_kb, TQ), F32)
        for _ in range(MOBA_TOPK):
            best = jnp.max(gate, axis=0, keepdims=True)
            first = jnp.min(jnp.where(gate == best, blk_id, n_kb), axis=0, keepdims=True)
            pick = blk_id == first
            sel = jnp.where(pick, 1.0, sel)
            gate = jnp.where(pick, -jnp.inf, gate)
        sel_ref[h] = jnp.where(blk_id < i, sel, 0.0)

        def step(j, carry, diagonal, q_h=q_h, slope=slope, h=h):
            start = pl.multiple_of(j * CK, CK)
            k_blk = kb_ref[h, pl.ds(start, CK), :]
            vt_blk = vbt_ref[h, :, pl.ds(start, CK)]
            if diagonal:
                mask = causal
            else:
                mask = jnp.broadcast_to(sel_ref[h, pl.ds(j, 1), :], (CK, TQ)) > 0.5
            bias = slope * (rows_f + ((j - i) * CK).astype(F32))
            return _attn_step(q_h, k_blk, vt_blk, bias, mask, carry)

        carry = lax.fori_loop(0, i, functools.partial(step, diagonal=False), _attn_init())
        carry = step(i, carry, True)
        o_ref[h] = _attn_finish(carry).astype(o_ref.dtype)


def _moba(slopes, qb, kb, vbt):
    h, t, dh = qb.shape
    assert t % MOBA_BLOCK == 0 and TQ == MOBA_BLOCK
    n_kb = t // MOBA_BLOCK
    return pl.pallas_call(
        functools.partial(_moba_kernel, n_kb=n_kb),
        grid_spec=pltpu.PrefetchScalarGridSpec(
            num_scalar_prefetch=0,
            grid=(t // TQ,),
            in_specs=[
                pl.BlockSpec(memory_space=pltpu.SMEM),
                pl.BlockSpec((h, TQ, dh), lambda i: (0, i, 0)),
                _resident(kb.shape),
                _resident(vbt.shape),
            ],
            out_specs=pl.BlockSpec((h, TQ, dh), lambda i: (0, i, 0)),
            scratch_shapes=[pltpu.VMEM((h, n_kb, dh), F32), pltpu.VMEM((h, n_kb, TQ), F32)],
        ),
        out_shape=jax.ShapeDtypeStruct((h, t, dh), BF16),
        compiler_params=_cparams("arbitrary"),
        name="moba",
    )(slopes, qb, kb, vbt)


def _mem_kernel(qc_ref, mk_ref, mvt_ref, o_ref):
    for h in range(N_HEADS_C):
        s = lax.dot_general(mk_ref[h], qc_ref[h], _NT, preferred_element_type=F32)
        m = jnp.max(s, axis=0, keepdims=True)
        p = jnp.exp(s - m)
        l = jnp.sum(p, axis=0, keepdims=True)
        acc = jnp.dot(mvt_ref[h], p.astype(BF16), preferred_element_type=F32)
        o_ref[h] = (acc * (1.0 / l)).T.astype(o_ref.dtype)


def _mem_attn(qc, mk, mvt):
    h, t, dh = qc.shape
    return pl.pallas_call(
        _mem_kernel,
        grid=(t // TQ,),
        in_specs=[pl.BlockSpec((h, TQ, dh), lambda i: (0, i, 0)),
                  _resident(mk.shape), _resident(mvt.shape)],
        out_specs=pl.BlockSpec((h, TQ, dh), lambda i: (0, i, 0)),
        out_shape=jax.ShapeDtypeStruct((h, t, dh), BF16),
        compiler_params=_cparams("parallel"),
        name="mem_attn",
    )(qc, mk, mvt)


def _sigmoid(x):
    return 1.0 / (1.0 + jnp.exp(-x))


def _merge_kernel(x_ref, wga_ref, wgb_ref, wgc_ref, bga_ref, bgb_ref, bgc_ref,
                  oa_ref, ob_ref, oc_ref, wua_ref, wub_ref, wuc_ref, o_ref):
    xb = x_ref[...].astype(BF16)

    def branch(wg_ref, bg_ref, oo_ref, wu_ref):
        g = _sigmoid(jnp.dot(xb, wg_ref[...], preferred_element_type=F32) + bg_ref[...])
        return g * jnp.dot(oo_ref[...], wu_ref[...], preferred_element_type=F32)

    acc = branch(wga_ref, bga_ref, oa_ref, wua_ref)
    acc = acc + branch(wgb_ref, bgb_ref, ob_ref, wub_ref)
    acc = acc + branch(wgc_ref, bgc_ref, oc_ref, wuc_ref)
    o_ref[...] = acc.astype(o_ref.dtype)


def _merge(x, w_gate, b_gate, o_a, o_b, o_c, w_up_a, w_up_b, w_up_c, *, tm, tn):
    t, d = x.shape
    nb = d // tn
    b_gate = b_gate.reshape(1, 3 * d)

    def wg(k):
        return pl.BlockSpec((d, tn), lambda i, j, k=k: (0, j + k * nb))

    def bg(k):
        return pl.BlockSpec((1, tn), lambda i, j, k=k: (0, j + k * nb))

    def act(w):
        return pl.BlockSpec((tm, w), lambda i, j: (i, 0))

    def wu(w):
        return pl.BlockSpec((w, tn), lambda i, j: (0, j))

    return pl.pallas_call(
        _merge_kernel,
        grid=(t // tm, nb),
        in_specs=[pl.BlockSpec((tm, d), lambda i, j: (i, 0)),
                  wg(0), wg(1), wg(2), bg(0), bg(1), bg(2),
                  act(W_A), act(W_B), act(W_C), wu(W_A), wu(W_B), wu(W_C)],
        out_specs=pl.BlockSpec((tm, tn), lambda i, j: (i, j)),
        out_shape=jax.ShapeDtypeStruct((t, d), BF16),
        compiler_params=_cparams("parallel", "parallel"),
        name="merge",
    )(x, w_gate, w_gate, w_gate, b_gate, b_gate, b_gate, o_a, o_b, o_c, w_up_a, w_up_b, w_up_c)


def _layer_norm(y, g, b):
    mu = jnp.mean(y, axis=-1, keepdims=True)
    yc = y - mu
    var = jnp.mean(yc * yc, axis=-1, keepdims=True)
    return yc * lax.rsqrt(var + NORM_EPS) * g + b


def _route(logits):
    lane = lax.broadcasted_iota(I32, logits.shape, 1)
    is_g = lane < N_GROUPS
    gl = jnp.where(is_g, logits, -jnp.inf)
    gmax = jnp.max(gl, axis=-1, keepdims=True)
    g_sel = jnp.min(jnp.where(gl == gmax, lane, ROUTE_LANES), axis=-1, keepdims=True)
    p_g = 1.0 / jnp.sum(jnp.where(is_g, jnp.exp(gl - gmax), 0.0), axis=-1, keepdims=True)
    e_id = lane - N_GROUPS
    in_grp = jnp.logical_and(e_id >= g_sel * EXPERTS_PER_GROUP, e_id < (g_sel + 1) * EXPERTS_PER_GROUP)
    el = jnp.where(in_grp, logits, -jnp.inf)
    emax = jnp.max(el, axis=-1, keepdims=True)
    ex = jnp.where(in_grp, jnp.exp(el - emax), 0.0)
    pe = ex / jnp.sum(ex, axis=-1, keepdims=True)
    pe = jnp.where(in_grp, pe, -1.0)
    p1 = jnp.max(pe, axis=-1, keepdims=True)
    i1 = jnp.min(jnp.where(pe == p1, e_id, ROUTE_LANES), axis=-1, keepdims=True)
    pe2 = jnp.where(e_id == i1, -1.0, pe)
    p2 = jnp.max(pe2, axis=-1, keepdims=True)
    i2 = jnp.min(jnp.where(pe2 == p2, e_id, ROUTE_LANES), axis=-1, keepdims=True)
    denom = p1 + p2
    w1 = p_g * (p1 / denom)
    w2 = p_g * (p2 / denom)
    return jnp.where(lane == 0, w1,
                     jnp.where(lane == 1, w2,
                               jnp.where(lane == 2, i1.astype(F32),
                                         jnp.where(lane == 3, i2.astype(F32), 0.0))))


def _oproj_kernel(x_ref, m_ref, wo_ref, g_ref, b_ref, wr_ref, br_ref, x1_ref, r_ref, *, alpha):
    y = alpha * x_ref[...] + jnp.dot(m_ref[...], wo_ref[...], preferred_element_type=F32)
    x1 = _layer_norm(y, g_ref[...], b_ref[...])
    x1_ref[...] = x1
    logits = jnp.dot(x1, wr_ref[...], preferred_element_type=F32,
                     precision=lax.Precision.HIGHEST) + br_ref[...]
    r_ref[...] = _route(logits)


def _oproj(x, merged, w_o, ln_g, ln_b, w_r, b_r, *, alpha, tm):
    t, d = x.shape
    return pl.pallas_call(
        functools.partial(_oproj_kernel, alpha=alpha),
        grid=(t // tm,),
        in_specs=[pl.BlockSpec((tm, d), lambda i: (i, 0)),
                  pl.BlockSpec((tm, d), lambda i: (i, 0)),
                  _resident(w_o.shape),
                  pl.BlockSpec((1, d), lambda i: (0, 0)),
                  pl.BlockSpec((1, d), lambda i: (0, 0)),
                  _resident(w_r.shape),
                  pl.BlockSpec((1, ROUTE_LANES), lambda i: (0, 0))],
        out_specs=[pl.BlockSpec((tm, d), lambda i: (i, 0)),
                   pl.BlockSpec((tm, ROUTE_LANES), lambda i: (i, 0))],
        out_shape=[jax.ShapeDtypeStruct((t, d), F32),
                   jax.ShapeDtypeStruct((t, ROUTE_LANES), F32)],
        compiler_params=_cparams("parallel"),
        name="oproj",
    )(x, merged, w_o, ln_g.reshape(1, d), ln_b.reshape(1, d), w_r, b_r)


def _row_copy(src_hbm, row, dst, k, sem):
    return pltpu.make_async_copy(src_hbm.at[pl.ds(row, 1), :], dst.at[pl.ds(k, 1), :], sem)


def _gather_kernel(tok_ref, x_hbm, o_ref, buf, sem):
    base = pl.program_id(0) * MOE_TM

    def issue(r, c):
        _row_copy(x_hbm, tok_ref[base + r], buf, r, sem).start()
        return c

    def wait(r, c):
        _row_copy(x_hbm, tok_ref[base + r], buf, r, sem).wait()
        return c

    lax.fori_loop(0, MOE_TM, issue, 0)
    lax.fori_loop(0, MOE_TM, wait, 0)
    o_ref[...] = buf[...].astype(o_ref.dtype)


def _gather_rows(row_token, x1):
    r = row_token.shape[0]
    d = x1.shape[1]
    return pl.pallas_call(
        _gather_kernel,
        grid_spec=pltpu.PrefetchScalarGridSpec(
            num_scalar_prefetch=1,
            grid=(r // MOE_TM,),
            in_specs=[pl.BlockSpec(memory_space=pl.ANY)],
            out_specs=pl.BlockSpec((MOE_TM, d), lambda i, tok: (i, 0)),
            scratch_shapes=[pltpu.VMEM((MOE_TM, d), F32), pltpu.SemaphoreType.DMA(())],
        ),
        out_shape=jax.ShapeDtypeStruct((r, d), BF16),
        compiler_params=_cparams("arbitrary"),
        name="moe_gather",
    )(row_token, x1)


def _expert_kernel(te_ref, nu_ref, xs_ref, rw_ref, w1_ref, w3_ref, w2_ref, o_ref):
    i = pl.program_id(0)

    @pl.when(i < nu_ref[0])
    def _():
        xs = xs_ref[...]
        h1 = jnp.dot(xs, w1_ref[0].astype(BF16), preferred_element_type=F32)
        h3 = jnp.dot(xs, w3_ref[0].astype(BF16), preferred_element_type=F32)
        hid = (h1 * _sigmoid(h1)) * h3
        y = jnp.dot(hid.astype(BF16), w2_ref[0].astype(BF16), preferred_element_type=F32)
        o_ref[...] = rw_ref[...] * y

    @pl.when(i >= nu_ref[0])
    def _():
        o_ref[...] = jnp.zeros_like(o_ref)


def _experts(tile_expert, n_used, xs, row_w, w1, w3, w2):
    r, d = xs.shape
    de = w1.shape[-1]
    return pl.pallas_call(
        _expert_kernel,
        grid_spec=pltpu.PrefetchScalarGridSpec(
            num_scalar_prefetch=2,
            grid=(r // MOE_TM,),
            in_specs=[pl.BlockSpec((MOE_TM, d), lambda i, te, nu: (i, 0)),
                      pl.BlockSpec((MOE_TM, 1), lambda i, te, nu: (i, 0)),
                      pl.BlockSpec((1, d, de), lambda i, te, nu: (te[i], 0, 0)),
                      pl.BlockSpec((1, d, de), lambda i, te, nu: (te[i], 0, 0)),
                      pl.BlockSpec((1, de, d), lambda i, te, nu: (te[i], 0, 0))],
            out_specs=pl.BlockSpec((MOE_TM, d), lambda i, te, nu: (i, 0)),
        ),
        out_shape=jax.ShapeDtypeStruct((r, d), F32),
        compiler_params=_cparams("arbitrary"),
        name="moe_experts",
    )(tile_expert, n_used, xs, row_w, w1, w3, w2)


def _combine_kernel(pos_ref, x1_ref, g_ref, b_ref, ys_hbm, o_ref, buf0, buf1, sem, *, alpha):
    base = pl.program_id(0) * MOE_TM

    def issue(r, c):
        _row_copy(ys_hbm, pos_ref[2 * (base + r)], buf0, r, sem).start()
        _row_copy(ys_hbm, pos_ref[2 * (base + r) + 1], buf1, r, sem).start()
        return c

    def wait(r, c):
        _row_copy(ys_hbm, pos_ref[2 * (base + r)], buf0, r, sem).wait()
        _row_copy(ys_hbm, pos_ref[2 * (base + r) + 1], buf1, r, sem).wait()
        return c

    lax.fori_loop(0, MOE_TM, issue, 0)
    lax.fori_loop(0, MOE_TM, wait, 0)
    y = alpha * x1_ref[...] + (buf0[...] + buf1[...])
    o_ref[...] = _layer_norm(y, g_ref[...], b_ref[...])


def _combine(pos, x1, ln_g, ln_b, ys, *, alpha):
    t, d = x1.shape
    return pl.pallas_call(
        functools.partial(_combine_kernel, alpha=alpha),
        grid_spec=pltpu.PrefetchScalarGridSpec(
            num_scalar_prefetch=1,
            grid=(t // MOE_TM,),
            in_specs=[pl.BlockSpec((MOE_TM, d), lambda i, pos: (i, 0)),
                      pl.BlockSpec((1, d), lambda i, pos: (0, 0)),
                      pl.BlockSpec((1, d), lambda i, pos: (0, 0)),
                      pl.BlockSpec(memory_space=pl.ANY)],
            out_specs=pl.BlockSpec((MOE_TM, d), lambda i, pos: (i, 0)),
            scratch_shapes=[pltpu.VMEM((MOE_TM, d), F32), pltpu.VMEM((MOE_TM, d), F32),
                            pltpu.SemaphoreType.DMA(())],
        ),
        out_shape=jax.ShapeDtypeStruct((t, d), F32),
        compiler_params=_cparams("arbitrary"),
        name="moe_combine",
    )(pos, x1, ln_g.reshape(1, d), ln_b.reshape(1, d), ys)


def _moe_plan(route, t):
    wts = route[:, 0:2].reshape(-1)
    flat_e = route[:, 2:4].astype(I32).reshape(-1)
    n = flat_e.shape[0]
    n_rows = n + N_EXPERTS * MOE_TM
    n_tiles = n_rows // MOE_TM
    onehot = (flat_e[:, None] == jnp.arange(N_EXPERTS, dtype=I32)[None, :]).astype(I32)
    running = jnp.cumsum(onehot, axis=0)
    rank = jnp.sum(onehot * running, axis=1) - 1
    counts = running[-1]
    tiles_per = (counts + MOE_TM - 1) // MOE_TM
    tile_end = jnp.cumsum(tiles_per)
    grp_row0 = (tile_end - tiles_per) * MOE_TM
    pos = jnp.sum(onehot * grp_row0[None, :], axis=1) + rank
    row_token = jnp.zeros((n_rows,), I32).at[pos].set(jnp.arange(n, dtype=I32) // 2)
    row_w = jnp.zeros((n_rows,), F32).at[pos].set(wts)
    n_used = tile_end[-1]
    tile_ids = jnp.minimum(jnp.arange(n_tiles, dtype=I32), n_used - 1)
    tile_expert = jnp.sum((tile_end[None, :] <= tile_ids[:, None]).astype(I32), axis=1)
    tile_expert = jnp.minimum(tile_expert, N_EXPERTS - 1)
    return row_token, row_w.reshape(n_rows, 1), pos, tile_expert, n_used.reshape(1).astype(I32)


def _heads(a, h):
    t = a.shape[0]
    return a.reshape(t, h, HEAD_DIM).transpose(1, 0, 2)


def _heads_t(a, h):
    t = a.shape[0]
    return a.reshape(t, h, HEAD_DIM).transpose(1, 2, 0)


def _unheads(a):
    h, t, dh = a.shape
    return a.transpose(1, 0, 2).reshape(t, h * dh)


def kernel(x, mem, w_mem_kv, w_in, g_cq, g_ckv, g_kidx, b_kidx, w_uq, w_uqi, w_ukv,
           w_up_a, w_up_b, w_up_c, w_gate, b_gate, w_o, ln1_g, ln1_b,
           w_grp, b_grp, w_rt, b_rt, w1, w3, w2, ln2_g, ln2_b):
    bsz, t, d = x.shape
    assert bsz == 1 and t % TQ == 0
    depth = w_in.shape[0]
    alpha = (2 * depth) ** 0.25
    scale = HEAD_DIM ** -0.5
    top_k = min(TOPK_A_MAX, t // 4)
    n_mem = mem.shape[1]
    tm = min(t, 512)

    slopes = 2.0 ** (-8.0 * jnp.arange(1, N_ALIBI + 1, dtype=F32) / N_ALIBI)
    slopes_a, slopes_b = slopes[0::2], slopes[1::2]

    mkv = _mm(mem[0], w_mem_kv.astype(BF16), out_dtype=BF16, tm=n_mem, tn=W_C)
    mk = _heads(mkv[:, :W_C], N_HEADS_C)
    mvt = _heads_t(mkv[:, W_C:], N_HEADS_C)

    xl = x[0]
    for l in range(depth):
        wl = w_in[l]
        o = 0
        parts = []
        for width in (Q_RANK_A, KV_RANK_A, IDX_DIM, N_IDX_HEADS, W_B, W_B, W_B, W_C):
            parts.append(wl[:, o:o + width])
            o += width
        w_cq, w_ckv, w_ki, w_wi, w_qb, w_kb, w_vb, w_qc = parts
        n_small = Q_RANK_A + KV_RANK_A + IDX_DIM + N_IDX_HEADS
        pad = (-n_small) % LANES
        w_p1 = jnp.concatenate([w_cq, w_ckv, w_ki, w_wi * N_IDX_HEADS ** -0.5,
                                jnp.zeros((d, pad), F32)], axis=1).astype(BF16)
        w_p2 = jnp.concatenate([w_qb * scale, w_kb, w_vb, w_qc * scale], axis=1).astype(BF16)
        p1 = _mm(xl, w_p1, out_dtype=F32, tm=tm, tn=w_p1.shape[1])
        p2 = _mm(xl, w_p2, out_dtype=BF16, tm=tm, tn=w_p2.shape[1] // 2)
        c_q = p1[:, :Q_RANK_A]
        c_kv = p1[:, Q_RANK_A:Q_RANK_A + KV_RANK_A]
        k_i = p1[:, Q_RANK_A + KV_RANK_A:Q_RANK_A + KV_RANK_A + IDX_DIM]
        w_i = p1[:, Q_RANK_A + KV_RANK_A + IDX_DIM:n_small]
        q_b, k_b = p2[:, :W_B], p2[:, W_B:2 * W_B]
        v_b, q_c = p2[:, 2 * W_B:3 * W_B], p2[:, 3 * W_B:]

        w_q = jnp.concatenate([w_uq[l] * scale, w_uqi[l] * IDX_DIM ** -0.5], axis=1).astype(BF16)
        qq = _mm(c_q, w_q, out_dtype=BF16, tm=tm, tn=w_q.shape[1], norm_g=g_cq[l])
        kv = _mm(c_kv, w_ukv[l].astype(BF16), out_dtype=BF16, tm=tm, tn=2 * W_A, norm_g=g_ckv[l])
        q_a, q_i = qq[:, :W_A], qq[:, W_A:]
        k_a, v_a = kv[:, :W_A], kv[:, W_A:]
        kidx = _kidx_norm(k_i, g_kidx[l], b_kidx[l])
        qi_r = (q_i.reshape(t // TQ, TQ, N_IDX_HEADS, IDX_DIM).transpose(0, 2, 1, 3)
                .reshape(t // TQ, N_IDX_HEADS * TQ, IDX_DIM))
        o_a = _dsa(slopes_a, qi_r, w_i.T, kidx, _heads(q_a, N_HEADS_A), _heads(k_a, N_HEADS_A),
                   _heads_t(v_a, N_HEADS_A), top_k)
        o_b = _moba(slopes_b, _heads(q_b, N_HEADS_B), _heads(k_b, N_HEADS_B), _heads_t(v_b, N_HEADS_B))
        o_c = _mem_attn(_heads(q_c, N_HEADS_C), mk, mvt)

        merged = _merge(xl, w_gate[l].astype(BF16), b_gate[l], _unheads(o_a), _unheads(o_b), _unheads(o_c),
                        w_up_a[l].astype(BF16), w_up_b[l].astype(BF16), w_up_c[l].astype(BF16),
                        tm=min(t, 1024), tn=512)

        w_r = jnp.concatenate([w_grp[l], w_rt[l],
                               jnp.zeros((d, ROUTE_LANES - N_GROUPS - N_EXPERTS), F32)], axis=1)
        b_r = jnp.concatenate([b_grp[l], b_rt[l],
                               jnp.zeros((ROUTE_LANES - N_GROUPS - N_EXPERTS,), F32)]).reshape(1, ROUTE_LANES)
        x1, route = _oproj(xl, merged, w_o[l].astype(BF16), ln1_g[l], ln1_b[l], w_r, b_r,
                           alpha=alpha, tm=min(t, 256))

        row_token, row_w, pos, tile_expert, n_used = _moe_plan(route, t)
        xs = _gather_rows(row_token, x1)
        ys = _experts(tile_expert, n_used, xs, row_w, w1[l], w3[l], w2[l])
        xl = _combine(pos, x1, ln2_g[l], ln2_b[l], ys, alpha=alpha)
    return xl[None]
```

```python
import functools

import jax
import jax.numpy as jnp
from jax import lax
from jax.experimental import pallas as pl
from jax.experimental.pallas import tpu as pltpu

HEAD_DIM = 128
N_HEADS_A = 6
N_HEADS_B = 6
N_HEADS_C = 4
W_A = N_HEADS_A * HEAD_DIM
W_B = N_HEADS_B * HEAD_DIM
W_C = N_HEADS_C * HEAD_DIM
Q_RANK_A = 512
KV_RANK_A = 256
N_IDX_HEADS = 16
IDX_DIM = 64
TOPK_A_MAX = 256
MOBA_BLOCK = 256
MOBA_TOPK = 3
N_ALIBI = N_HEADS_A + N_HEADS_B
N_GROUPS = 4
EXPERTS_PER_GROUP = 8
N_EXPERTS = N_GROUPS * EXPERTS_PER_GROUP
D_EXPERT = 512
NORM_EPS = 1e-5
NEG = -1e30
LOG2E = 1.4426950408889634
INT_MIN = -(2 ** 31)

LANES = 128
SUBLANES = 8
TQ = 256
CK = 256
ROUTE_LANES = 128
MOE_TM = 256
VMEM_LIMIT = 56 * 1024 * 1024

F32 = jnp.float32
BF16 = jnp.bfloat16
I32 = jnp.int32

_NT = (((1,), (1,)), ((), ()))


def _cparams(*sem):
    return pltpu.CompilerParams(dimension_semantics=sem, vmem_limit_bytes=VMEM_LIMIT)


def _resident(shape):
    nd = len(shape)
    return pl.BlockSpec(shape, lambda *_: (0,) * nd, pipeline_mode=pl.Buffered(1))


def _mm_kernel(*refs, has_norm, has_bias):
    x_ref, w_ref = refs[0], refs[1]
    k = 2
    x = x_ref[...]
    if has_norm:
        g_ref = refs[k]
        k += 1
        xf = x.astype(F32)
        x = xf * lax.rsqrt(jnp.mean(xf * xf, axis=-1, keepdims=True) + NORM_EPS) * g_ref[...]
    acc = jnp.dot(x.astype(BF16), w_ref[...], preferred_element_type=F32)
    if has_bias:
        acc = acc + refs[k][...]
        k += 1
    o_ref = refs[k]
    o_ref[...] = acc.astype(o_ref.dtype)


def _mm(x, w, *, out_dtype, tm, tn, norm_g=None, bias=None):
    m, kdim = x.shape
    n = w.shape[1]
    assert m % tm == 0 and n % tn == 0
    in_specs = [pl.BlockSpec((tm, kdim), lambda i, j: (i, 0)),
                pl.BlockSpec((kdim, tn), lambda i, j: (0, j))]
    args = [x, w]
    if norm_g is not None:
        in_specs.append(pl.BlockSpec((1, kdim), lambda i, j: (0, 0)))
        args.append(norm_g.reshape(1, kdim).astype(F32))
    if bias is not None:
        in_specs.append(pl.BlockSpec((1, tn), lambda i, j: (0, j)))
        args.append(bias.reshape(1, n).astype(F32))
    return pl.pallas_call(
        functools.partial(_mm_kernel, has_norm=norm_g is not None, has_bias=bias is not None),
        grid=(m // tm, n // tn),
        in_specs=in_specs,
        out_specs=pl.BlockSpec((tm, tn), lambda i, j: (i, j)),
        out_shape=jax.ShapeDtypeStruct((m, n), out_dtype),
        compiler_params=_cparams("parallel", "parallel"),
        name="mm",
    )(*args)


def _kidx_kernel(x_ref, g_ref, b_ref, o_ref):
    x = x_ref[...]
    mu = jnp.mean(x, axis=-1, keepdims=True)
    xc = x - mu
    var = jnp.mean(xc * xc, axis=-1, keepdims=True)
    o_ref[...] = (xc * lax.rsqrt(var + NORM_EPS) * g_ref[...] + b_ref[...]).astype(o_ref.dtype)


def _kidx_norm(k_i, g, b):
    t, d = k_i.shape
    tm = min(t, 1024)
    return pl.pallas_call(
        _kidx_kernel,
        grid=(t // tm,),
        in_specs=[pl.BlockSpec((tm, d), lambda i: (i, 0)),
                  pl.BlockSpec((1, d), lambda i: (0, 0)),
                  pl.BlockSpec((1, d), lambda i: (0, 0))],
        out_specs=pl.BlockSpec((tm, d), lambda i: (i, 0)),
        out_shape=jax.ShapeDtypeStruct((t, d), BF16),
        compiler_params=_cparams("parallel"),
        name="kidx_norm",
    )(k_i, g.reshape(1, d), b.reshape(1, d))


def _head(h):
    return slice(h * HEAD_DIM, (h + 1) * HEAD_DIM)


def _attn_state(n_heads):
    return ([pltpu.VMEM((2, n_heads, CK, TQ), F32)]
            + [pltpu.VMEM((HEAD_DIM, TQ), F32) for _ in range(n_heads)]
            + [pltpu.VMEM((1, TQ), F32) for _ in range(2 * n_heads)])


def _attn_reset(state, n_heads):
    state = state[1:]
    for h in range(n_heads):
        state[h][...] = jnp.zeros((HEAD_DIM, TQ), F32)
        state[n_heads + h][...] = jnp.full((1, TQ), NEG, F32)
        state[2 * n_heads + h][...] = jnp.zeros((1, TQ), F32)


def _attn_run(state, n_heads, i, q_ref, k_ref, consume):
    s_ref = state[0]

    def scores(j, slot):
        start = pl.multiple_of(j * CK, CK)
        for h in range(n_heads):
            s_ref[slot, h] = lax.dot_general(k_ref[pl.ds(start, CK), _head(h)], q_ref[:, _head(h)], _NT,
                                             preferred_element_type=F32)

    scores(0, 0)

    def body(j, c):
        slot = lax.rem(j, 2)
        scores(j + 1, 1 - slot)
        consume(j, slot, False)
        return c

    lax.fori_loop(0, i, body, 0)
    consume(i, lax.rem(i, 2), True)


def _attn_update(state, n_heads, h, slot, vt_blk, t_bias, c):
    s_ref, state = state[0], state[1:]
    acc_ref, m_ref, l_ref = state[h], state[n_heads + h], state[2 * n_heads + h]
    t = s_ref[slot, h] + t_bias
    m_old = m_ref[...]
    m_new = jnp.maximum(m_old, jnp.max(t, axis=0, keepdims=True) + c)
    alpha = jnp.exp2(m_old - m_new)
    p = jnp.exp2(t - (m_new - c))
    l_ref[...] = alpha * l_ref[...] + jnp.sum(p, axis=0, keepdims=True)
    acc_ref[...] = alpha * acc_ref[...] + jnp.dot(vt_blk, p.astype(BF16), preferred_element_type=F32)
    m_ref[...] = m_new


def _attn_finish(state, n_heads, h):
    state = state[1:]
    return (state[h][...] * (1.0 / state[2 * n_heads + h][...])).T


def _key_offsets():
    return lax.broadcasted_iota(I32, (CK, TQ), 0), lax.broadcasted_iota(I32, (CK, TQ), 1)


def _sortable(x):
    b = pltpu.bitcast(x, I32)
    return b ^ ((b >> 31) & 0x7FFFFFFF)


def _dsa_kernel(slopes_ref, qi_ref, wi_ref, kidx_ref, qa_ref, ka_ref, vat_ref, o_ref,
                keys_ref, sb_ref, *state, top_k):
    i = pl.program_id(0)
    rows, cols = _key_offsets()
    causal = rows <= cols
    nh = N_HEADS_A

    @pl.when(i == 0)
    def _():
        rows_f = rows.astype(F32)
        for h in range(nh):
            sb_ref[h] = slopes_ref[h] * rows_f

    def score_chunk(j, diagonal):
        start = pl.multiple_of(j * CK, CK)
        kc = kidx_ref[pl.ds(start, CK), :]
        acc = jnp.zeros((CK, TQ), F32)
        for h in range(N_IDX_HEADS):
            z = lax.dot_general(kc, qi_ref[0, h * TQ:(h + 1) * TQ, :], _NT,
                                preferred_element_type=F32)
            acc = acc + wi_ref[h:h + 1, :] * jnp.maximum(z, 0.0)
        if diagonal:
            acc = jnp.where(causal, acc, NEG)
        keys_ref[pl.ds(start, CK), :] = _sortable(acc)

    def score_body(j, c):
        score_chunk(j, False)
        return c

    lax.fori_loop(0, i, score_body, 0)
    score_chunk(i, True)

    def count_ge(cand):
        def body(j, cnt):
            blk = keys_ref[pl.ds(pl.multiple_of(j * CK, CK), CK), :]
            ge = jnp.where(blk >= cand, 1, 0).astype(I32)
            return cnt + jnp.sum(ge.reshape(CK // SUBLANES, SUBLANES, TQ), axis=0)
        cnt = lax.fori_loop(0, i + 1, body, jnp.zeros((SUBLANES, TQ), I32))
        return jnp.sum(cnt, axis=0, keepdims=True)

    def bit_body(b, thr):
        cand = thr + jnp.left_shift(jnp.int32(1), 31 - b)
        return jnp.where(count_ge(cand) >= top_k, cand, thr)

    thr = lax.fori_loop(0, 32, bit_body, jnp.full((1, TQ), INT_MIN, I32))

    _attn_reset(state, nh)

    def consume(j, slot, diagonal):
        start = pl.multiple_of(j * CK, CK)
        sel = keys_ref[pl.ds(start, CK), :] >= thr
        if diagonal:
            sel = jnp.logical_and(sel, causal)
        mask_bias = jnp.where(sel, 0.0, NEG)
        off = ((j - i) * CK).astype(F32)
        for h in range(nh):
            _attn_update(state, nh, h, slot, vat_ref[_head(h), pl.ds(start, CK)],
                         sb_ref[h] + mask_bias, slopes_ref[h] * off)

    _attn_run(state, nh, i, qa_ref, ka_ref, consume)
    for h in range(nh):
        o_ref[:, _head(h)] = _attn_finish(state, nh, h).astype(o_ref.dtype)


def _dsa(slopes, qi_r, wi_t, kidx, qa, ka, vat, top_k):
    t, w = qa.shape
    nh = w // HEAD_DIM
    return pl.pallas_call(
        functools.partial(_dsa_kernel, top_k=top_k),
        grid_spec=pltpu.PrefetchScalarGridSpec(
            num_scalar_prefetch=0,
            grid=(t // TQ,),
            in_specs=[
                pl.BlockSpec(memory_space=pltpu.SMEM),
                pl.BlockSpec((1, N_IDX_HEADS * TQ, IDX_DIM), lambda i: (i, 0, 0)),
                pl.BlockSpec((N_IDX_HEADS, TQ), lambda i: (0, i)),
                _resident(kidx.shape),
                pl.BlockSpec((TQ, w), lambda i: (i, 0)),
                _resident(ka.shape),
                _resident(vat.shape),
            ],
            out_specs=pl.BlockSpec((TQ, w), lambda i: (i, 0)),
            scratch_shapes=[pltpu.VMEM((t, TQ), I32), pltpu.VMEM((nh, CK, TQ), F32)] + _attn_state(nh),
        ),
        out_shape=jax.ShapeDtypeStruct((t, w), BF16),
        compiler_params=_cparams("arbitrary"),
        name="dsa",
    )(slopes, qi_r, wi_t, kidx, qa, ka, vat)


def _moba_kernel(slopes_ref, qb_ref, kb_ref, vbt_ref, o_ref, kmean_ref, sel_ref, sb_ref, *state, n_kb):
    i = pl.program_id(0)
    rows, cols = _key_offsets()
    nh = N_HEADS_B

    @pl.when(i == 0)
    def _():
        rows_f = rows.astype(F32)
        for h in range(nh):
            sb_ref[h] = slopes_ref[h] * rows_f
            for n in range(n_kb):
                blk = kb_ref[n * CK:(n + 1) * CK, _head(h)].astype(F32)
                kmean_ref[h, n:n + 1, :] = jnp.mean(blk, axis=0, keepdims=True)

    blk_id = lax.broadcasted_iota(I32, (n_kb, TQ), 0)
    for h in range(nh):
        q_h = qb_ref[:, _head(h)]
        km = kmean_ref[h]
        km_hi = km.astype(BF16)
        km_lo = (km - km_hi.astype(F32)).astype(BF16)
        gate = (lax.dot_general(km_hi, q_h, _NT, preferred_element_type=F32)
                + lax.dot_general(km_lo, q_h, _NT, preferred_element_type=F32))
        gate = jnp.where(blk_id < i, gate, NEG)
        sel = jnp.full((n_kb, TQ), NEG, F32)
        for _ in range(MOBA_TOPK):
            best = jnp.max(gate, axis=0, keepdims=True)
            first = jnp.min(jnp.where(gate == best, blk_id, n_kb), axis=0, keepdims=True)
            pick = blk_id == first
            sel = jnp.where(pick, 0.0, sel)
            gate = jnp.where(pick, -jnp.inf, gate)
        sel_ref[h] = jnp.where(blk_id < i, sel, NEG)

    _attn_reset(state, nh)

    def consume(j, slot, diagonal):
        start = pl.multiple_of(j * CK, CK)
        off = ((j - i) * CK).astype(F32)
        for h in range(nh):
            if diagonal:
                t_bias = jnp.where(rows <= cols, sb_ref[h], NEG)
            else:
                t_bias = sb_ref[h] + sel_ref[h, pl.ds(j, 1), :]
            _attn_update(state, nh, h, slot, vbt_ref[_head(h), pl.ds(start, CK)],
                         t_bias, slopes_ref[h] * off)

    _attn_run(state, nh, i, qb_ref, kb_ref, consume)
    for h in range(nh):
        o_ref[:, _head(h)] = _attn_finish(state, nh, h).astype(o_ref.dtype)


def _moba(slopes, qb, kb, vbt):
    t, w = qb.shape
    nh = w // HEAD_DIM
    assert t % MOBA_BLOCK == 0 and TQ == MOBA_BLOCK and CK == MOBA_BLOCK
    n_kb = t // MOBA_BLOCK
    return pl.pallas_call(
        functools.partial(_moba_kernel, n_kb=n_kb),
        grid_spec=pltpu.PrefetchScalarGridSpec(
            num_scalar_prefetch=0,
            grid=(t // TQ,),
            in_specs=[
                pl.BlockSpec(memory_space=pltpu.SMEM),
                pl.BlockSpec((TQ, w), lambda i: (i, 0)),
                _resident(kb.shape),
                _resident(vbt.shape),
            ],
            out_specs=pl.BlockSpec((TQ, w), lambda i: (i, 0)),
            scratch_shapes=[pltpu.VMEM((nh, n_kb, HEAD_DIM), F32), pltpu.VMEM((nh, n_kb, TQ), F32),
                            pltpu.VMEM((nh, CK, TQ), F32)] + _attn_state(nh),
        ),
        out_shape=jax.ShapeDtypeStruct((t, w), BF16),
        compiler_params=_cparams("arbitrary"),
        name="moba",
    )(slopes, qb, kb, vbt)


def _mem_kernel(qc_ref, mk_ref, mvt_ref, o_ref):
    for h in range(N_HEADS_C):
        s = lax.dot_general(mk_ref[:, _head(h)], qc_ref[:, _head(h)], _NT,
                            preferred_element_type=F32)
        m = jnp.max(s, axis=0, keepdims=True)
        p = jnp.exp2(s - m)
        l = jnp.sum(p, axis=0, keepdims=True)
        acc = jnp.dot(mvt_ref[_head(h), :], p.astype(BF16), preferred_element_type=F32)
        o_ref[:, _head(h)] = (acc * (1.0 / l)).T.astype(o_ref.dtype)


def _mem_attn(qc, mk, mvt):
    t, w = qc.shape
    return pl.pallas_call(
        _mem_kernel,
        grid=(t // TQ,),
        in_specs=[pl.BlockSpec((TQ, w), lambda i: (i, 0)),
                  _resident(mk.shape), _resident(mvt.shape)],
        out_specs=pl.BlockSpec((TQ, w), lambda i: (i, 0)),
        out_shape=jax.ShapeDtypeStruct((t, w), BF16),
        compiler_params=_cparams("parallel"),
        name="mem_attn",
    )(qc, mk, mvt)


def _sigmoid(x):
    return 1.0 / (1.0 + jnp.exp(-x))


def _merge_kernel(x_ref, wga_ref, wgb_ref, wgc_ref, bga_ref, bgb_ref, bgc_ref,
                  oa_ref, ob_ref, oc_ref, wua_ref, wub_ref, wuc_ref, o_ref):
    xb = x_ref[...].astype(BF16)

    def branch(wg_ref, bg_ref, oo_ref, wu_ref):
        g = _sigmoid(jnp.dot(xb, wg_ref[...], preferred_element_type=F32) + bg_ref[...])
        return g * jnp.dot(oo_ref[...], wu_ref[...], preferred_element_type=F32)

    acc = branch(wga_ref, bga_ref, oa_ref, wua_ref)
    acc = acc + branch(wgb_ref, bgb_ref, ob_ref, wub_ref)
    acc = acc + branch(wgc_ref, bgc_ref, oc_ref, wuc_ref)
    o_ref[...] = acc.astype(o_ref.dtype)


def _merge(x, w_gate, b_gate, o_a, o_b, o_c, w_up_a, w_up_b, w_up_c, *, tm, tn):
    t, d = x.shape
    nb = d // tn
    b_gate = b_gate.reshape(1, 3 * d)

    def wg(k):
        return pl.BlockSpec((d, tn), lambda i, j, k=k: (0, j + k * nb))

    def bg(k):
        return pl.BlockSpec((1, tn), lambda i, j, k=k: (0, j + k * nb))

    def act(w):
        return pl.BlockSpec((tm, w), lambda i, j: (i, 0))

    def wu(w):
        return pl.BlockSpec((w, tn), lambda i, j: (0, j))

    return pl.pallas_call(
        _merge_kernel,
        grid=(t // tm, nb),
        in_specs=[pl.BlockSpec((tm, d), lambda i, j: (i, 0)),
                  wg(0), wg(1), wg(2), bg(0), bg(1), bg(2),
                  act(W_A), act(W_B), act(W_C), wu(W_A), wu(W_B), wu(W_C)],
        out_specs=pl.BlockSpec((tm, tn), lambda i, j: (i, j)),
        out_shape=jax.ShapeDtypeStruct((t, d), BF16),
        compiler_params=_cparams("parallel", "parallel"),
        name="merge",
    )(x, w_gate, w_gate, w_gate, b_gate, b_gate, b_gate, o_a, o_b, o_c, w_up_a, w_up_b, w_up_c)


def _layer_norm(y, g, b):
    mu = jnp.mean(y, axis=-1, keepdims=True)
    yc = y - mu
    var = jnp.mean(yc * yc, axis=-1, keepdims=True)
    return yc * lax.rsqrt(var + NORM_EPS) * g + b


def _route(logits):
    lane = lax.broadcasted_iota(I32, logits.shape, 1)
    is_g = lane < N_GROUPS
    gl = jnp.where(is_g, logits, -jnp.inf)
    gmax = jnp.max(gl, axis=-1, keepdims=True)
    g_sel = jnp.min(jnp.where(gl == gmax, lane, ROUTE_LANES), axis=-1, keepdims=True)
    p_g = 1.0 / jnp.sum(jnp.where(is_g, jnp.exp(gl - gmax), 0.0), axis=-1, keepdims=True)
    e_id = lane - N_GROUPS
    in_grp = jnp.logical_and(e_id >= g_sel * EXPERTS_PER_GROUP, e_id < (g_sel + 1) * EXPERTS_PER_GROUP)
    el = jnp.where(in_grp, logits, -jnp.inf)
    emax = jnp.max(el, axis=-1, keepdims=True)
    ex = jnp.where(in_grp, jnp.exp(el - emax), 0.0)
    pe = ex / jnp.sum(ex, axis=-1, keepdims=True)
    pe = jnp.where(in_grp, pe, -1.0)
    p1 = jnp.max(pe, axis=-1, keepdims=True)
    i1 = jnp.min(jnp.where(pe == p1, e_id, ROUTE_LANES), axis=-1, keepdims=True)
    pe2 = jnp.where(e_id == i1, -1.0, pe)
    p2 = jnp.max(pe2, axis=-1, keepdims=True)
    i2 = jnp.min(jnp.where(pe2 == p2, e_id, ROUTE_LANES), axis=-1, keepdims=True)
    denom = p1 + p2
    w1 = p_g * (p1 / denom)
    w2 = p_g * (p2 / denom)
    return jnp.where(lane == 0, w1,
                     jnp.where(lane == 1, w2,
                               jnp.where(lane == 2, i1.astype(F32),
                                         jnp.where(lane == 3, i2.astype(F32), 0.0))))


def _oproj_kernel(x_ref, m_ref, wo_ref, g_ref, b_ref, wr_ref, br_ref, x1_ref, r_ref, *, alpha):
    y = alpha * x_ref[...] + jnp.dot(m_ref[...], wo_ref[...], preferred_element_type=F32)
    x1 = _layer_norm(y, g_ref[...], b_ref[...])
    x1_ref[...] = x1
    logits = jnp.dot(x1, wr_ref[...], preferred_element_type=F32,
                     precision=lax.Precision.HIGHEST) + br_ref[...]
    r_ref[...] = _route(logits)


def _oproj(x, merged, w_o, ln_g, ln_b, w_r, b_r, *, alpha, tm):
    t, d = x.shape
    return pl.pallas_call(
        functools.partial(_oproj_kernel, alpha=alpha),
        grid=(t // tm,),
        in_specs=[pl.BlockSpec((tm, d), lambda i: (i, 0)),
                  pl.BlockSpec((tm, d), lambda i: (i, 0)),
                  _resident(w_o.shape),
                  pl.BlockSpec((1, d), lambda i: (0, 0)),
                  pl.BlockSpec((1, d), lambda i: (0, 0)),
                  _resident(w_r.shape),
                  pl.BlockSpec((1, ROUTE_LANES), lambda i: (0, 0))],
        out_specs=[pl.BlockSpec((tm, d), lambda i: (i, 0)),
                   pl.BlockSpec((tm, ROUTE_LANES), lambda i: (i, 0))],
        out_shape=[jax.ShapeDtypeStruct((t, d), F32),
                   jax.ShapeDtypeStruct((t, ROUTE_LANES), F32)],
        compiler_params=_cparams("parallel"),
        name="oproj",
    )(x, merged, w_o, ln_g.reshape(1, d), ln_b.reshape(1, d), w_r, b_r)


def _row_copy(src_hbm, row, dst, k, sem):
    return pltpu.make_async_copy(src_hbm.at[pl.ds(row, 1), :], dst.at[pl.ds(k, 1), :], sem)


def _expert_kernel(tok_ref, te_ref, nu_ref, x_hbm, rw_ref, w1_ref, w3_ref, w2_ref, o_ref, buf, sem):
    i = pl.program_id(0)
    n_used = nu_ref[0]
    slot = lax.rem(i, 2)

    def rows(tile, s, go):
        base = tile * MOE_TM

        def body(r, c):
            cp = _row_copy(x_hbm, tok_ref[base + r], buf.at[s], r, sem.at[s])
            if go:
                cp.start()
            else:
                cp.wait()
            return c

        lax.fori_loop(0, MOE_TM, body, 0)

    @pl.when(i == 0)
    def _():
        rows(0, 0, True)

    @pl.when(i < n_used)
    def _():
        rows(i, slot, False)

        @pl.when(i + 1 < n_used)
        def _():
            rows(i + 1, 1 - slot, True)

        xs = buf[slot].astype(BF16)
        h1 = jnp.dot(xs, w1_ref[0, 0].astype(BF16), preferred_element_type=F32)
        h3 = jnp.dot(xs, w3_ref[0, 0].astype(BF16), preferred_element_type=F32)
        hid = (h1 * _sigmoid(h1)) * h3
        y = jnp.dot(hid.astype(BF16), w2_ref[0, 0].astype(BF16), preferred_element_type=F32)
        o_ref[...] = rw_ref[...] * y

    @pl.when(i >= n_used)
    def _():
        o_ref[...] = jnp.zeros_like(o_ref)


def _experts(row_token, tile_expert, n_used, x1, row_w, w1, w3, w2, layer):
    r = row_token.shape[0]
    d = x1.shape[1]
    de = w1.shape[-1]

    def wspec(a, b):
        return pl.BlockSpec((1, 1, a, b), lambda i, tok, te, nu: (layer, te[i], 0, 0))

    return pl.pallas_call(
        _expert_kernel,
        grid_spec=pltpu.PrefetchScalarGridSpec(
            num_scalar_prefetch=3,
            grid=(r // MOE_TM,),
            in_specs=[pl.BlockSpec(memory_space=pl.ANY),
                      pl.BlockSpec((MOE_TM, 1), lambda i, tok, te, nu: (i, 0)),
                      wspec(d, de), wspec(d, de), wspec(de, d)],
            out_specs=pl.BlockSpec((MOE_TM, d), lambda i, tok, te, nu: (i, 0)),
            scratch_shapes=[pltpu.VMEM((2, MOE_TM, d), F32), pltpu.SemaphoreType.DMA((2,))],
        ),
        out_shape=jax.ShapeDtypeStruct((r, d), F32),
        compiler_params=_cparams("arbitrary"),
        name="moe_experts",
    )(row_token, tile_expert, n_used, x1, row_w, w1, w3, w2)


def _combine_kernel(pos_ref, x1_ref, g_ref, b_ref, ys_hbm, o_ref, buf0, buf1, sem, *, alpha):
    base = pl.program_id(0) * MOE_TM

    def issue(r, c):
        _row_copy(ys_hbm, pos_ref[2 * (base + r)], buf0, r, sem).start()
        _row_copy(ys_hbm, pos_ref[2 * (base + r) + 1], buf1, r, sem).start()
        return c

    def wait(r, c):
        _row_copy(ys_hbm, pos_ref[2 * (base + r)], buf0, r, sem).wait()
        _row_copy(ys_hbm, pos_ref[2 * (base + r) + 1], buf1, r, sem).wait()
        return c

    lax.fori_loop(0, MOE_TM, issue, 0)
    lax.fori_loop(0, MOE_TM, wait, 0)
    y = alpha * x1_ref[...] + (buf0[...] + buf1[...])
    o_ref[...] = _layer_norm(y, g_ref[...], b_ref[...])


def _combine(pos, x1, ln_g, ln_b, ys, *, alpha):
    t, d = x1.shape
    return pl.pallas_call(
        functools.partial(_combine_kernel, alpha=alpha),
        grid_spec=pltpu.PrefetchScalarGridSpec(
            num_scalar_prefetch=1,
            grid=(t // MOE_TM,),
            in_specs=[pl.BlockSpec((MOE_TM, d), lambda i, pos: (i, 0)),
                      pl.BlockSpec((1, d), lambda i, pos: (0, 0)),
                      pl.BlockSpec((1, d), lambda i, pos: (0, 0)),
                      pl.BlockSpec(memory_space=pl.ANY)],
            out_specs=pl.BlockSpec((MOE_TM, d), lambda i, pos: (i, 0)),
            scratch_shapes=[pltpu.VMEM((MOE_TM, d), F32), pltpu.VMEM((MOE_TM, d), F32),
                            pltpu.SemaphoreType.DMA(())],
        ),
        out_shape=jax.ShapeDtypeStruct((t, d), F32),
        compiler_params=_cparams("arbitrary"),
        name="moe_combine",
    )(pos, x1, ln_g.reshape(1, d), ln_b.reshape(1, d), ys)


def _moe_plan(route, t):
    wts = route[:, 0:2].reshape(-1)
    flat_e = route[:, 2:4].astype(I32).reshape(-1)
    n = flat_e.shape[0]
    n_rows = n + N_EXPERTS * MOE_TM
    n_tiles = n_rows // MOE_TM
    onehot = (flat_e[:, None] == jnp.arange(N_EXPERTS, dtype=I32)[None, :]).astype(I32)
    running = jnp.cumsum(onehot, axis=0)
    rank = jnp.sum(onehot * running, axis=1) - 1
    counts = running[-1]
    tiles_per = (counts + MOE_TM - 1) // MOE_TM
    tile_end = jnp.cumsum(tiles_per)
    grp_row0 = (tile_end - tiles_per) * MOE_TM
    pos = jnp.sum(onehot * grp_row0[None, :], axis=1) + rank
    row_token = jnp.zeros((n_rows,), I32).at[pos].set(jnp.arange(n, dtype=I32) // 2)
    row_w = jnp.zeros((n_rows,), F32).at[pos].set(wts)
    n_used = tile_end[-1]
    tile_ids = jnp.minimum(jnp.arange(n_tiles, dtype=I32), n_used - 1)
    tile_expert = jnp.sum((tile_end[None, :] <= tile_ids[:, None]).astype(I32), axis=1)
    tile_expert = jnp.minimum(tile_expert, N_EXPERTS - 1)
    return row_token, row_w.reshape(n_rows, 1), pos, tile_expert, n_used.reshape(1).astype(I32)


def kernel(x, mem, w_mem_kv, w_in, g_cq, g_ckv, g_kidx, b_kidx, w_uq, w_uqi, w_ukv,
           w_up_a, w_up_b, w_up_c, w_gate, b_gate, w_o, ln1_g, ln1_b,
           w_grp, b_grp, w_rt, b_rt, w1, w3, w2, ln2_g, ln2_b):
    bsz, t, d = x.shape
    assert bsz == 1 and t % TQ == 0
    depth = w_in.shape[0]
    alpha = (2 * depth) ** 0.25
    scale = HEAD_DIM ** -0.5 * LOG2E
    top_k = min(TOPK_A_MAX, t // 4)
    n_mem = mem.shape[1]
    tm = min(t, 512)

    slopes = LOG2E * 2.0 ** (-8.0 * jnp.arange(1, N_ALIBI + 1, dtype=F32) / N_ALIBI)
    slopes_a, slopes_b = slopes[0::2], slopes[1::2]

    mkv = _mm(mem[0], w_mem_kv.astype(BF16), out_dtype=BF16, tm=n_mem, tn=W_C)
    mk = mkv[:, :W_C]
    mvt = mkv[:, W_C:].T

    xl = x[0]
    for l in range(depth):
        wl = w_in[l]
        o = 0
        parts = []
        for width in (Q_RANK_A, KV_RANK_A, IDX_DIM, N_IDX_HEADS, W_B, W_B, W_B, W_C):
            parts.append(wl[:, o:o + width])
            o += width
        w_cq, w_ckv, w_ki, w_wi, w_qb, w_kb, w_vb, w_qc = parts
        n_small = Q_RANK_A + KV_RANK_A + IDX_DIM + N_IDX_HEADS
        pad = (-n_small) % LANES
        w_p1 = jnp.concatenate([w_cq, w_ckv, w_ki, w_wi * N_IDX_HEADS ** -0.5,
                                jnp.zeros((d, pad), F32)], axis=1).astype(BF16)
        w_p2 = jnp.concatenate([w_qb * scale, w_kb, w_vb, w_qc * scale], axis=1).astype(BF16)
        p1 = _mm(xl, w_p1, out_dtype=F32, tm=tm, tn=w_p1.shape[1])
        p2 = _mm(xl, w_p2, out_dtype=BF16, tm=tm, tn=w_p2.shape[1] // 2)
        c_q = p1[:, :Q_RANK_A]
        c_kv = p1[:, Q_RANK_A:Q_RANK_A + KV_RANK_A]
        k_i = p1[:, Q_RANK_A + KV_RANK_A:Q_RANK_A + KV_RANK_A + IDX_DIM]
        w_i = p1[:, Q_RANK_A + KV_RANK_A + IDX_DIM:n_small]
        q_b, k_b = p2[:, :W_B], p2[:, W_B:2 * W_B]
        v_b, q_c = p2[:, 2 * W_B:3 * W_B], p2[:, 3 * W_B:]

        w_q = jnp.concatenate([w_uq[l] * scale, w_uqi[l] * IDX_DIM ** -0.5], axis=1).astype(BF16)
        qq = _mm(c_q, w_q, out_dtype=BF16, tm=tm, tn=w_q.shape[1], norm_g=g_cq[l])
        kv = _mm(c_kv, w_ukv[l].astype(BF16), out_dtype=BF16, tm=tm, tn=2 * W_A, norm_g=g_ckv[l])
        q_a, q_i = qq[:, :W_A], qq[:, W_A:]
        k_a, v_a = kv[:, :W_A], kv[:, W_A:]
        kidx = _kidx_norm(k_i, g_kidx[l], b_kidx[l])
        qi_r = (q_i.reshape(t // TQ, TQ, N_IDX_HEADS, IDX_DIM).transpose(0, 2, 1, 3)
                .reshape(t // TQ, N_IDX_HEADS * TQ, IDX_DIM))
        o_a = _dsa(slopes_a, qi_r, w_i.T, kidx, q_a, k_a, v_a.T, top_k)
        o_b = _moba(slopes_b, q_b, k_b, v_b.T)
        o_c = _mem_attn(q_c, mk, mvt)

        merged = _merge(xl, w_gate[l].astype(BF16), b_gate[l], o_a, o_b, o_c,
                        w_up_a[l].astype(BF16), w_up_b[l].astype(BF16), w_up_c[l].astype(BF16),
                        tm=min(t, 1024), tn=512)

        w_r = jnp.concatenate([w_grp[l], w_rt[l],
                               jnp.zeros((d, ROUTE_LANES - N_GROUPS - N_EXPERTS), F32)], axis=1)
        b_r = jnp.concatenate([b_grp[l], b_rt[l],
                               jnp.zeros((ROUTE_LANES - N_GROUPS - N_EXPERTS,), F32)]).reshape(1, ROUTE_LANES)
        x1, route = _oproj(xl, merged, w_o[l].astype(BF16), ln1_g[l], ln1_b[l], w_r, b_r,
                           alpha=alpha, tm=min(t, 256))

        row_token, row_w, pos, tile_expert, n_used = _moe_plan(route, t)
        ys = _experts(row_token, tile_expert, n_used, x1, row_w, w1, w3, w2, l)
        xl = _combine(pos, x1, ln2_g[l], ln2_b[l], ys, alpha=alpha)
    return xl[None]
```

```python
import functools

import jax
import jax.numpy as jnp
from jax import lax
from jax.experimental import pallas as pl
from jax.experimental.pallas import tpu as pltpu

HEAD_DIM = 128
N_HEADS_A = 6
N_HEADS_B = 6
N_HEADS_C = 4
W_A = N_HEADS_A * HEAD_DIM
W_B = N_HEADS_B * HEAD_DIM
W_C = N_HEADS_C * HEAD_DIM
Q_RANK_A = 512
KV_RANK_A = 256
N_IDX_HEADS = 16
IDX_DIM = 64
TOPK_A_MAX = 256
MOBA_BLOCK = 256
MOBA_TOPK = 3
N_ALIBI = N_HEADS_A + N_HEADS_B
N_GROUPS = 4
EXPERTS_PER_GROUP = 8
N_EXPERTS = N_GROUPS * EXPERTS_PER_GROUP
D_EXPERT = 512
NORM_EPS = 1e-5
NEG = -1e30
LOG2E = 1.4426950408889634
INT_MIN = -(2 ** 31)

LANES = 128
SUBLANES = 8
TQ = 256
CK = 256
ROUTE_LANES = 128
MOE_TM = 256
ROW_DMA_UNROLL = 8
VMEM_LIMIT = 56 * 1024 * 1024

F32 = jnp.float32
BF16 = jnp.bfloat16
I32 = jnp.int32

_NT = (((1,), (1,)), ((), ()))


def _cparams(*sem):
    return pltpu.CompilerParams(dimension_semantics=sem, vmem_limit_bytes=VMEM_LIMIT)


def _resident(shape):
    nd = len(shape)
    return pl.BlockSpec(shape, lambda *_: (0,) * nd, pipeline_mode=pl.Buffered(1))


def _mm_kernel(*refs, has_norm, has_bias):
    x_ref, w_ref = refs[0], refs[1]
    k = 2
    x = x_ref[...]
    if has_norm:
        g_ref = refs[k]
        k += 1
        xf = x.astype(F32)
        x = xf * lax.rsqrt(jnp.mean(xf * xf, axis=-1, keepdims=True) + NORM_EPS) * g_ref[...]
    acc = jnp.dot(x.astype(BF16), w_ref[...], preferred_element_type=F32)
    if has_bias:
        acc = acc + refs[k][...]
        k += 1
    o_ref = refs[k]
    o_ref[...] = acc.astype(o_ref.dtype)


def _mm(x, w, *, out_dtype, tm, tn, norm_g=None, bias=None):
    m, kdim = x.shape
    n = w.shape[1]
    assert m % tm == 0 and n % tn == 0
    in_specs = [pl.BlockSpec((tm, kdim), lambda i, j: (i, 0)),
                pl.BlockSpec((kdim, tn), lambda i, j: (0, j))]
    args = [x, w]
    if norm_g is not None:
        in_specs.append(pl.BlockSpec((1, kdim), lambda i, j: (0, 0)))
        args.append(norm_g.reshape(1, kdim).astype(F32))
    if bias is not None:
        in_specs.append(pl.BlockSpec((1, tn), lambda i, j: (0, j)))
        args.append(bias.reshape(1, n).astype(F32))
    return pl.pallas_call(
        functools.partial(_mm_kernel, has_norm=norm_g is not None, has_bias=bias is not None),
        grid=(m // tm, n // tn),
        in_specs=in_specs,
        out_specs=pl.BlockSpec((tm, tn), lambda i, j: (i, j)),
        out_shape=jax.ShapeDtypeStruct((m, n), out_dtype),
        compiler_params=_cparams("parallel", "parallel"),
        name="mm",
    )(*args)


def _kidx_kernel(x_ref, g_ref, b_ref, o_ref):
    x = x_ref[...]
    mu = jnp.mean(x, axis=-1, keepdims=True)
    xc = x - mu
    var = jnp.mean(xc * xc, axis=-1, keepdims=True)
    o_ref[...] = (xc * lax.rsqrt(var + NORM_EPS) * g_ref[...] + b_ref[...]).astype(o_ref.dtype)


def _kidx_norm(k_i, g, b):
    t, d = k_i.shape
    tm = min(t, 1024)
    return pl.pallas_call(
        _kidx_kernel,
        grid=(t // tm,),
        in_specs=[pl.BlockSpec((tm, d), lambda i: (i, 0)),
                  pl.BlockSpec((1, d), lambda i: (0, 0)),
                  pl.BlockSpec((1, d), lambda i: (0, 0))],
        out_specs=pl.BlockSpec((tm, d), lambda i: (i, 0)),
        out_shape=jax.ShapeDtypeStruct((t, d), BF16),
        compiler_params=_cparams("parallel"),
        name="kidx_norm",
    )(k_i, g.reshape(1, d), b.reshape(1, d))


def _head(h):
    return slice(h * HEAD_DIM, (h + 1) * HEAD_DIM)


def _attn_state(n_heads):
    return ([pltpu.VMEM((CK, TQ), F32) for _ in range(2 * n_heads)]
            + [pltpu.VMEM((HEAD_DIM, TQ), F32) for _ in range(n_heads)]
            + [pltpu.VMEM((1, TQ), F32) for _ in range(2 * n_heads)])


def _attn_reset(state, n_heads):
    state = state[2 * n_heads:]
    for h in range(n_heads):
        state[h][...] = jnp.zeros((HEAD_DIM, TQ), F32)
        state[n_heads + h][...] = jnp.full((1, TQ), NEG, F32)
        state[2 * n_heads + h][...] = jnp.zeros((1, TQ), F32)


def _attn_run(state, n_heads, i, q_ref, k_ref, consume):
    def scores(j, slot):
        start = pl.multiple_of(j * CK, CK)
        for h in range(n_heads):
            state[slot * n_heads + h][...] = lax.dot_general(
                k_ref[pl.ds(start, CK), _head(h)], q_ref[:, _head(h)], _NT,
                preferred_element_type=F32)

    scores(0, 0)

    def pair(p, c):
        j = 2 * p
        scores(j + 1, 1)
        consume(j, 0, False)
        scores(j + 2, 0)
        consume(j + 1, 1, False)
        return c

    lax.fori_loop(0, i // 2, pair, 0)

    @pl.when(i % 2 == 0)
    def _():
        consume(i, 0, True)

    @pl.when(i % 2 == 1)
    def _():
        scores(i, 1)
        consume(i - 1, 0, False)
        consume(i, 1, True)


def _attn_update(state, n_heads, h, slot, vt_blk, t_bias, c):
    s_ref, state = state[slot * n_heads + h], state[2 * n_heads:]
    acc_ref, m_ref, l_ref = state[h], state[n_heads + h], state[2 * n_heads + h]
    t = s_ref[...] + t_bias
    m_old = m_ref[...]
    m_new = jnp.maximum(m_old, jnp.max(t, axis=0, keepdims=True) + c)
    alpha = jnp.exp2(m_old - m_new)
    p = jnp.exp2(t - (m_new - c))
    l_ref[...] = alpha * l_ref[...] + jnp.sum(p, axis=0, keepdims=True)
    acc_ref[...] = alpha * acc_ref[...] + jnp.dot(vt_blk, p.astype(BF16), preferred_element_type=F32)
    m_ref[...] = m_new


def _attn_finish(state, n_heads, h):
    state = state[2 * n_heads:]
    return (state[h][...] * (1.0 / state[2 * n_heads + h][...])).T


def _key_offsets():
    return lax.broadcasted_iota(I32, (CK, TQ), 0), lax.broadcasted_iota(I32, (CK, TQ), 1)


def _sortable(x):
    b = pltpu.bitcast(x, I32)
    return b ^ ((b >> 31) & 0x7FFFFFFF)


def _dsa_kernel(slopes_ref, qi_ref, wi_ref, kidx_ref, qa_ref, ka_ref, vat_ref, o_ref,
                keys_ref, sb_ref, *state, top_k):
    i = pl.program_id(0)
    rows, cols = _key_offsets()
    causal = rows <= cols
    nh = N_HEADS_A

    @pl.when(i == 0)
    def _():
        rows_f = rows.astype(F32)
        for h in range(nh):
            sb_ref[h] = slopes_ref[h] * rows_f

    def score_chunk(j, diagonal):
        start = pl.multiple_of(j * CK, CK)
        kc = kidx_ref[pl.ds(start, CK), :]
        acc = jnp.zeros((CK, TQ), F32)
        for h in range(N_IDX_HEADS):
            z = lax.dot_general(kc, qi_ref[0, h * TQ:(h + 1) * TQ, :], _NT,
                                preferred_element_type=F32)
            acc = acc + wi_ref[h:h + 1, :] * jnp.maximum(z, 0.0)
        if diagonal:
            acc = jnp.where(causal, acc, NEG)
        keys_ref[pl.ds(start, CK), :] = _sortable(acc)

    def score_body(j, c):
        score_chunk(j, False)
        return c

    lax.fori_loop(0, i, score_body, 0)
    score_chunk(i, True)

    def count_ge(cand):
        def body(j, cnt):
            blk = keys_ref[pl.ds(pl.multiple_of(j * CK, CK), CK), :]
            ge = jnp.where(blk >= cand, 1, 0).astype(I32)
            return cnt + jnp.sum(ge.reshape(CK // SUBLANES, SUBLANES, TQ), axis=0)
        cnt = lax.fori_loop(0, i + 1, body, jnp.zeros((SUBLANES, TQ), I32))
        return jnp.sum(cnt, axis=0, keepdims=True)

    def bit_body(b, thr):
        cand = thr + jnp.left_shift(jnp.int32(1), 31 - b)
        return jnp.where(count_ge(cand) >= top_k, cand, thr)

    thr = lax.fori_loop(0, 32, bit_body, jnp.full((1, TQ), INT_MIN, I32))

    _attn_reset(state, nh)

    def consume(j, slot, diagonal):
        start = pl.multiple_of(j * CK, CK)
        sel = keys_ref[pl.ds(start, CK), :] >= thr
        if diagonal:
            sel = jnp.logical_and(sel, causal)
        mask_bias = jnp.where(sel, 0.0, NEG)
        off = ((j - i) * CK).astype(F32)
        for h in range(nh):
            _attn_update(state, nh, h, slot, vat_ref[_head(h), pl.ds(start, CK)],
                         sb_ref[h] + mask_bias, slopes_ref[h] * off)

    _attn_run(state, nh, i, qa_ref, ka_ref, consume)
    for h in range(nh):
        o_ref[:, _head(h)] = _attn_finish(state, nh, h).astype(o_ref.dtype)


def _dsa(slopes, qi_r, wi_t, kidx, qa, ka, vat, top_k):
    t, w = qa.shape
    nh = w // HEAD_DIM
    return pl.pallas_call(
        functools.partial(_dsa_kernel, top_k=top_k),
        grid_spec=pltpu.PrefetchScalarGridSpec(
            num_scalar_prefetch=0,
            grid=(t // TQ,),
            in_specs=[
                pl.BlockSpec(memory_space=pltpu.SMEM),
                pl.BlockSpec((1, N_IDX_HEADS * TQ, IDX_DIM), lambda i: (i, 0, 0)),
                pl.BlockSpec((N_IDX_HEADS, TQ), lambda i: (0, i)),
                _resident(kidx.shape),
                pl.BlockSpec((TQ, w), lambda i: (i, 0)),
                _resident(ka.shape),
                _resident(vat.shape),
            ],
            out_specs=pl.BlockSpec((TQ, w), lambda i: (i, 0)),
            scratch_shapes=[pltpu.VMEM((t, TQ), I32), pltpu.VMEM((nh, CK, TQ), F32)] + _attn_state(nh),
        ),
        out_shape=jax.ShapeDtypeStruct((t, w), BF16),
        compiler_params=_cparams("arbitrary"),
        name="dsa",
    )(slopes, qi_r, wi_t, kidx, qa, ka, vat)


def _moba_kernel(slopes_ref, qb_ref, kb_ref, vbt_ref, o_ref, kmean_ref, sel_ref, sb_ref, *state, n_kb):
    i = pl.program_id(0)
    rows, cols = _key_offsets()
    nh = N_HEADS_B

    @pl.when(i == 0)
    def _():
        rows_f = rows.astype(F32)
        for h in range(nh):
            sb_ref[h] = slopes_ref[h] * rows_f
            for n in range(n_kb):
                blk = kb_ref[n * CK:(n + 1) * CK, _head(h)].astype(F32)
                kmean_ref[h, n:n + 1, :] = jnp.mean(blk, axis=0, keepdims=True)

    blk_id = lax.broadcasted_iota(I32, (n_kb, TQ), 0)
    for h in range(nh):
        q_h = qb_ref[:, _head(h)]
        km = kmean_ref[h]
        km_hi = km.astype(BF16)
        km_lo = (km - km_hi.astype(F32)).astype(BF16)
        gate = (lax.dot_general(km_hi, q_h, _NT, preferred_element_type=F32)
                + lax.dot_general(km_lo, q_h, _NT, preferred_element_type=F32))
        gate = jnp.where(blk_id < i, gate, NEG)
        sel = jnp.full((n_kb, TQ), NEG, F32)
        for _ in range(MOBA_TOPK):
            best = jnp.max(gate, axis=0, keepdims=True)
            first = jnp.min(jnp.where(gate == best, blk_id, n_kb), axis=0, keepdims=True)
            pick = blk_id == first
            sel = jnp.where(pick, 0.0, sel)
            gate = jnp.where(pick, -jnp.inf, gate)
        sel_ref[h] = jnp.where(blk_id < i, sel, NEG)

    _attn_reset(state, nh)

    def consume(j, slot, diagonal):
        start = pl.multiple_of(j * CK, CK)
        off = ((j - i) * CK).astype(F32)
        for h in range(nh):
            if diagonal:
                t_bias = jnp.where(rows <= cols, sb_ref[h], NEG)
            else:
                t_bias = sb_ref[h] + sel_ref[h, pl.ds(j, 1), :]
            _attn_update(state, nh, h, slot, vbt_ref[_head(h), pl.ds(start, CK)],
                         t_bias, slopes_ref[h] * off)

    _attn_run(state, nh, i, qb_ref, kb_ref, consume)
    for h in range(nh):
        o_ref[:, _head(h)] = _attn_finish(state, nh, h).astype(o_ref.dtype)


def _moba(slopes, qb, kb, vbt):
    t, w = qb.shape
    nh = w // HEAD_DIM
    assert t % MOBA_BLOCK == 0 and TQ == MOBA_BLOCK and CK == MOBA_BLOCK
    n_kb = t // MOBA_BLOCK
    return pl.pallas_call(
        functools.partial(_moba_kernel, n_kb=n_kb),
        grid_spec=pltpu.PrefetchScalarGridSpec(
            num_scalar_prefetch=0,
            grid=(t // TQ,),
            in_specs=[
                pl.BlockSpec(memory_space=pltpu.SMEM),
                pl.BlockSpec((TQ, w), lambda i: (i, 0)),
                _resident(kb.shape),
                _resident(vbt.shape),
            ],
            out_specs=pl.BlockSpec((TQ, w), lambda i: (i, 0)),
            scratch_shapes=[pltpu.VMEM((nh, n_kb, HEAD_DIM), F32), pltpu.VMEM((nh, n_kb, TQ), F32),
                            pltpu.VMEM((nh, CK, TQ), F32)] + _attn_state(nh),
        ),
        out_shape=jax.ShapeDtypeStruct((t, w), BF16),
        compiler_params=_cparams("arbitrary"),
        name="moba",
    )(slopes, qb, kb, vbt)


def _mem_kernel(qc_ref, mk_ref, mvt_ref, o_ref):
    for h in range(N_HEADS_C):
        s = lax.dot_general(mk_ref[:, _head(h)], qc_ref[:, _head(h)], _NT,
                            preferred_element_type=F32)
        m = jnp.max(s, axis=0, keepdims=True)
        p = jnp.exp2(s - m)
        l = jnp.sum(p, axis=0, keepdims=True)
        acc = jnp.dot(mvt_ref[_head(h), :], p.astype(BF16), preferred_element_type=F32)
        o_ref[:, _head(h)] = (acc * (1.0 / l)).T.astype(o_ref.dtype)


def _mem_attn(qc, mk, mvt):
    t, w = qc.shape
    return pl.pallas_call(
        _mem_kernel,
        grid=(t // TQ,),
        in_specs=[pl.BlockSpec((TQ, w), lambda i: (i, 0)),
                  _resident(mk.shape), _resident(mvt.shape)],
        out_specs=pl.BlockSpec((TQ, w), lambda i: (i, 0)),
        out_shape=jax.ShapeDtypeStruct((t, w), BF16),
        compiler_params=_cparams("parallel"),
        name="mem_attn",
    )(qc, mk, mvt)


def _sigmoid(x):
    return 1.0 / (1.0 + jnp.exp(-x))


def _merge_kernel(x_ref, wga_ref, wgb_ref, wgc_ref, bga_ref, bgb_ref, bgc_ref,
                  oa_ref, ob_ref, oc_ref, wua_ref, wub_ref, wuc_ref, o_ref):
    xb = x_ref[...].astype(BF16)

    def branch(wg_ref, bg_ref, oo_ref, wu_ref):
        g = _sigmoid(jnp.dot(xb, wg_ref[...], preferred_element_type=F32) + bg_ref[...])
        return g * jnp.dot(oo_ref[...], wu_ref[...], preferred_element_type=F32)

    acc = branch(wga_ref, bga_ref, oa_ref, wua_ref)
    acc = acc + branch(wgb_ref, bgb_ref, ob_ref, wub_ref)
    acc = acc + branch(wgc_ref, bgc_ref, oc_ref, wuc_ref)
    o_ref[...] = acc.astype(o_ref.dtype)


def _merge(x, w_gate, b_gate, o_a, o_b, o_c, w_up_a, w_up_b, w_up_c, *, tm, tn):
    t, d = x.shape
    nb = d // tn
    b_gate = b_gate.reshape(1, 3 * d)

    def wg(k):
        return pl.BlockSpec((d, tn), lambda i, j, k=k: (0, j + k * nb))

    def bg(k):
        return pl.BlockSpec((1, tn), lambda i, j, k=k: (0, j + k * nb))

    def act(w):
        return pl.BlockSpec((tm, w), lambda i, j: (i, 0))

    def wu(w):
        return pl.BlockSpec((w, tn), lambda i, j: (0, j))

    return pl.pallas_call(
        _merge_kernel,
        grid=(t // tm, nb),
        in_specs=[pl.BlockSpec((tm, d), lambda i, j: (i, 0)),
                  wg(0), wg(1), wg(2), bg(0), bg(1), bg(2),
                  act(W_A), act(W_B), act(W_C), wu(W_A), wu(W_B), wu(W_C)],
        out_specs=pl.BlockSpec((tm, tn), lambda i, j: (i, j)),
        out_shape=jax.ShapeDtypeStruct((t, d), BF16),
        compiler_params=_cparams("parallel", "parallel"),
        name="merge",
    )(x, w_gate, w_gate, w_gate, b_gate, b_gate, b_gate, o_a, o_b, o_c, w_up_a, w_up_b, w_up_c)


def _layer_norm(y, g, b):
    mu = jnp.mean(y, axis=-1, keepdims=True)
    yc = y - mu
    var = jnp.mean(yc * yc, axis=-1, keepdims=True)
    return yc * lax.rsqrt(var + NORM_EPS) * g + b


def _route(logits):
    lane = lax.broadcasted_iota(I32, logits.shape, 1)
    is_g = lane < N_GROUPS
    gl = jnp.where(is_g, logits, -jnp.inf)
    gmax = jnp.max(gl, axis=-1, keepdims=True)
    g_sel = jnp.min(jnp.where(gl == gmax, lane, ROUTE_LANES), axis=-1, keepdims=True)
    p_g = 1.0 / jnp.sum(jnp.where(is_g, jnp.exp(gl - gmax), 0.0), axis=-1, keepdims=True)
    e_id = lane - N_GROUPS
    in_grp = jnp.logical_and(e_id >= g_sel * EXPERTS_PER_GROUP, e_id < (g_sel + 1) * EXPERTS_PER_GROUP)
    el = jnp.where(in_grp, logits, -jnp.inf)
    emax = jnp.max(el, axis=-1, keepdims=True)
    ex = jnp.where(in_grp, jnp.exp(el - emax), 0.0)
    pe = ex / jnp.sum(ex, axis=-1, keepdims=True)
    pe = jnp.where(in_grp, pe, -1.0)
    p1 = jnp.max(pe, axis=-1, keepdims=True)
    i1 = jnp.min(jnp.where(pe == p1, e_id, ROUTE_LANES), axis=-1, keepdims=True)
    pe2 = jnp.where(e_id == i1, -1.0, pe)
    p2 = jnp.max(pe2, axis=-1, keepdims=True)
    i2 = jnp.min(jnp.where(pe2 == p2, e_id, ROUTE_LANES), axis=-1, keepdims=True)
    denom = p1 + p2
    w1 = p_g * (p1 / denom)
    w2 = p_g * (p2 / denom)
    return jnp.where(lane == 0, w1,
                     jnp.where(lane == 1, w2,
                               jnp.where(lane == 2, i1.astype(F32),
                                         jnp.where(lane == 3, i2.astype(F32), 0.0))))


def _oproj_kernel(x_ref, m_ref, wo_ref, g_ref, b_ref, wr_ref, br_ref, x1_ref, r_ref, *, alpha):
    y = alpha * x_ref[...] + jnp.dot(m_ref[...], wo_ref[...], preferred_element_type=F32)
    x1 = _layer_norm(y, g_ref[...], b_ref[...])
    x1_ref[...] = x1
    x_hi = x1.astype(BF16)
    x_lo = (x1 - x_hi.astype(F32)).astype(BF16)
    parts = (jnp.dot(x_hi, wr_ref[...], preferred_element_type=F32)
             + jnp.dot(x_lo, wr_ref[...], preferred_element_type=F32))
    logits = parts + pltpu.roll(parts, ROUTE_LANES // 2, axis=1) + br_ref[...]
    r_ref[...] = _route(logits)


def _oproj(x, merged, w_o, ln_g, ln_b, w_r, b_r, *, alpha, tm):
    t, d = x.shape
    return pl.pallas_call(
        functools.partial(_oproj_kernel, alpha=alpha),
        grid=(t // tm,),
        in_specs=[pl.BlockSpec((tm, d), lambda i: (i, 0)),
                  pl.BlockSpec((tm, d), lambda i: (i, 0)),
                  _resident(w_o.shape),
                  pl.BlockSpec((1, d), lambda i: (0, 0)),
                  pl.BlockSpec((1, d), lambda i: (0, 0)),
                  _resident(w_r.shape),
                  pl.BlockSpec((1, ROUTE_LANES), lambda i: (0, 0))],
        out_specs=[pl.BlockSpec((tm, d), lambda i: (i, 0)),
                   pl.BlockSpec((tm, ROUTE_LANES), lambda i: (i, 0))],
        out_shape=[jax.ShapeDtypeStruct((t, d), F32),
                   jax.ShapeDtypeStruct((t, ROUTE_LANES), F32)],
        compiler_params=_cparams("parallel"),
        name="oproj",
    )(x, merged, w_o, ln_g.reshape(1, d), ln_b.reshape(1, d), w_r, b_r)


def _row_copy(src_hbm, row, dst, k, sem):
    return pltpu.make_async_copy(src_hbm.at[pl.ds(row, 1), :], dst.at[pl.ds(k, 1), :], sem)


def _expert_kernel(tok_ref, te_ref, nu_ref, x_hbm, rw_ref, w1_ref, w3_ref, w2_ref, o_ref, buf, sem):
    i = pl.program_id(0)
    n_used = nu_ref[0]
    slot = lax.rem(i, 2)

    def gather(tile, s):
        base = tile * MOE_TM

        def body(r, c):
            _row_copy(x_hbm, tok_ref[base + r], buf.at[s], r, sem.at[s]).start()
            return c

        lax.fori_loop(0, MOE_TM, body, 0, unroll=ROW_DMA_UNROLL)

    @pl.when(i == 0)
    def _():
        gather(0, 0)

    @pl.when(i < n_used)
    def _():
        pltpu.make_async_copy(x_hbm.at[pl.ds(0, MOE_TM), :], buf.at[slot], sem.at[slot]).wait()

        @pl.when(i + 1 < n_used)
        def _():
            gather(i + 1, 1 - slot)

        xs = buf[slot].astype(BF16)
        h1 = jnp.dot(xs, w1_ref[0, 0].astype(BF16), preferred_element_type=F32)
        h3 = jnp.dot(xs, w3_ref[0, 0].astype(BF16), preferred_element_type=F32)
        hid = (h1 * _sigmoid(h1)) * h3
        y = jnp.dot(hid.astype(BF16), w2_ref[0, 0].astype(BF16), preferred_element_type=F32)
        o_ref[...] = rw_ref[...] * y

    @pl.when(i >= n_used)
    def _():
        o_ref[...] = jnp.zeros_like(o_ref)


def _experts(row_token, tile_expert, n_used, x1, row_w, w1, w3, w2, layer):
    r = row_token.shape[0]
    d = x1.shape[1]
    de = w1.shape[-1]

    def wspec(a, b):
        return pl.BlockSpec((1, 1, a, b), lambda i, tok, te, nu: (layer, te[i], 0, 0))

    return pl.pallas_call(
        _expert_kernel,
        grid_spec=pltpu.PrefetchScalarGridSpec(
            num_scalar_prefetch=3,
            grid=(r // MOE_TM,),
            in_specs=[pl.BlockSpec(memory_space=pl.ANY),
                      pl.BlockSpec((MOE_TM, 1), lambda i, tok, te, nu: (i, 0)),
                      wspec(d, de), wspec(d, de), wspec(de, d)],
            out_specs=pl.BlockSpec((MOE_TM, d), lambda i, tok, te, nu: (i, 0)),
            scratch_shapes=[pltpu.VMEM((2, MOE_TM, d), F32), pltpu.SemaphoreType.DMA((2,))],
        ),
        out_shape=jax.ShapeDtypeStruct((r, d), F32),
        compiler_params=_cparams("arbitrary"),
        name="moe_experts",
    )(row_token, tile_expert, n_used, x1, row_w, w1, w3, w2)


def _combine_kernel(pos_ref, x1_ref, g_ref, b_ref, ys_hbm, o_ref, buf, sem, *, alpha):
    i = pl.program_id(0)
    slot = lax.rem(i, 2)

    def gather(tile, s):
        base = tile * MOE_TM

        def body(r, c):
            for k in range(2):
                _row_copy(ys_hbm, pos_ref[2 * (base + r) + k], buf.at[s, k], r, sem.at[s]).start()
            return c

        lax.fori_loop(0, MOE_TM, body, 0, unroll=ROW_DMA_UNROLL // 2)

    @pl.when(i == 0)
    def _():
        gather(0, 0)

    for k in range(2):
        pltpu.make_async_copy(ys_hbm.at[pl.ds(0, MOE_TM), :], buf.at[slot, k], sem.at[slot]).wait()

    @pl.when(i + 1 < pl.num_programs(0))
    def _():
        gather(i + 1, 1 - slot)

    y = alpha * x1_ref[...] + (buf[slot, 0] + buf[slot, 1])
    o_ref[...] = _layer_norm(y, g_ref[...], b_ref[...])


def _combine(pos, x1, ln_g, ln_b, ys, *, alpha):
    t, d = x1.shape
    return pl.pallas_call(
        functools.partial(_combine_kernel, alpha=alpha),
        grid_spec=pltpu.PrefetchScalarGridSpec(
            num_scalar_prefetch=1,
            grid=(t // MOE_TM,),
            in_specs=[pl.BlockSpec((MOE_TM, d), lambda i, pos: (i, 0)),
                      pl.BlockSpec((1, d), lambda i, pos: (0, 0)),
                      pl.BlockSpec((1, d), lambda i, pos: (0, 0)),
                      pl.BlockSpec(memory_space=pl.ANY)],
            out_specs=pl.BlockSpec((MOE_TM, d), lambda i, pos: (i, 0)),
            scratch_shapes=[pltpu.VMEM((2, 2, MOE_TM, d), F32), pltpu.SemaphoreType.DMA((2,))],
        ),
        out_shape=jax.ShapeDtypeStruct((t, d), F32),
        compiler_params=_cparams("arbitrary"),
        name="moe_combine",
    )(pos, x1, ln_g.reshape(1, d), ln_b.reshape(1, d), ys)


def _moe_plan(route, t):
    wts = route[:, 0:2].reshape(-1)
    flat_e = route[:, 2:4].astype(I32).reshape(-1)
    n = flat_e.shape[0]
    n_rows = n + N_EXPERTS * MOE_TM
    n_tiles = n_rows // MOE_TM
    onehot = (flat_e[:, None] == jnp.arange(N_EXPERTS, dtype=I32)[None, :]).astype(I32)
    running = jnp.cumsum(onehot, axis=0)
    rank = jnp.sum(onehot * running, axis=1) - 1
    counts = running[-1]
    tiles_per = (counts + MOE_TM - 1) // MOE_TM
    tile_end = jnp.cumsum(tiles_per)
    grp_row0 = (tile_end - tiles_per) * MOE_TM
    pos = jnp.sum(onehot * grp_row0[None, :], axis=1) + rank
    row_token = jnp.zeros((n_rows,), I32).at[pos].set(jnp.arange(n, dtype=I32) // 2)
    row_w = jnp.zeros((n_rows,), F32).at[pos].set(wts)
    n_used = tile_end[-1]
    tile_ids = jnp.minimum(jnp.arange(n_tiles, dtype=I32), n_used - 1)
    tile_expert = jnp.sum((tile_end[None, :] <= tile_ids[:, None]).astype(I32), axis=1)
    tile_expert = jnp.minimum(tile_expert, N_EXPERTS - 1)
    return row_token, row_w.reshape(n_rows, 1), pos, tile_expert, n_used.reshape(1).astype(I32)


def kernel(x, mem, w_mem_kv, w_in, g_cq, g_ckv, g_kidx, b_kidx, w_uq, w_uqi, w_ukv,
           w_up_a, w_up_b, w_up_c, w_gate, b_gate, w_o, ln1_g, ln1_b,
           w_grp, b_grp, w_rt, b_rt, w1, w3, w2, ln2_g, ln2_b):
    bsz, t, d = x.shape
    assert bsz == 1 and t % TQ == 0
    depth = w_in.shape[0]
    alpha = (2 * depth) ** 0.25
    scale = HEAD_DIM ** -0.5 * LOG2E
    top_k = min(TOPK_A_MAX, t // 4)
    n_mem = mem.shape[1]
    tm = min(t, 512)

    slopes = LOG2E * 2.0 ** (-8.0 * jnp.arange(1, N_ALIBI + 1, dtype=F32) / N_ALIBI)
    slopes_a, slopes_b = slopes[0::2], slopes[1::2]

    mkv = _mm(mem[0], w_mem_kv.astype(BF16), out_dtype=BF16, tm=n_mem, tn=W_C)
    mk = mkv[:, :W_C]
    mvt = mkv[:, W_C:].T

    xl = x[0]
    for l in range(depth):
        wl = w_in[l]
        o = 0
        parts = []
        for width in (Q_RANK_A, KV_RANK_A, IDX_DIM, N_IDX_HEADS, W_B, W_B, W_B, W_C):
            parts.append(wl[:, o:o + width])
            o += width
        w_cq, w_ckv, w_ki, w_wi, w_qb, w_kb, w_vb, w_qc = parts
        n_small = Q_RANK_A + KV_RANK_A + IDX_DIM + N_IDX_HEADS
        pad = (-n_small) % LANES
        w_p1 = jnp.concatenate([w_cq, w_ckv, w_ki, w_wi * N_IDX_HEADS ** -0.5,
                                jnp.zeros((d, pad), F32)], axis=1).astype(BF16)
        w_p2 = jnp.concatenate([w_qb * scale, w_kb, w_vb, w_qc * scale], axis=1).astype(BF16)
        p1 = _mm(xl, w_p1, out_dtype=F32, tm=tm, tn=w_p1.shape[1])
        p2 = _mm(xl, w_p2, out_dtype=BF16, tm=tm, tn=w_p2.shape[1] // 2)
        c_q = p1[:, :Q_RANK_A]
        c_kv = p1[:, Q_RANK_A:Q_RANK_A + KV_RANK_A]
        k_i = p1[:, Q_RANK_A + KV_RANK_A:Q_RANK_A + KV_RANK_A + IDX_DIM]
        w_i = p1[:, Q_RANK_A + KV_RANK_A + IDX_DIM:n_small]
        q_b, k_b = p2[:, :W_B], p2[:, W_B:2 * W_B]
        v_b, q_c = p2[:, 2 * W_B:3 * W_B], p2[:, 3 * W_B:]

        w_q = jnp.concatenate([w_uq[l] * scale, w_uqi[l] * IDX_DIM ** -0.5], axis=1).astype(BF16)
        qq = _mm(c_q, w_q, out_dtype=BF16, tm=tm, tn=w_q.shape[1], norm_g=g_cq[l])
        kv = _mm(c_kv, w_ukv[l].astype(BF16), out_dtype=BF16, tm=tm, tn=2 * W_A, norm_g=g_ckv[l])
        q_a, q_i = qq[:, :W_A], qq[:, W_A:]
        k_a, v_a = kv[:, :W_A], kv[:, W_A:]
        kidx = _kidx_norm(k_i, g_kidx[l], b_kidx[l])
        qi_r = (q_i.reshape(t // TQ, TQ, N_IDX_HEADS, IDX_DIM).transpose(0, 2, 1, 3)
                .reshape(t // TQ, N_IDX_HEADS * TQ, IDX_DIM))
        o_a = _dsa(slopes_a, qi_r, w_i.T, kidx, q_a, k_a, v_a.T, top_k)
        o_b = _moba(slopes_b, q_b, k_b, v_b.T)
        o_c = _mem_attn(q_c, mk, mvt)

        merged = _merge(xl, w_gate[l].astype(BF16), b_gate[l], o_a, o_b, o_c,
                        w_up_a[l].astype(BF16), w_up_b[l].astype(BF16), w_up_c[l].astype(BF16),
                        tm=min(t, 1024), tn=512)

        n_route = N_GROUPS + N_EXPERTS
        w_r = jnp.concatenate([w_grp[l], w_rt[l]], axis=1)
        w_r_hi = w_r.astype(BF16)
        w_r_lo = (w_r - w_r_hi.astype(F32)).astype(BF16)
        lane_pad = jnp.zeros((d, ROUTE_LANES // 2 - n_route), BF16)
        w_r2 = jnp.concatenate([w_r_hi, lane_pad, w_r_lo, lane_pad], axis=1)
        b_r = jnp.concatenate([b_grp[l], b_rt[l],
                               jnp.zeros((ROUTE_LANES - n_route,), F32)]).reshape(1, ROUTE_LANES)
        x1, route = _oproj(xl, merged, w_o[l].astype(BF16), ln1_g[l], ln1_b[l], w_r2, b_r,
                           alpha=alpha, tm=tm)

        row_token, row_w, pos, tile_expert, n_used = _moe_plan(route, t)
        ys = _experts(row_token, tile_expert, n_used, x1, row_w, w1, w3, w2, l)
        xl = _combine(pos, x1, ln2_g[l], ln2_b[l], ys, alpha=alpha)
    return xl[None]
```

```python
import functools

import jax
import jax.numpy as jnp
from jax import lax
from jax.experimental import pallas as pl
from jax.experimental.pallas import tpu as pltpu

HEAD_DIM = 128
N_HEADS_A = 6
N_HEADS_B = 6
N_HEADS_C = 4
W_A = N_HEADS_A * HEAD_DIM
W_B = N_HEADS_B * HEAD_DIM
W_C = N_HEADS_C * HEAD_DIM
Q_RANK_A = 512
KV_RANK_A = 256
N_IDX_HEADS = 16
IDX_DIM = 64
TOPK_A_MAX = 256
MOBA_BLOCK = 256
MOBA_TOPK = 3
N_ALIBI = N_HEADS_A + N_HEADS_B
N_GROUPS = 4
EXPERTS_PER_GROUP = 8
N_EXPERTS = N_GROUPS * EXPERTS_PER_GROUP
D_EXPERT = 512
NORM_EPS = 1e-5
NEG = -1e30
LOG2E = 1.4426950408889634
I16_MIN = -(2 ** 15)

LANES = 128
SUBLANES = 8
PACKED_ROWS = 2 * SUBLANES
TQ = 256
CK = 256
ROUTE_LANES = 128
MOE_TM = 256
ROW_DMA_UNROLL = 8
VMEM_LIMIT = 56 * 1024 * 1024

F32 = jnp.float32
BF16 = jnp.bfloat16
I32 = jnp.int32
I16 = jnp.int16

_NT = (((1,), (1,)), ((), ()))


def _cparams(*sem):
    return pltpu.CompilerParams(dimension_semantics=sem, vmem_limit_bytes=VMEM_LIMIT)


def _resident(shape):
    nd = len(shape)
    return pl.BlockSpec(shape, lambda *_: (0,) * nd, pipeline_mode=pl.Buffered(1))


def _mm_kernel(*refs, has_norm, has_bias):
    x_ref, w_ref = refs[0], refs[1]
    k = 2
    x = x_ref[...]
    if has_norm:
        g_ref = refs[k]
        k += 1
        xf = x.astype(F32)
        x = xf * lax.rsqrt(jnp.mean(xf * xf, axis=-1, keepdims=True) + NORM_EPS) * g_ref[...]
    acc = jnp.dot(x.astype(BF16), w_ref[...], preferred_element_type=F32)
    if has_bias:
        acc = acc + refs[k][...]
        k += 1
    o_ref = refs[k]
    o_ref[...] = acc.astype(o_ref.dtype)


def _mm(x, w, *, out_dtype, tm, tn, norm_g=None, bias=None):
    m, kdim = x.shape
    n = w.shape[1]
    assert m % tm == 0 and n % tn == 0
    in_specs = [pl.BlockSpec((tm, kdim), lambda i, j: (i, 0)),
                pl.BlockSpec((kdim, tn), lambda i, j: (0, j))]
    args = [x, w]
    if norm_g is not None:
        in_specs.append(pl.BlockSpec((1, kdim), lambda i, j: (0, 0)))
        args.append(norm_g.reshape(1, kdim).astype(F32))
    if bias is not None:
        in_specs.append(pl.BlockSpec((1, tn), lambda i, j: (0, j)))
        args.append(bias.reshape(1, n).astype(F32))
    return pl.pallas_call(
        functools.partial(_mm_kernel, has_norm=norm_g is not None, has_bias=bias is not None),
        grid=(m // tm, n // tn),
        in_specs=in_specs,
        out_specs=pl.BlockSpec((tm, tn), lambda i, j: (i, j)),
        out_shape=jax.ShapeDtypeStruct((m, n), out_dtype),
        compiler_params=_cparams("parallel", "parallel"),
        name="mm",
    )(*args)


def _kidx_kernel(x_ref, g_ref, b_ref, o_ref):
    x = x_ref[...]
    mu = jnp.mean(x, axis=-1, keepdims=True)
    xc = x - mu
    var = jnp.mean(xc * xc, axis=-1, keepdims=True)
    o_ref[...] = (xc * lax.rsqrt(var + NORM_EPS) * g_ref[...] + b_ref[...]).astype(o_ref.dtype)


def _kidx_norm(k_i, g, b):
    t, d = k_i.shape
    tm = min(t, 1024)
    return pl.pallas_call(
        _kidx_kernel,
        grid=(t // tm,),
        in_specs=[pl.BlockSpec((tm, d), lambda i: (i, 0)),
                  pl.BlockSpec((1, d), lambda i: (0, 0)),
                  pl.BlockSpec((1, d), lambda i: (0, 0))],
        out_specs=pl.BlockSpec((tm, d), lambda i: (i, 0)),
        out_shape=jax.ShapeDtypeStruct((t, d), BF16),
        compiler_params=_cparams("parallel"),
        name="kidx_norm",
    )(k_i, g.reshape(1, d), b.reshape(1, d))


def _head(h):
    return slice(h * HEAD_DIM, (h + 1) * HEAD_DIM)


def _attn_state(n_heads):
    return ([pltpu.VMEM((CK, TQ), F32) for _ in range(2 * n_heads)]
            + [pltpu.VMEM((HEAD_DIM, TQ), F32) for _ in range(n_heads)]
            + [pltpu.VMEM((1, TQ), F32) for _ in range(2 * n_heads)])


def _attn_reset(state, n_heads):
    state = state[2 * n_heads:]
    for h in range(n_heads):
        state[h][...] = jnp.zeros((HEAD_DIM, TQ), F32)
        state[n_heads + h][...] = jnp.full((1, TQ), NEG, F32)
        state[2 * n_heads + h][...] = jnp.zeros((1, TQ), F32)


def _attn_run(state, n_heads, i, q_ref, k_ref, consume):
    def scores(j, slot):
        start = pl.multiple_of(j * CK, CK)
        for h in range(n_heads):
            state[slot * n_heads + h][...] = lax.dot_general(
                k_ref[pl.ds(start, CK), _head(h)], q_ref[:, _head(h)], _NT,
                preferred_element_type=F32)

    scores(0, 0)

    def pair(p, c):
        j = 2 * p
        scores(j + 1, 1)
        consume(j, 0, False)
        scores(j + 2, 0)
        consume(j + 1, 1, False)
        return c

    lax.fori_loop(0, i // 2, pair, 0)

    @pl.when(i % 2 == 0)
    def _():
        consume(i, 0, True)

    @pl.when(i % 2 == 1)
    def _():
        scores(i, 1)
        consume(i - 1, 0, False)
        consume(i, 1, True)


def _attn_update(state, n_heads, h, slot, vt_blk, t_bias, c):
    s_ref, state = state[slot * n_heads + h], state[2 * n_heads:]
    acc_ref, m_ref, l_ref = state[h], state[n_heads + h], state[2 * n_heads + h]
    t = s_ref[...] + t_bias
    m_old = m_ref[...]
    m_new = jnp.maximum(m_old, jnp.max(t, axis=0, keepdims=True) + c)
    alpha = jnp.exp2(m_old - m_new)
    p = jnp.exp2(t - (m_new - c))
    l_ref[...] = alpha * l_ref[...] + jnp.sum(p, axis=0, keepdims=True)
    acc_ref[...] = alpha * acc_ref[...] + jnp.dot(vt_blk, p.astype(BF16), preferred_element_type=F32)
    m_ref[...] = m_new


def _attn_finish(state, n_heads, h):
    state = state[2 * n_heads:]
    return (state[h][...] * (1.0 / state[2 * n_heads + h][...])).T


def _key_offsets():
    return lax.broadcasted_iota(I32, (CK, TQ), 0), lax.broadcasted_iota(I32, (CK, TQ), 1)


def _sortable(x):
    b = pltpu.bitcast(x, I32)
    return b ^ ((b >> 31) & 0x7FFFFFFF)


def _dsa_kernel(slopes_ref, qi_ref, wi_ref, kidx_ref, qa_ref, ka_ref, vat_ref, o_ref,
                keys_ref, half_ref, sb_ref, *state, top_k):
    i = pl.program_id(0)
    rows, cols = _key_offsets()
    causal = rows <= cols
    nh = N_HEADS_A

    @pl.when(i == 0)
    def _():
        rows_f = rows.astype(F32)
        for h in range(nh):
            sb_ref[h] = slopes_ref[h] * rows_f

    def score_chunk(j, diagonal):
        start = pl.multiple_of(j * CK, CK)
        kc = kidx_ref[pl.ds(start, CK), :]
        acc = jnp.zeros((CK, TQ), F32)
        for h in range(N_IDX_HEADS):
            z = lax.dot_general(kc, qi_ref[0, h * TQ:(h + 1) * TQ, :], _NT,
                                preferred_element_type=F32)
            acc = acc + wi_ref[h:h + 1, :] * jnp.maximum(z, 0.0)
        if diagonal:
            acc = jnp.where(causal, acc, NEG)
        key = _sortable(acc)
        keys_ref[pl.ds(start, CK), :] = key
        half_ref[pl.ds(start, CK), :] = (key >> 16).astype(I16)

    def score_body(j, c):
        score_chunk(j, False)
        return c

    lax.fori_loop(0, i, score_body, 0)
    score_chunk(i, True)

    def count16(cand, strict):
        cand = cand.astype(I16)

        def body(j, cnt):
            blk = half_ref[pl.ds(pl.multiple_of(j * CK, CK), CK), :]
            hit = jnp.where((blk > cand) if strict else (blk >= cand), jnp.int16(1), jnp.int16(0))
            for r in range(0, CK, PACKED_ROWS):
                cnt = cnt + hit[r:r + PACKED_ROWS, :]
            return cnt

        cnt = lax.fori_loop(0, i + 1, body, jnp.zeros((PACKED_ROWS, TQ), I16))
        return jnp.sum(cnt.astype(I32), axis=0, keepdims=True)

    def search16(target):
        def bit_body(b, thr):
            cand = thr + jnp.left_shift(jnp.int32(1), 15 - b)
            return jnp.where(count16(cand, False) >= target, cand, thr)
        return lax.fori_loop(0, 16, bit_body, jnp.full((1, TQ), I16_MIN, I32))

    t_hi = search16(top_k)
    need = top_k - count16(t_hi, True)

    def low_half(j, c):
        start = pl.multiple_of(j * CK, CK)
        key = keys_ref[pl.ds(start, CK), :]
        low = ((key ^ 0x8000) << 16) >> 16
        half_ref[pl.ds(start, CK), :] = jnp.where((key >> 16) == t_hi, low, I16_MIN).astype(I16)
        return c

    lax.fori_loop(0, i + 1, low_half, 0)
    t_lo = search16(need)
    thr = (t_hi << 16) + (t_lo - I16_MIN)

    _attn_reset(state, nh)

    def consume(j, slot, diagonal):
        start = pl.multiple_of(j * CK, CK)
        sel = keys_ref[pl.ds(start, CK), :] >= thr
        if diagonal:
            sel = jnp.logical_and(sel, causal)
        mask_bias = jnp.where(sel, 0.0, NEG)
        off = ((j - i) * CK).astype(F32)
        for h in range(nh):
            _attn_update(state, nh, h, slot, vat_ref[_head(h), pl.ds(start, CK)],
                         sb_ref[h] + mask_bias, slopes_ref[h] * off)

    _attn_run(state, nh, i, qa_ref, ka_ref, consume)
    for h in range(nh):
        o_ref[:, _head(h)] = _attn_finish(state, nh, h).astype(o_ref.dtype)


def _dsa(slopes, qi_r, wi_t, kidx, qa, ka, vat, top_k):
    t, w = qa.shape
    nh = w // HEAD_DIM
    return pl.pallas_call(
        functools.partial(_dsa_kernel, top_k=top_k),
        grid_spec=pltpu.PrefetchScalarGridSpec(
            num_scalar_prefetch=0,
            grid=(t // TQ,),
            in_specs=[
                pl.BlockSpec(memory_space=pltpu.SMEM),
                pl.BlockSpec((1, N_IDX_HEADS * TQ, IDX_DIM), lambda i: (i, 0, 0)),
                pl.BlockSpec((N_IDX_HEADS, TQ), lambda i: (0, i)),
                _resident(kidx.shape),
                pl.BlockSpec((TQ, w), lambda i: (i, 0)),
                _resident(ka.shape),
                _resident(vat.shape),
            ],
            out_specs=pl.BlockSpec((TQ, w), lambda i: (i, 0)),
            scratch_shapes=[pltpu.VMEM((t, TQ), I32), pltpu.VMEM((t, TQ), I16),
                            pltpu.VMEM((nh, CK, TQ), F32)] + _attn_state(nh),
        ),
        out_shape=jax.ShapeDtypeStruct((t, w), BF16),
        compiler_params=_cparams("arbitrary"),
        name="dsa",
    )(slopes, qi_r, wi_t, kidx, qa, ka, vat)


def _moba_kernel(slopes_ref, qb_ref, kb_ref, vbt_ref, o_ref, kmean_ref, sel_ref, sb_ref, *state, n_kb):
    i = pl.program_id(0)
    rows, cols = _key_offsets()
    nh = N_HEADS_B

    @pl.when(i == 0)
    def _():
        rows_f = rows.astype(F32)
        for h in range(nh):
            sb_ref[h] = slopes_ref[h] * rows_f
            for n in range(n_kb):
                blk = kb_ref[n * CK:(n + 1) * CK, _head(h)].astype(F32)
                kmean_ref[h, n:n + 1, :] = jnp.mean(blk, axis=0, keepdims=True)

    blk_id = lax.broadcasted_iota(I32, (n_kb, TQ), 0)
    for h in range(nh):
        q_h = qb_ref[:, _head(h)]
        km = kmean_ref[h]
        km_hi = km.astype(BF16)
        km_lo = (km - km_hi.astype(F32)).astype(BF16)
        gate = (lax.dot_general(km_hi, q_h, _NT, preferred_element_type=F32)
                + lax.dot_general(km_lo, q_h, _NT, preferred_element_type=F32))
        gate = jnp.where(blk_id < i, gate, NEG)
        sel = jnp.full((n_kb, TQ), NEG, F32)
        for _ in range(MOBA_TOPK):
            best = jnp.max(gate, axis=0, keepdims=True)
            first = jnp.min(jnp.where(gate == best, blk_id, n_kb), axis=0, keepdims=True)
            pick = blk_id == first
            sel = jnp.where(pick, 0.0, sel)
            gate = jnp.where(pick, -jnp.inf, gate)
        sel_ref[h] = jnp.where(blk_id < i, sel, NEG)

    _attn_reset(state, nh)

    def consume(j, slot, diagonal):
        start = pl.multiple_of(j * CK, CK)
        off = ((j - i) * CK).astype(F32)
        for h in range(nh):
            if diagonal:
                t_bias = jnp.where(rows <= cols, sb_ref[h], NEG)
            else:
                t_bias = sb_ref[h] + sel_ref[h, pl.ds(j, 1), :]
            _attn_update(state, nh, h, slot, vbt_ref[_head(h), pl.ds(start, CK)],
                         t_bias, slopes_ref[h] * off)

    _attn_run(state, nh, i, qb_ref, kb_ref, consume)
    for h in range(nh):
        o_ref[:, _head(h)] = _attn_finish(state, nh, h).astype(o_ref.dtype)


def _moba(slopes, qb, kb, vbt):
    t, w = qb.shape
    nh = w // HEAD_DIM
    assert t % MOBA_BLOCK == 0 and TQ == MOBA_BLOCK and CK == MOBA_BLOCK
    n_kb = t // MOBA_BLOCK
    return pl.pallas_call(
        functools.partial(_moba_kernel, n_kb=n_kb),
        grid_spec=pltpu.PrefetchScalarGridSpec(
            num_scalar_prefetch=0,
            grid=(t // TQ,),
            in_specs=[
                pl.BlockSpec(memory_space=pltpu.SMEM),
                pl.BlockSpec((TQ, w), lambda i: (i, 0)),
                _resident(kb.shape),
                _resident(vbt.shape),
            ],
            out_specs=pl.BlockSpec((TQ, w), lambda i: (i, 0)),
            scratch_shapes=[pltpu.VMEM((nh, n_kb, HEAD_DIM), F32), pltpu.VMEM((nh, n_kb, TQ), F32),
                            pltpu.VMEM((nh, CK, TQ), F32)] + _attn_state(nh),
        ),
        out_shape=jax.ShapeDtypeStruct((t, w), BF16),
        compiler_params=_cparams("arbitrary"),
        name="moba",
    )(slopes, qb, kb, vbt)


def _mem_kernel(qc_ref, mk_ref, mvt_ref, o_ref):
    for h in range(N_HEADS_C):
        s = lax.dot_general(mk_ref[:, _head(h)], qc_ref[:, _head(h)], _NT,
                            preferred_element_type=F32)
        m = jnp.max(s, axis=0, keepdims=True)
        p = jnp.exp2(s - m)
        l = jnp.sum(p, axis=0, keepdims=True)
        acc = jnp.dot(mvt_ref[_head(h), :], p.astype(BF16), preferred_element_type=F32)
        o_ref[:, _head(h)] = (acc * (1.0 / l)).T.astype(o_ref.dtype)


def _mem_attn(qc, mk, mvt):
    t, w = qc.shape
    return pl.pallas_call(
        _mem_kernel,
        grid=(t // TQ,),
        in_specs=[pl.BlockSpec((TQ, w), lambda i: (i, 0)),
                  _resident(mk.shape), _resident(mvt.shape)],
        out_specs=pl.BlockSpec((TQ, w), lambda i: (i, 0)),
        out_shape=jax.ShapeDtypeStruct((t, w), BF16),
        compiler_params=_cparams("parallel"),
        name="mem_attn",
    )(qc, mk, mvt)


def _sigmoid(x):
    return 1.0 / (1.0 + jnp.exp(-x))


def _merge_kernel(x_ref, wga_ref, wgb_ref, wgc_ref, bga_ref, bgb_ref, bgc_ref,
                  oa_ref, ob_ref, oc_ref, wua_ref, wub_ref, wuc_ref, o_ref):
    xb = x_ref[...].astype(BF16)

    def branch(wg_ref, bg_ref, oo_ref, wu_ref):
        g = _sigmoid(jnp.dot(xb, wg_ref[...], preferred_element_type=F32) + bg_ref[...])
        return g * jnp.dot(oo_ref[...], wu_ref[...], preferred_element_type=F32)

    acc = branch(wga_ref, bga_ref, oa_ref, wua_ref)
    acc = acc + branch(wgb_ref, bgb_ref, ob_ref, wub_ref)
    acc = acc + branch(wgc_ref, bgc_ref, oc_ref, wuc_ref)
    o_ref[...] = acc.astype(o_ref.dtype)


def _merge(x, w_gate, b_gate, o_a, o_b, o_c, w_up_a, w_up_b, w_up_c, *, tm, tn):
    t, d = x.shape
    nb = d // tn
    b_gate = b_gate.reshape(1, 3 * d)

    def wg(k):
        return pl.BlockSpec((d, tn), lambda i, j, k=k: (0, j + k * nb))

    def bg(k):
        return pl.BlockSpec((1, tn), lambda i, j, k=k: (0, j + k * nb))

    def act(w):
        return pl.BlockSpec((tm, w), lambda i, j: (i, 0))

    def wu(w):
        return pl.BlockSpec((w, tn), lambda i, j: (0, j))

    return pl.pallas_call(
        _merge_kernel,
        grid=(t // tm, nb),
        in_specs=[pl.BlockSpec((tm, d), lambda i, j: (i, 0)),
                  wg(0), wg(1), wg(2), bg(0), bg(1), bg(2),
                  act(W_A), act(W_B), act(W_C), wu(W_A), wu(W_B), wu(W_C)],
        out_specs=pl.BlockSpec((tm, tn), lambda i, j: (i, j)),
        out_shape=jax.ShapeDtypeStruct((t, d), BF16),
        compiler_params=_cparams("parallel", "parallel"),
        name="merge",
    )(x, w_gate, w_gate, w_gate, b_gate, b_gate, b_gate, o_a, o_b, o_c, w_up_a, w_up_b, w_up_c)


def _layer_norm(y, g, b):
    mu = jnp.mean(y, axis=-1, keepdims=True)
    yc = y - mu
    var = jnp.mean(yc * yc, axis=-1, keepdims=True)
    return yc * lax.rsqrt(var + NORM_EPS) * g + b


def _route(logits):
    lane = lax.broadcasted_iota(I32, logits.shape, 1)
    is_g = lane < N_GROUPS
    gl = jnp.where(is_g, logits, -jnp.inf)
    gmax = jnp.max(gl, axis=-1, keepdims=True)
    g_sel = jnp.min(jnp.where(gl == gmax, lane, ROUTE_LANES), axis=-1, keepdims=True)
    p_g = 1.0 / jnp.sum(jnp.where(is_g, jnp.exp(gl - gmax), 0.0), axis=-1, keepdims=True)
    e_id = lane - N_GROUPS
    in_grp = jnp.logical_and(e_id >= g_sel * EXPERTS_PER_GROUP, e_id < (g_sel + 1) * EXPERTS_PER_GROUP)
    el = jnp.where(in_grp, logits, -jnp.inf)
    emax = jnp.max(el, axis=-1, keepdims=True)
    ex = jnp.where(in_grp, jnp.exp(el - emax), 0.0)
    pe = ex / jnp.sum(ex, axis=-1, keepdims=True)
    pe = jnp.where(in_grp, pe, -1.0)
    p1 = jnp.max(pe, axis=-1, keepdims=True)
    i1 = jnp.min(jnp.where(pe == p1, e_id, ROUTE_LANES), axis=-1, keepdims=True)
    pe2 = jnp.where(e_id == i1, -1.0, pe)
    p2 = jnp.max(pe2, axis=-1, keepdims=True)
    i2 = jnp.min(jnp.where(pe2 == p2, e_id, ROUTE_LANES), axis=-1, keepdims=True)
    denom = p1 + p2
    w1 = p_g * (p1 / denom)
    w2 = p_g * (p2 / denom)
    return jnp.where(lane == 0, w1,
                     jnp.where(lane == 1, w2,
                               jnp.where(lane == 2, i1.astype(F32),
                                         jnp.where(lane == 3, i2.astype(F32), 0.0))))


def _oproj_kernel(x_ref, m_ref, wo_ref, g_ref, b_ref, wr_ref, br_ref, x1_ref, r_ref, *, alpha):
    y = alpha * x_ref[...] + jnp.dot(m_ref[...], wo_ref[...], preferred_element_type=F32)
    x1 = _layer_norm(y, g_ref[...], b_ref[...])
    x1_ref[...] = x1
    x_hi = x1.astype(BF16)
    x_lo = (x1 - x_hi.astype(F32)).astype(BF16)
    parts = (jnp.dot(x_hi, wr_ref[...], preferred_element_type=F32)
             + jnp.dot(x_lo, wr_ref[...], preferred_element_type=F32))
    logits = parts + pltpu.roll(parts, ROUTE_LANES // 2, axis=1) + br_ref[...]
    r_ref[...] = _route(logits)


def _oproj(x, merged, w_o, ln_g, ln_b, w_r, b_r, *, alpha, tm):
    t, d = x.shape
    return pl.pallas_call(
        functools.partial(_oproj_kernel, alpha=alpha),
        grid=(t // tm,),
        in_specs=[pl.BlockSpec((tm, d), lambda i: (i, 0)),
                  pl.BlockSpec((tm, d), lambda i: (i, 0)),
                  _resident(w_o.shape),
                  pl.BlockSpec((1, d), lambda i: (0, 0)),
                  pl.BlockSpec((1, d), lambda i: (0, 0)),
                  _resident(w_r.shape),
                  pl.BlockSpec((1, ROUTE_LANES), lambda i: (0, 0))],
        out_specs=[pl.BlockSpec((tm, d), lambda i: (i, 0)),
                   pl.BlockSpec((tm, ROUTE_LANES), lambda i: (i, 0))],
        out_shape=[jax.ShapeDtypeStruct((t, d), F32),
                   jax.ShapeDtypeStruct((t, ROUTE_LANES), F32)],
        compiler_params=_cparams("parallel"),
        name="oproj",
    )(x, merged, w_o, ln_g.reshape(1, d), ln_b.reshape(1, d), w_r, b_r)


def _row_copy(src_hbm, row, dst, k, sem):
    return pltpu.make_async_copy(src_hbm.at[pl.ds(row, 1), :], dst.at[pl.ds(k, 1), :], sem)


def _expert_kernel(tok_ref, te_ref, nu_ref, x_hbm, w1_ref, w3_ref, w2_ref, o_ref, buf, sem):
    i = pl.program_id(0)
    n_used = nu_ref[0]
    slot = lax.rem(i, 2)

    def gather(tile, s):
        base = tile * MOE_TM

        def body(r, c):
            _row_copy(x_hbm, tok_ref[base + r], buf.at[s], r, sem.at[s]).start()
            return c

        lax.fori_loop(0, MOE_TM, body, 0, unroll=ROW_DMA_UNROLL)

    @pl.when(i == 0)
    def _():
        gather(0, 0)

    @pl.when(i < n_used)
    def _():
        pltpu.make_async_copy(x_hbm.at[pl.ds(0, MOE_TM), :], buf.at[slot], sem.at[slot]).wait()

        @pl.when(i + 1 < n_used)
        def _():
            gather(i + 1, 1 - slot)

        xs = buf[slot].astype(BF16)
        h1 = jnp.dot(xs, w1_ref[0, 0].astype(BF16), preferred_element_type=F32)
        h3 = jnp.dot(xs, w3_ref[0, 0].astype(BF16), preferred_element_type=F32)
        hid = (h1 * _sigmoid(h1)) * h3
        o_ref[...] = jnp.dot(hid.astype(BF16), w2_ref[0, 0].astype(BF16), preferred_element_type=F32)

    @pl.when(i >= n_used)
    def _():
        o_ref[...] = jnp.zeros_like(o_ref)


def _experts(row_token, tile_expert, n_used, x1, w1, w3, w2, layer):
    r = row_token.shape[0]
    d = x1.shape[1]
    de = w1.shape[-1]

    def wspec(a, b):
        return pl.BlockSpec((1, 1, a, b), lambda i, tok, te, nu: (layer, te[i], 0, 0))

    return pl.pallas_call(
        _expert_kernel,
        grid_spec=pltpu.PrefetchScalarGridSpec(
            num_scalar_prefetch=3,
            grid=(r // MOE_TM,),
            in_specs=[pl.BlockSpec(memory_space=pl.ANY), wspec(d, de), wspec(d, de), wspec(de, d)],
            out_specs=pl.BlockSpec((MOE_TM, d), lambda i, tok, te, nu: (i, 0)),
            scratch_shapes=[pltpu.VMEM((2, MOE_TM, d), F32), pltpu.SemaphoreType.DMA((2,))],
        ),
        out_shape=jax.ShapeDtypeStruct((r, d), F32),
        compiler_params=_cparams("arbitrary"),
        name="moe_experts",
    )(row_token, tile_expert, n_used, x1, w1, w3, w2)


def _combine_kernel(pos_ref, x1_ref, route_ref, g_ref, b_ref, ys_hbm, o_ref, buf, sem, *, alpha):
    i = pl.program_id(0)
    slot = lax.rem(i, 2)

    def gather(tile, s):
        base = tile * MOE_TM

        def body(r, c):
            for k in range(2):
                _row_copy(ys_hbm, pos_ref[2 * (base + r) + k], buf.at[s, k], r, sem.at[s]).start()
            return c

        lax.fori_loop(0, MOE_TM, body, 0, unroll=ROW_DMA_UNROLL // 2)

    @pl.when(i == 0)
    def _():
        gather(0, 0)

    for k in range(2):
        pltpu.make_async_copy(ys_hbm.at[pl.ds(0, MOE_TM), :], buf.at[slot, k], sem.at[slot]).wait()

    @pl.when(i + 1 < pl.num_programs(0))
    def _():
        gather(i + 1, 1 - slot)

    route = route_ref[...]
    y = alpha * x1_ref[...] + (route[:, 0:1] * buf[slot, 0] + route[:, 1:2] * buf[slot, 1])
    o_ref[...] = _layer_norm(y, g_ref[...], b_ref[...])


def _combine(pos, x1, route, ln_g, ln_b, ys, *, alpha):
    t, d = x1.shape
    return pl.pallas_call(
        functools.partial(_combine_kernel, alpha=alpha),
        grid_spec=pltpu.PrefetchScalarGridSpec(
            num_scalar_prefetch=1,
            grid=(t // MOE_TM,),
            in_specs=[pl.BlockSpec((MOE_TM, d), lambda i, pos: (i, 0)),
                      pl.BlockSpec((MOE_TM, ROUTE_LANES), lambda i, pos: (i, 0)),
                      pl.BlockSpec((1, d), lambda i, pos: (0, 0)),
                      pl.BlockSpec((1, d), lambda i, pos: (0, 0)),
                      pl.BlockSpec(memory_space=pl.ANY)],
            out_specs=pl.BlockSpec((MOE_TM, d), lambda i, pos: (i, 0)),
            scratch_shapes=[pltpu.VMEM((2, 2, MOE_TM, d), F32), pltpu.SemaphoreType.DMA((2,))],
        ),
        out_shape=jax.ShapeDtypeStruct((t, d), F32),
        compiler_params=_cparams("arbitrary"),
        name="moe_combine",
    )(pos, x1, route, ln_g.reshape(1, d), ln_b.reshape(1, d), ys)


def _moe_plan(route, t):
    flat_e = route[:, 2:4].astype(I32).reshape(-1)
    n = flat_e.shape[0]
    n_rows = n + N_EXPERTS * MOE_TM
    n_tiles = n_rows // MOE_TM
    onehot = (flat_e[:, None] == jnp.arange(N_EXPERTS, dtype=I32)[None, :]).astype(I32)
    running = jnp.cumsum(onehot, axis=0)
    rank = jnp.sum(onehot * running, axis=1) - 1
    counts = running[-1]
    tiles_per = (counts + MOE_TM - 1) // MOE_TM
    tile_end = jnp.cumsum(tiles_per)
    grp_row0 = (tile_end - tiles_per) * MOE_TM
    pos = jnp.sum(onehot * grp_row0[None, :], axis=1) + rank
    row_token = jnp.zeros((n_rows,), I32).at[pos].set(jnp.arange(n, dtype=I32) // 2, unique_indices=True)
    n_used = tile_end[-1]
    tile_ids = jnp.minimum(jnp.arange(n_tiles, dtype=I32), n_used - 1)
    tile_expert = jnp.sum((tile_end[None, :] <= tile_ids[:, None]).astype(I32), axis=1)
    tile_expert = jnp.minimum(tile_expert, N_EXPERTS - 1)
    return row_token, pos, tile_expert, n_used.reshape(1).astype(I32)


def kernel(x, mem, w_mem_kv, w_in, g_cq, g_ckv, g_kidx, b_kidx, w_uq, w_uqi, w_ukv,
           w_up_a, w_up_b, w_up_c, w_gate, b_gate, w_o, ln1_g, ln1_b,
           w_grp, b_grp, w_rt, b_rt, w1, w3, w2, ln2_g, ln2_b):
    bsz, t, d = x.shape
    assert bsz == 1 and t % TQ == 0
    depth = w_in.shape[0]
    alpha = (2 * depth) ** 0.25
    scale = HEAD_DIM ** -0.5 * LOG2E
    top_k = min(TOPK_A_MAX, t // 4)
    n_mem = mem.shape[1]
    tm = min(t, 512)

    slopes = LOG2E * 2.0 ** (-8.0 * jnp.arange(1, N_ALIBI + 1, dtype=F32) / N_ALIBI)
    slopes_a, slopes_b = slopes[0::2], slopes[1::2]

    mkv = _mm(mem[0], w_mem_kv.astype(BF16), out_dtype=BF16, tm=n_mem, tn=W_C)
    mk = mkv[:, :W_C]
    mvt = mkv[:, W_C:].T

    xl = x[0]
    for l in range(depth):
        wl = w_in[l]
        o = 0
        parts = []
        for width in (Q_RANK_A, KV_RANK_A, IDX_DIM, N_IDX_HEADS, W_B, W_B, W_B, W_C):
            parts.append(wl[:, o:o + width])
            o += width
        w_cq, w_ckv, w_ki, w_wi, w_qb, w_kb, w_vb, w_qc = parts
        n_small = Q_RANK_A + KV_RANK_A + IDX_DIM + N_IDX_HEADS
        pad = (-n_small) % LANES
        w_p1 = jnp.concatenate([w_cq, w_ckv, w_ki, w_wi * N_IDX_HEADS ** -0.5,
                                jnp.zeros((d, pad), F32)], axis=1).astype(BF16)
        w_p2 = jnp.concatenate([w_qb * scale, w_kb, w_vb, w_qc * scale], axis=1).astype(BF16)
        p1 = _mm(xl, w_p1, out_dtype=F32, tm=tm, tn=w_p1.shape[1])
        p2 = _mm(xl, w_p2, out_dtype=BF16, tm=tm, tn=w_p2.shape[1] // 2)
        c_q = p1[:, :Q_RANK_A]
        c_kv = p1[:, Q_RANK_A:Q_RANK_A + KV_RANK_A]
        k_i = p1[:, Q_RANK_A + KV_RANK_A:Q_RANK_A + KV_RANK_A + IDX_DIM]
        w_i = p1[:, Q_RANK_A + KV_RANK_A + IDX_DIM:n_small]
        q_b, k_b = p2[:, :W_B], p2[:, W_B:2 * W_B]
        v_b, q_c = p2[:, 2 * W_B:3 * W_B], p2[:, 3 * W_B:]

        w_q = jnp.concatenate([w_uq[l] * scale, w_uqi[l] * IDX_DIM ** -0.5], axis=1).astype(BF16)
        qq = _mm(c_q, w_q, out_dtype=BF16, tm=tm, tn=w_q.shape[1], norm_g=g_cq[l])
        kv = _mm(c_kv, w_ukv[l].astype(BF16), out_dtype=BF16, tm=tm, tn=2 * W_A, norm_g=g_ckv[l])
        q_a, q_i = qq[:, :W_A], qq[:, W_A:]
        k_a, v_a = kv[:, :W_A], kv[:, W_A:]
        kidx = _kidx_norm(k_i, g_kidx[l], b_kidx[l])
        qi_r = (q_i.reshape(t // TQ, TQ, N_IDX_HEADS, IDX_DIM).transpose(0, 2, 1, 3)
                .reshape(t // TQ, N_IDX_HEADS * TQ, IDX_DIM))
        o_a = _dsa(slopes_a, qi_r, w_i.T, kidx, q_a, k_a, v_a.T, top_k)
        o_b = _moba(slopes_b, q_b, k_b, v_b.T)
        o_c = _mem_attn(q_c, mk, mvt)

        merged = _merge(xl, w_gate[l].astype(BF16), b_gate[l], o_a, o_b, o_c,
                        w_up_a[l].astype(BF16), w_up_b[l].astype(BF16), w_up_c[l].astype(BF16),
                        tm=min(t, 1024), tn=512)

        n_route = N_GROUPS + N_EXPERTS
        w_r = jnp.concatenate([w_grp[l], w_rt[l]], axis=1)
        w_r_hi = w_r.astype(BF16)
        w_r_lo = (w_r - w_r_hi.astype(F32)).astype(BF16)
        lane_pad = jnp.zeros((d, ROUTE_LANES // 2 - n_route), BF16)
        w_r2 = jnp.concatenate([w_r_hi, lane_pad, w_r_lo, lane_pad], axis=1)
        b_r = jnp.concatenate([b_grp[l], b_rt[l],
                               jnp.zeros((ROUTE_LANES - n_route,), F32)]).reshape(1, ROUTE_LANES)
        x1, route = _oproj(xl, merged, w_o[l].astype(BF16), ln1_g[l], ln1_b[l], w_r2, b_r,
                           alpha=alpha, tm=tm)

        row_token, pos, tile_expert, n_used = _moe_plan(route, t)
        ys = _experts(row_token, tile_expert, n_used, x1, w1, w3, w2, l)
        xl = _combine(pos, x1, route, ln2_g[l], ln2_b[l], ys, alpha=alpha)
    return xl[None]
```

```python
import functools

import jax
import jax.numpy as jnp
from jax import lax
from jax.experimental import pallas as pl
from jax.experimental.pallas import tpu as pltpu

HEAD_DIM = 128
N_HEADS_A = 6
N_HEADS_B = 6
N_HEADS_C = 4
W_A = N_HEADS_A * HEAD_DIM
W_B = N_HEADS_B * HEAD_DIM
W_C = N_HEADS_C * HEAD_DIM
Q_RANK_A = 512
KV_RANK_A = 256
N_IDX_HEADS = 16
IDX_DIM = 64
TOPK_A_MAX = 256
MOBA_BLOCK = 256
MOBA_TOPK = 3
N_ALIBI = N_HEADS_A + N_HEADS_B
N_GROUPS = 4
EXPERTS_PER_GROUP = 8
N_EXPERTS = N_GROUPS * EXPERTS_PER_GROUP
D_EXPERT = 512
NORM_EPS = 1e-5
NEG = -1e30
LOG2E = 1.4426950408889634
I16_MIN = -(2 ** 15)

LANES = 128
SUBLANES = 8
PACKED_ROWS = 2 * SUBLANES
TQ = 256
CK = 256
ROUTE_LANES = 128
MOE_TM = 256
ROW_DMA_UNROLL = 8
VMEM_LIMIT = 56 * 1024 * 1024

F32 = jnp.float32
BF16 = jnp.bfloat16
I32 = jnp.int32
I16 = jnp.int16

_NT = (((1,), (1,)), ((), ()))


def _cparams(*sem):
    return pltpu.CompilerParams(dimension_semantics=sem, vmem_limit_bytes=VMEM_LIMIT)


def _resident(shape):
    nd = len(shape)
    return pl.BlockSpec(shape, lambda *_: (0,) * nd, pipeline_mode=pl.Buffered(1))


def _mm_kernel(*refs, has_norm, has_bias):
    x_ref, w_ref = refs[0], refs[1]
    k = 2
    x = x_ref[...]
    if has_norm:
        g_ref = refs[k]
        k += 1
        xf = x.astype(F32)
        x = xf * lax.rsqrt(jnp.mean(xf * xf, axis=-1, keepdims=True) + NORM_EPS) * g_ref[...]
    acc = jnp.dot(x.astype(BF16), w_ref[...], preferred_element_type=F32)
    if has_bias:
        acc = acc + refs[k][...]
        k += 1
    o_ref = refs[k]
    o_ref[...] = acc.astype(o_ref.dtype)


def _mm(x, w, *, out_dtype, tm, tn, norm_g=None, bias=None):
    m, kdim = x.shape
    n = w.shape[1]
    assert m % tm == 0 and n % tn == 0
    in_specs = [pl.BlockSpec((tm, kdim), lambda i, j: (i, 0)),
                pl.BlockSpec((kdim, tn), lambda i, j: (0, j))]
    args = [x, w]
    if norm_g is not None:
        in_specs.append(pl.BlockSpec((1, kdim), lambda i, j: (0, 0)))
        args.append(norm_g.reshape(1, kdim).astype(F32))
    if bias is not None:
        in_specs.append(pl.BlockSpec((1, tn), lambda i, j: (0, j)))
        args.append(bias.reshape(1, n).astype(F32))
    return pl.pallas_call(
        functools.partial(_mm_kernel, has_norm=norm_g is not None, has_bias=bias is not None),
        grid=(m // tm, n // tn),
        in_specs=in_specs,
        out_specs=pl.BlockSpec((tm, tn), lambda i, j: (i, j)),
        out_shape=jax.ShapeDtypeStruct((m, n), out_dtype),
        compiler_params=_cparams("parallel", "parallel"),
        name="mm",
    )(*args)


def _kidx_kernel(x_ref, g_ref, b_ref, o_ref):
    x = x_ref[...]
    mu = jnp.mean(x, axis=-1, keepdims=True)
    xc = x - mu
    var = jnp.mean(xc * xc, axis=-1, keepdims=True)
    o_ref[...] = (xc * lax.rsqrt(var + NORM_EPS) * g_ref[...] + b_ref[...]).astype(o_ref.dtype)


def _kidx_norm(k_i, g, b):
    t, d = k_i.shape
    tm = min(t, 1024)
    return pl.pallas_call(
        _kidx_kernel,
        grid=(t // tm,),
        in_specs=[pl.BlockSpec((tm, d), lambda i: (i, 0)),
                  pl.BlockSpec((1, d), lambda i: (0, 0)),
                  pl.BlockSpec((1, d), lambda i: (0, 0))],
        out_specs=pl.BlockSpec((tm, d), lambda i: (i, 0)),
        out_shape=jax.ShapeDtypeStruct((t, d), BF16),
        compiler_params=_cparams("parallel"),
        name="kidx_norm",
    )(k_i, g.reshape(1, d), b.reshape(1, d))


def _head(h):
    return slice(h * HEAD_DIM, (h + 1) * HEAD_DIM)


def _attn_state(n_heads):
    return ([pltpu.VMEM((CK, TQ), F32) for _ in range(2 * n_heads)]
            + [pltpu.VMEM((HEAD_DIM, TQ), F32) for _ in range(n_heads)]
            + [pltpu.VMEM((1, TQ), F32) for _ in range(2 * n_heads)])


def _attn_reset(state, n_heads):
    state = state[2 * n_heads:]
    for h in range(n_heads):
        state[h][...] = jnp.zeros((HEAD_DIM, TQ), F32)
        state[n_heads + h][...] = jnp.full((1, TQ), NEG, F32)
        state[2 * n_heads + h][...] = jnp.zeros((1, TQ), F32)


def _attn_run(state, n_heads, i, q_ref, k_ref, consume):
    def scores(j, slot):
        start = pl.multiple_of(j * CK, CK)
        for h in range(n_heads):
            state[slot * n_heads + h][...] = lax.dot_general(
                k_ref[pl.ds(start, CK), _head(h)], q_ref[:, _head(h)], _NT,
                preferred_element_type=F32)

    scores(0, 0)

    def pair(p, c):
        j = 2 * p
        scores(j + 1, 1)
        consume(j, 0, False)
        scores(j + 2, 0)
        consume(j + 1, 1, False)
        return c

    lax.fori_loop(0, i // 2, pair, 0)

    @pl.when(i % 2 == 0)
    def _():
        consume(i, 0, True)

    @pl.when(i % 2 == 1)
    def _():
        scores(i, 1)
        consume(i - 1, 0, False)
        consume(i, 1, True)


def _attn_update(state, n_heads, h, slot, vt_blk, t_bias, c):
    s_ref, state = state[slot * n_heads + h], state[2 * n_heads:]
    acc_ref, m_ref, l_ref = state[h], state[n_heads + h], state[2 * n_heads + h]
    t = s_ref[...] + t_bias
    m_old = m_ref[...]
    m_new = jnp.maximum(m_old, jnp.max(t, axis=0, keepdims=True) + c)
    alpha = jnp.exp2(m_old - m_new)
    p = jnp.exp2(t - (m_new - c))
    l_ref[...] = alpha * l_ref[...] + jnp.sum(p, axis=0, keepdims=True)
    acc_ref[...] = alpha * acc_ref[...] + jnp.dot(vt_blk, p.astype(BF16), preferred_element_type=F32)
    m_ref[...] = m_new


def _attn_finish(state, n_heads, h):
    state = state[2 * n_heads:]
    return (state[h][...] * (1.0 / state[2 * n_heads + h][...])).T


def _key_offsets():
    return lax.broadcasted_iota(I32, (CK, TQ), 0), lax.broadcasted_iota(I32, (CK, TQ), 1)


def _sortable(x):
    b = pltpu.bitcast(x, I32)
    return b ^ ((b >> 31) & 0x7FFFFFFF)


def _dsa_kernel(slopes_ref, qi_ref, wi_ref, kidx_ref, qa_ref, ka_ref, vat_ref, o_ref,
                keys_ref, half_ref, sb_ref, *state, top_k):
    i = pl.program_id(0)
    rows, cols = _key_offsets()
    causal = rows <= cols
    nh = N_HEADS_A

    @pl.when(i == 0)
    def _():
        rows_f = rows.astype(F32)
        for h in range(nh):
            sb_ref[h] = slopes_ref[h] * rows_f

    def score_chunk(j, diagonal):
        start = pl.multiple_of(j * CK, CK)
        kc = kidx_ref[pl.ds(start, CK), :]
        acc = jnp.zeros((CK, TQ), F32)
        for h in range(N_IDX_HEADS):
            z = lax.dot_general(kc, qi_ref[0, h * TQ:(h + 1) * TQ, :], _NT,
                                preferred_element_type=F32)
            acc = acc + wi_ref[h:h + 1, :] * jnp.maximum(z, 0.0)
        if diagonal:
            acc = jnp.where(causal, acc, NEG)
        key = _sortable(acc)
        keys_ref[pl.ds(start, CK), :] = key
        half_ref[pl.ds(start, CK), :] = (key >> 16).astype(I16)

    def score_body(j, c):
        score_chunk(j, False)
        return c

    lax.fori_loop(0, i, score_body, 0)
    score_chunk(i, True)
    half_ref[pl.ds(pl.multiple_of((i + 1) * CK, CK), CK), :] = jnp.full((CK, TQ), I16_MIN, I16)

    def count16(cand, strict):
        cand = cand.astype(I16)

        def body(p, cnt):
            blk = half_ref[pl.ds(pl.multiple_of(p * (2 * CK), 2 * CK), 2 * CK), :]
            hit = jnp.where((blk > cand) if strict else (blk >= cand), jnp.int16(1), jnp.int16(0))
            for r in range(0, 2 * CK, PACKED_ROWS):
                cnt = cnt + hit[r:r + PACKED_ROWS, :]
            return cnt

        cnt = lax.fori_loop(0, (i + 2) // 2, body, jnp.zeros((PACKED_ROWS, TQ), I16))
        return jnp.sum(cnt.astype(I32), axis=0, keepdims=True)

    def search16(target):
        def bit_body(b, thr):
            cand = thr + jnp.left_shift(jnp.int32(1), 15 - b)
            return jnp.where(count16(cand, False) >= target, cand, thr)
        return lax.fori_loop(0, 16, bit_body, jnp.full((1, TQ), I16_MIN, I32))

    t_hi = search16(top_k)
    need = top_k - count16(t_hi, True)

    def low_half(j, c):
        start = pl.multiple_of(j * CK, CK)
        key = keys_ref[pl.ds(start, CK), :]
        low = ((key ^ 0x8000) << 16) >> 16
        half_ref[pl.ds(start, CK), :] = jnp.where((key >> 16) == t_hi, low, I16_MIN).astype(I16)
        return c

    lax.fori_loop(0, i + 1, low_half, 0)
    t_lo = search16(need)
    thr = (t_hi << 16) + (t_lo - I16_MIN)

    _attn_reset(state, nh)

    def consume(j, slot, diagonal):
        start = pl.multiple_of(j * CK, CK)
        sel = keys_ref[pl.ds(start, CK), :] >= thr
        if diagonal:
            sel = jnp.logical_and(sel, causal)
        mask_bias = jnp.where(sel, 0.0, NEG)
        off = ((j - i) * CK).astype(F32)
        for h in range(nh):
            _attn_update(state, nh, h, slot, vat_ref[_head(h), pl.ds(start, CK)],
                         sb_ref[h] + mask_bias, slopes_ref[h] * off)

    _attn_run(state, nh, i, qa_ref, ka_ref, consume)
    for h in range(nh):
        o_ref[:, _head(h)] = _attn_finish(state, nh, h).astype(o_ref.dtype)


def _dsa(slopes, qi_r, wi_t, kidx, qa, ka, vat, top_k):
    t, w = qa.shape
    nh = w // HEAD_DIM
    return pl.pallas_call(
        functools.partial(_dsa_kernel, top_k=top_k),
        grid_spec=pltpu.PrefetchScalarGridSpec(
            num_scalar_prefetch=0,
            grid=(t // TQ,),
            in_specs=[
                pl.BlockSpec(memory_space=pltpu.SMEM),
                pl.BlockSpec((1, N_IDX_HEADS * TQ, IDX_DIM), lambda i: (i, 0, 0)),
                pl.BlockSpec((N_IDX_HEADS, TQ), lambda i: (0, i)),
                _resident(kidx.shape),
                pl.BlockSpec((TQ, w), lambda i: (i, 0)),
                _resident(ka.shape),
                _resident(vat.shape),
            ],
            out_specs=pl.BlockSpec((TQ, w), lambda i: (i, 0)),
            scratch_shapes=[pltpu.VMEM((t, TQ), I32), pltpu.VMEM((t + CK, TQ), I16),
                            pltpu.VMEM((nh, CK, TQ), F32)] + _attn_state(nh),
        ),
        out_shape=jax.ShapeDtypeStruct((t, w), BF16),
        compiler_params=_cparams("arbitrary"),
        name="dsa",
    )(slopes, qi_r, wi_t, kidx, qa, ka, vat)


def _moba_kernel(slopes_ref, qb_ref, kb_ref, vbt_ref, o_ref, kmean_ref, sel_ref, sb_ref, *state, n_kb):
    i = pl.program_id(0)
    rows, cols = _key_offsets()
    nh = N_HEADS_B

    @pl.when(i == 0)
    def _():
        rows_f = rows.astype(F32)
        for h in range(nh):
            sb_ref[h] = slopes_ref[h] * rows_f
            for n in range(n_kb):
                blk = kb_ref[n * CK:(n + 1) * CK, _head(h)].astype(F32)
                kmean_ref[h, n:n + 1, :] = jnp.mean(blk, axis=0, keepdims=True)

    blk_id = lax.broadcasted_iota(I32, (n_kb, TQ), 0)
    for h in range(nh):
        q_h = qb_ref[:, _head(h)]
        km = kmean_ref[h]
        km_hi = km.astype(BF16)
        km_lo = (km - km_hi.astype(F32)).astype(BF16)
        gate = (lax.dot_general(km_hi, q_h, _NT, preferred_element_type=F32)
                + lax.dot_general(km_lo, q_h, _NT, preferred_element_type=F32))
        gate = jnp.where(blk_id < i, gate, NEG)
        sel = jnp.full((n_kb, TQ), NEG, F32)
        for _ in range(MOBA_TOPK):
            best = jnp.max(gate, axis=0, keepdims=True)
            first = jnp.min(jnp.where(gate == best, blk_id, n_kb), axis=0, keepdims=True)
            pick = blk_id == first
            sel = jnp.where(pick, 0.0, sel)
            gate = jnp.where(pick, -jnp.inf, gate)
        sel_ref[h] = jnp.where(blk_id < i, sel, NEG)

    _attn_reset(state, nh)

    def consume(j, slot, diagonal):
        start = pl.multiple_of(j * CK, CK)
        off = ((j - i) * CK).astype(F32)
        for h in range(nh):
            if diagonal:
                t_bias = jnp.where(rows <= cols, sb_ref[h], NEG)
            else:
                t_bias = sb_ref[h] + sel_ref[h, pl.ds(j, 1), :]
            _attn_update(state, nh, h, slot, vbt_ref[_head(h), pl.ds(start, CK)],
                         t_bias, slopes_ref[h] * off)

    _attn_run(state, nh, i, qb_ref, kb_ref, consume)
    for h in range(nh):
        o_ref[:, _head(h)] = _attn_finish(state, nh, h).astype(o_ref.dtype)


def _moba(slopes, qb, kb, vbt):
    t, w = qb.shape
    nh = w // HEAD_DIM
    assert t % MOBA_BLOCK == 0 and TQ == MOBA_BLOCK and CK == MOBA_BLOCK
    n_kb = t // MOBA_BLOCK
    return pl.pallas_call(
        functools.partial(_moba_kernel, n_kb=n_kb),
        grid_spec=pltpu.PrefetchScalarGridSpec(
            num_scalar_prefetch=0,
            grid=(t // TQ,),
            in_specs=[
                pl.BlockSpec(memory_space=pltpu.SMEM),
                pl.BlockSpec((TQ, w), lambda i: (i, 0)),
                _resident(kb.shape),
                _resident(vbt.shape),
            ],
            out_specs=pl.BlockSpec((TQ, w), lambda i: (i, 0)),
            scratch_shapes=[pltpu.VMEM((nh, n_kb, HEAD_DIM), F32), pltpu.VMEM((nh, n_kb, TQ), F32),
                            pltpu.VMEM((nh, CK, TQ), F32)] + _attn_state(nh),
        ),
        out_shape=jax.ShapeDtypeStruct((t, w), BF16),
        compiler_params=_cparams("arbitrary"),
        name="moba",
    )(slopes, qb, kb, vbt)


def _mem_kernel(qc_ref, mk_ref, mvt_ref, o_ref):
    for h in range(N_HEADS_C):
        s = lax.dot_general(mk_ref[:, _head(h)], qc_ref[:, _head(h)], _NT,
                            preferred_element_type=F32)
        m = jnp.max(s, axis=0, keepdims=True)
        p = jnp.exp2(s - m)
        l = jnp.sum(p, axis=0, keepdims=True)
        acc = jnp.dot(mvt_ref[_head(h), :], p.astype(BF16), preferred_element_type=F32)
        o_ref[:, _head(h)] = (acc * (1.0 / l)).T.astype(o_ref.dtype)


def _mem_attn(qc, mk, mvt):
    t, w = qc.shape
    return pl.pallas_call(
        _mem_kernel,
        grid=(t // TQ,),
        in_specs=[pl.BlockSpec((TQ, w), lambda i: (i, 0)),
                  _resident(mk.shape), _resident(mvt.shape)],
        out_specs=pl.BlockSpec((TQ, w), lambda i: (i, 0)),
        out_shape=jax.ShapeDtypeStruct((t, w), BF16),
        compiler_params=_cparams("parallel"),
        name="mem_attn",
    )(qc, mk, mvt)


def _sigmoid(x):
    return 1.0 / (1.0 + jnp.exp(-x))


def _merge_kernel(x_ref, wga_ref, wgb_ref, wgc_ref, bga_ref, bgb_ref, bgc_ref,
                  oa_ref, ob_ref, oc_ref, wua_ref, wub_ref, wuc_ref, o_ref):
    xb = x_ref[...].astype(BF16)

    def branch(wg_ref, bg_ref, oo_ref, wu_ref):
        g = _sigmoid(jnp.dot(xb, wg_ref[...], preferred_element_type=F32) + bg_ref[...])
        return g * jnp.dot(oo_ref[...], wu_ref[...], preferred_element_type=F32)

    acc = branch(wga_ref, bga_ref, oa_ref, wua_ref)
    acc = acc + branch(wgb_ref, bgb_ref, ob_ref, wub_ref)
    acc = acc + branch(wgc_ref, bgc_ref, oc_ref, wuc_ref)
    o_ref[...] = acc.astype(o_ref.dtype)


def _merge(x, w_gate, b_gate, o_a, o_b, o_c, w_up_a, w_up_b, w_up_c, *, tm, tn):
    t, d = x.shape
    nb = d // tn
    b_gate = b_gate.reshape(1, 3 * d)

    def wg(k):
        return pl.BlockSpec((d, tn), lambda i, j, k=k: (0, j + k * nb))

    def bg(k):
        return pl.BlockSpec((1, tn), lambda i, j, k=k: (0, j + k * nb))

    def act(w):
        return pl.BlockSpec((tm, w), lambda i, j: (i, 0))

    def wu(w):
        return pl.BlockSpec((w, tn), lambda i, j: (0, j))

    return pl.pallas_call(
        _merge_kernel,
        grid=(t // tm, nb),
        in_specs=[pl.BlockSpec((tm, d), lambda i, j: (i, 0)),
                  wg(0), wg(1), wg(2), bg(0), bg(1), bg(2),
                  act(W_A), act(W_B), act(W_C), wu(W_A), wu(W_B), wu(W_C)],
        out_specs=pl.BlockSpec((tm, tn), lambda i, j: (i, j)),
        out_shape=jax.ShapeDtypeStruct((t, d), BF16),
        compiler_params=_cparams("parallel", "parallel"),
        name="merge",
    )(x, w_gate, w_gate, w_gate, b_gate, b_gate, b_gate, o_a, o_b, o_c, w_up_a, w_up_b, w_up_c)


def _layer_norm(y, g, b):
    mu = jnp.mean(y, axis=-1, keepdims=True)
    yc = y - mu
    var = jnp.mean(yc * yc, axis=-1, keepdims=True)
    return yc * lax.rsqrt(var + NORM_EPS) * g + b


def _route(logits):
    lane = lax.broadcasted_iota(I32, logits.shape, 1)
    is_g = lane < N_GROUPS
    gl = jnp.where(is_g, logits, -jnp.inf)
    gmax = jnp.max(gl, axis=-1, keepdims=True)
    g_sel = jnp.min(jnp.where(gl == gmax, lane, ROUTE_LANES), axis=-1, keepdims=True)
    p_g = 1.0 / jnp.sum(jnp.where(is_g, jnp.exp(gl - gmax), 0.0), axis=-1, keepdims=True)
    e_id = lane - N_GROUPS
    in_grp = jnp.logical_and(e_id >= g_sel * EXPERTS_PER_GROUP, e_id < (g_sel + 1) * EXPERTS_PER_GROUP)
    el = jnp.where(in_grp, logits, -jnp.inf)
    emax = jnp.max(el, axis=-1, keepdims=True)
    ex = jnp.where(in_grp, jnp.exp(el - emax), 0.0)
    pe = ex / jnp.sum(ex, axis=-1, keepdims=True)
    pe = jnp.where(in_grp, pe, -1.0)
    p1 = jnp.max(pe, axis=-1, keepdims=True)
    i1 = jnp.min(jnp.where(pe == p1, e_id, ROUTE_LANES), axis=-1, keepdims=True)
    pe2 = jnp.where(e_id == i1, -1.0, pe)
    p2 = jnp.max(pe2, axis=-1, keepdims=True)
    i2 = jnp.min(jnp.where(pe2 == p2, e_id, ROUTE_LANES), axis=-1, keepdims=True)
    denom = p1 + p2
    w1 = p_g * (p1 / denom)
    w2 = p_g * (p2 / denom)
    return jnp.where(lane == 0, w1,
                     jnp.where(lane == 1, w2,
                               jnp.where(lane == 2, i1.astype(F32),
                                         jnp.where(lane == 3, i2.astype(F32), 0.0))))


def _oproj_kernel(x_ref, m_ref, wo_ref, g_ref, b_ref, wr_ref, br_ref, x1_ref, r_ref, *, alpha):
    y = alpha * x_ref[...] + jnp.dot(m_ref[...], wo_ref[...], preferred_element_type=F32)
    x1 = _layer_norm(y, g_ref[...], b_ref[...])
    x1_ref[...] = x1
    x_hi = x1.astype(BF16)
    x_lo = (x1 - x_hi.astype(F32)).astype(BF16)
    parts = (jnp.dot(x_hi, wr_ref[...], preferred_element_type=F32)
             + jnp.dot(x_lo, wr_ref[...], preferred_element_type=F32))
    logits = parts + pltpu.roll(parts, ROUTE_LANES // 2, axis=1) + br_ref[...]
    r_ref[...] = _route(logits)


def _oproj(x, merged, w_o, ln_g, ln_b, w_r, b_r, *, alpha, tm):
    t, d = x.shape
    return pl.pallas_call(
        functools.partial(_oproj_kernel, alpha=alpha),
        grid=(t // tm,),
        in_specs=[pl.BlockSpec((tm, d), lambda i: (i, 0)),
                  pl.BlockSpec((tm, d), lambda i: (i, 0)),
                  _resident(w_o.shape),
                  pl.BlockSpec((1, d), lambda i: (0, 0)),
                  pl.BlockSpec((1, d), lambda i: (0, 0)),
                  _resident(w_r.shape),
                  pl.BlockSpec((1, ROUTE_LANES), lambda i: (0, 0))],
        out_specs=[pl.BlockSpec((tm, d), lambda i: (i, 0)),
                   pl.BlockSpec((tm, ROUTE_LANES), lambda i: (i, 0))],
        out_shape=[jax.ShapeDtypeStruct((t, d), F32),
                   jax.ShapeDtypeStruct((t, ROUTE_LANES), F32)],
        compiler_params=_cparams("parallel"),
        name="oproj",
    )(x, merged, w_o, ln_g.reshape(1, d), ln_b.reshape(1, d), w_r, b_r)


def _row_copy(src_hbm, row, dst, k, sem):
    return pltpu.make_async_copy(src_hbm.at[pl.ds(row, 1), :], dst.at[pl.ds(k, 1), :], sem)


def _expert_kernel(tok_ref, te_ref, first_ref, nxt_ref, ws_ref, nu_ref, x_hbm, w1_hbm, w3_hbm, w2_hbm,
                   o_ref, buf, w1b, w3b, w2b, sem, wsem, *, layer):
    i = pl.program_id(0)
    n_used = nu_ref[0]
    slot = lax.rem(i, 2)

    def gather(tile, s):
        base = tile * MOE_TM

        def body(r, c):
            _row_copy(x_hbm, tok_ref[base + r], buf.at[s], r, sem.at[s]).start()
            return c

        lax.fori_loop(0, MOE_TM, body, 0, unroll=ROW_DMA_UNROLL)

    def weight_copies(e, s):
        return [pltpu.make_async_copy(src.at[layer, e], dst.at[s], wsem.at[s])
                for src, dst in ((w1_hbm, w1b), (w3_hbm, w3b), (w2_hbm, w2b))]

    @pl.when(i == 0)
    def _():
        gather(0, 0)
        for cp in weight_copies(te_ref[0], 0):
            cp.start()

    @pl.when(i < n_used)
    def _():
        ws = ws_ref[i]

        @pl.when(first_ref[i] == 1)
        def _():
            for cp in weight_copies(te_ref[i], ws):
                cp.wait()

            @pl.when(nxt_ref[i] >= 0)
            def _():
                for cp in weight_copies(nxt_ref[i], 1 - ws):
                    cp.start()

        pltpu.make_async_copy(x_hbm.at[pl.ds(0, MOE_TM), :], buf.at[slot], sem.at[slot]).wait()

        @pl.when(i + 1 < n_used)
        def _():
            gather(i + 1, 1 - slot)

        xs = buf[slot].astype(BF16)
        h1 = jnp.dot(xs, w1b[ws].astype(BF16), preferred_element_type=F32)
        h3 = jnp.dot(xs, w3b[ws].astype(BF16), preferred_element_type=F32)
        hid = (h1 * _sigmoid(h1)) * h3
        o_ref[...] = jnp.dot(hid.astype(BF16), w2b[ws].astype(BF16), preferred_element_type=F32)

    @pl.when(i >= n_used)
    def _():
        o_ref[...] = jnp.zeros_like(o_ref)


def _experts(plan, x1, w1, w3, w2, layer):
    r = plan[0].shape[0]
    d = x1.shape[1]
    de = w1.shape[-1]
    n_pref = len(plan)
    return pl.pallas_call(
        functools.partial(_expert_kernel, layer=layer),
        grid_spec=pltpu.PrefetchScalarGridSpec(
            num_scalar_prefetch=n_pref,
            grid=(r // MOE_TM,),
            in_specs=[pl.BlockSpec(memory_space=pl.ANY)] * 4,
            out_specs=pl.BlockSpec((MOE_TM, d), lambda i, *_: (i, 0)),
            scratch_shapes=[pltpu.VMEM((2, MOE_TM, d), F32),
                            pltpu.VMEM((2, d, de), F32), pltpu.VMEM((2, d, de), F32),
                            pltpu.VMEM((2, de, d), F32),
                            pltpu.SemaphoreType.DMA((2,)), pltpu.SemaphoreType.DMA((2,))],
        ),
        out_shape=jax.ShapeDtypeStruct((r, d), F32),
        compiler_params=_cparams("arbitrary"),
        name="moe_experts",
    )(*plan, x1, w1, w3, w2)


def _combine_kernel(pos_ref, x1_ref, route_ref, g_ref, b_ref, ys_hbm, o_ref, buf, sem, *, alpha):
    i = pl.program_id(0)
    slot = lax.rem(i, 2)

    def gather(tile, s):
        base = tile * MOE_TM

        def body(r, c):
            for k in range(2):
                _row_copy(ys_hbm, pos_ref[2 * (base + r) + k], buf.at[s, k], r, sem.at[s]).start()
            return c

        lax.fori_loop(0, MOE_TM, body, 0, unroll=ROW_DMA_UNROLL // 2)

    @pl.when(i == 0)
    def _():
        gather(0, 0)

    for k in range(2):
        pltpu.make_async_copy(ys_hbm.at[pl.ds(0, MOE_TM), :], buf.at[slot, k], sem.at[slot]).wait()

    @pl.when(i + 1 < pl.num_programs(0))
    def _():
        gather(i + 1, 1 - slot)

    route = route_ref[...]
    y = alpha * x1_ref[...] + (route[:, 0:1] * buf[slot, 0] + route[:, 1:2] * buf[slot, 1])
    o_ref[...] = _layer_norm(y, g_ref[...], b_ref[...])


def _combine(pos, x1, route, ln_g, ln_b, ys, *, alpha):
    t, d = x1.shape
    return pl.pallas_call(
        functools.partial(_combine_kernel, alpha=alpha),
        grid_spec=pltpu.PrefetchScalarGridSpec(
            num_scalar_prefetch=1,
            grid=(t // MOE_TM,),
            in_specs=[pl.BlockSpec((MOE_TM, d), lambda i, pos: (i, 0)),
                      pl.BlockSpec((MOE_TM, ROUTE_LANES), lambda i, pos: (i, 0)),
                      pl.BlockSpec((1, d), lambda i, pos: (0, 0)),
                      pl.BlockSpec((1, d), lambda i, pos: (0, 0)),
                      pl.BlockSpec(memory_space=pl.ANY)],
            out_specs=pl.BlockSpec((MOE_TM, d), lambda i, pos: (i, 0)),
            scratch_shapes=[pltpu.VMEM((2, 2, MOE_TM, d), F32), pltpu.SemaphoreType.DMA((2,))],
        ),
        out_shape=jax.ShapeDtypeStruct((t, d), F32),
        compiler_params=_cparams("arbitrary"),
        name="moe_combine",
    )(pos, x1, route, ln_g.reshape(1, d), ln_b.reshape(1, d), ys)


def _moe_plan(route, t):
    flat_e = route[:, 2:4].astype(I32).reshape(-1)
    n = flat_e.shape[0]
    n_rows = n + N_EXPERTS * MOE_TM
    n_tiles = n_rows // MOE_TM
    onehot = (flat_e[:, None] == jnp.arange(N_EXPERTS, dtype=I32)[None, :]).astype(I32)
    running = jnp.cumsum(onehot, axis=0)
    rank = jnp.sum(onehot * running, axis=1) - 1
    counts = running[-1]
    tiles_per = (counts + MOE_TM - 1) // MOE_TM
    tile_end = jnp.cumsum(tiles_per)
    grp_row0 = (tile_end - tiles_per) * MOE_TM
    pos = jnp.sum(onehot * grp_row0[None, :], axis=1) + rank
    row_token = jnp.zeros((n_rows,), I32).at[pos].set(jnp.arange(n, dtype=I32) // 2, unique_indices=True)
    n_used = tile_end[-1]
    tile_ids = jnp.minimum(jnp.arange(n_tiles, dtype=I32), n_used - 1)
    tile_expert = jnp.sum((tile_end[None, :] <= tile_ids[:, None]).astype(I32), axis=1)
    tile_expert = jnp.minimum(tile_expert, N_EXPERTS - 1)
    first = jnp.concatenate([jnp.ones((1,), I32), (tile_expert[1:] != tile_expert[:-1]).astype(I32)])
    w_slot = (jnp.cumsum(first) - 1) % 2
    e_ids = jnp.arange(N_EXPERTS, dtype=I32)
    later = jnp.logical_and(e_ids[None, :] > e_ids[:, None], tiles_per[None, :] > 0)
    next_e = jnp.min(jnp.where(later, e_ids[None, :], N_EXPERTS), axis=1)
    next_e = jnp.where(next_e == N_EXPERTS, -1, next_e)
    nxt = jnp.sum((tile_expert[:, None] == e_ids[None, :]).astype(I32) * next_e[None, :], axis=1)
    plan = (row_token, tile_expert, first, nxt.astype(I32), w_slot.astype(I32), n_used.reshape(1).astype(I32))
    return plan, pos


def kernel(x, mem, w_mem_kv, w_in, g_cq, g_ckv, g_kidx, b_kidx, w_uq, w_uqi, w_ukv,
           w_up_a, w_up_b, w_up_c, w_gate, b_gate, w_o, ln1_g, ln1_b,
           w_grp, b_grp, w_rt, b_rt, w1, w3, w2, ln2_g, ln2_b):
    bsz, t, d = x.shape
    assert bsz == 1 and t % TQ == 0
    depth = w_in.shape[0]
    alpha = (2 * depth) ** 0.25
    scale = HEAD_DIM ** -0.5 * LOG2E
    top_k = min(TOPK_A_MAX, t // 4)
    n_mem = mem.shape[1]
    tm = min(t, 512)

    slopes = LOG2E * 2.0 ** (-8.0 * jnp.arange(1, N_ALIBI + 1, dtype=F32) / N_ALIBI)
    slopes_a, slopes_b = slopes[0::2], slopes[1::2]

    mkv = _mm(mem[0], w_mem_kv.astype(BF16), out_dtype=BF16, tm=n_mem, tn=W_C)
    mk = mkv[:, :W_C]
    mvt = mkv[:, W_C:].T

    xl = x[0]
    for l in range(depth):
        wl = w_in[l]
        o = 0
        parts = []
        for width in (Q_RANK_A, KV_RANK_A, IDX_DIM, N_IDX_HEADS, W_B, W_B, W_B, W_C):
            parts.append(wl[:, o:o + width])
            o += width
        w_cq, w_ckv, w_ki, w_wi, w_qb, w_kb, w_vb, w_qc = parts
        n_small = Q_RANK_A + KV_RANK_A + IDX_DIM + N_IDX_HEADS
        pad = (-n_small) % LANES
        w_p1 = jnp.concatenate([w_cq, w_ckv, w_ki, w_wi * N_IDX_HEADS ** -0.5,
                                jnp.zeros((d, pad), F32)], axis=1).astype(BF16)
        w_p2 = jnp.concatenate([w_qb * scale, w_kb, w_vb, w_qc * scale], axis=1).astype(BF16)
        p1 = _mm(xl, w_p1, out_dtype=F32, tm=tm, tn=w_p1.shape[1])
        p2 = _mm(xl, w_p2, out_dtype=BF16, tm=tm, tn=w_p2.shape[1] // 2)
        c_q = p1[:, :Q_RANK_A]
        c_kv = p1[:, Q_RANK_A:Q_RANK_A + KV_RANK_A]
        k_i = p1[:, Q_RANK_A + KV_RANK_A:Q_RANK_A + KV_RANK_A + IDX_DIM]
        w_i = p1[:, Q_RANK_A + KV_RANK_A + IDX_DIM:n_small]
        q_b, k_b = p2[:, :W_B], p2[:, W_B:2 * W_B]
        v_b, q_c = p2[:, 2 * W_B:3 * W_B], p2[:, 3 * W_B:]

        w_q = jnp.concatenate([w_uq[l] * scale, w_uqi[l] * IDX_DIM ** -0.5], axis=1).astype(BF16)
        qq = _mm(c_q, w_q, out_dtype=BF16, tm=tm, tn=w_q.shape[1], norm_g=g_cq[l])
        kv = _mm(c_kv, w_ukv[l].astype(BF16), out_dtype=BF16, tm=tm, tn=2 * W_A, norm_g=g_ckv[l])
        q_a, q_i = qq[:, :W_A], qq[:, W_A:]
        k_a, v_a = kv[:, :W_A], kv[:, W_A:]
        kidx = _kidx_norm(k_i, g_kidx[l], b_kidx[l])
        qi_r = (q_i.reshape(t // TQ, TQ, N_IDX_HEADS, IDX_DIM).transpose(0, 2, 1, 3)
                .reshape(t // TQ, N_IDX_HEADS * TQ, IDX_DIM))
        o_a = _dsa(slopes_a, qi_r, w_i.T, kidx, q_a, k_a, v_a.T, top_k)
        o_b = _moba(slopes_b, q_b, k_b, v_b.T)
        o_c = _mem_attn(q_c, mk, mvt)

        merged = _merge(xl, w_gate[l].astype(BF16), b_gate[l], o_a, o_b, o_c,
                        w_up_a[l].astype(BF16), w_up_b[l].astype(BF16), w_up_c[l].astype(BF16),
                        tm=min(t, 1024), tn=512)

        n_route = N_GROUPS + N_EXPERTS
        w_r = jnp.concatenate([w_grp[l], w_rt[l]], axis=1)
        w_r_hi = w_r.astype(BF16)
        w_r_lo = (w_r - w_r_hi.astype(F32)).astype(BF16)
        lane_pad = jnp.zeros((d, ROUTE_LANES // 2 - n_route), BF16)
        w_r2 = jnp.concatenate([w_r_hi, lane_pad, w_r_lo, lane_pad], axis=1)
        b_r = jnp.concatenate([b_grp[l], b_rt[l],
                               jnp.zeros((ROUTE_LANES - n_route,), F32)]).reshape(1, ROUTE_LANES)
        x1, route = _oproj(xl, merged, w_o[l].astype(BF16), ln1_g[l], ln1_b[l], w_r2, b_r,
                           alpha=alpha, tm=tm)

        plan, pos = _moe_plan(route, t)
        ys = _experts(plan, x1, w1, w3, w2, l)
        xl = _combine(pos, x1, route, ln2_g[l], ln2_b[l], ys, alpha=alpha)
    return xl[None]
```

```python
import functools

import jax
import jax.numpy as jnp
from jax import lax
from jax.experimental import pallas as pl
from jax.experimental.pallas import tpu as pltpu

HEAD_DIM = 128
N_HEADS_A = 6
N_HEADS_B = 6
N_HEADS_C = 4
W_A = N_HEADS_A * HEAD_DIM
W_B = N_HEADS_B * HEAD_DIM
W_C = N_HEADS_C * HEAD_DIM
Q_RANK_A = 512
KV_RANK_A = 256
N_IDX_HEADS = 16
IDX_DIM = 64
TOPK_A_MAX = 256
MOBA_BLOCK = 256
MOBA_TOPK = 3
N_ALIBI = N_HEADS_A + N_HEADS_B
N_GROUPS = 4
EXPERTS_PER_GROUP = 8
N_EXPERTS = N_GROUPS * EXPERTS_PER_GROUP
D_EXPERT = 512
NORM_EPS = 1e-5
NEG = -1e30
LOG2E = 1.4426950408889634
I16_MIN = -(2 ** 15)

LANES = 128
SUBLANES = 8
PACKED_ROWS = 2 * SUBLANES
TQ = 256
CK = 256
ROUTE_LANES = 128
MOE_TM = 256
ROW_DMA_UNROLL = 8
VMEM_LIMIT = 56 * 1024 * 1024

F32 = jnp.float32
BF16 = jnp.bfloat16
I32 = jnp.int32
I16 = jnp.int16

_NT = (((1,), (1,)), ((), ()))


def _cparams(*sem):
    return pltpu.CompilerParams(dimension_semantics=sem, vmem_limit_bytes=VMEM_LIMIT)


def _resident(shape, col_block=0):
    index = (0,) * (len(shape) - 1) + (col_block,)
    return pl.BlockSpec(shape, lambda *_: index, pipeline_mode=pl.Buffered(1))


def _mm_kernel(*refs, has_norm, has_bias):
    x_ref, w_ref = refs[0], refs[1]
    k = 2
    x = x_ref[...]
    if has_norm:
        g_ref = refs[k]
        k += 1
        xf = x.astype(F32)
        x = xf * lax.rsqrt(jnp.mean(xf * xf, axis=-1, keepdims=True) + NORM_EPS) * g_ref[...]
    acc = jnp.dot(x.astype(BF16), w_ref[...], preferred_element_type=F32)
    if has_bias:
        acc = acc + refs[k][...]
        k += 1
    o_ref = refs[k]
    o_ref[...] = acc.astype(o_ref.dtype)


def _mm(x, w, *, out_dtype, tm, tn, norm_g=None, bias=None, x_col_block=0):
    m = x.shape[0]
    kdim, n = w.shape
    assert m % tm == 0 and n % tn == 0
    in_specs = [pl.BlockSpec((tm, kdim), lambda i, j: (i, x_col_block)),
                pl.BlockSpec((kdim, tn), lambda i, j: (0, j))]
    args = [x, w]
    if norm_g is not None:
        in_specs.append(pl.BlockSpec((1, kdim), lambda i, j: (0, 0)))
        args.append(norm_g.reshape(1, kdim).astype(F32))
    if bias is not None:
        in_specs.append(pl.BlockSpec((1, tn), lambda i, j: (0, j)))
        args.append(bias.reshape(1, n).astype(F32))
    return pl.pallas_call(
        functools.partial(_mm_kernel, has_norm=norm_g is not None, has_bias=bias is not None),
        grid=(m // tm, n // tn),
        in_specs=in_specs,
        out_specs=pl.BlockSpec((tm, tn), lambda i, j: (i, j)),
        out_shape=jax.ShapeDtypeStruct((m, n), out_dtype),
        compiler_params=_cparams("parallel", "parallel"),
        name="mm",
    )(*args)


def _kidx_kernel(x_ref, g_ref, b_ref, k_ref, w_ref):
    blk = x_ref[...]
    x = blk[:, :IDX_DIM]
    mu = jnp.mean(x, axis=-1, keepdims=True)
    xc = x - mu
    var = jnp.mean(xc * xc, axis=-1, keepdims=True)
    k_ref[...] = (xc * lax.rsqrt(var + NORM_EPS) * g_ref[...] + b_ref[...]).astype(k_ref.dtype)
    w_ref[...] = blk.T[IDX_DIM:IDX_DIM + N_IDX_HEADS, :]


def _kidx_norm(p1, col_block, g, b):
    t = p1.shape[0]
    tm = min(t, 1024)
    return pl.pallas_call(
        _kidx_kernel,
        grid=(t // tm,),
        in_specs=[pl.BlockSpec((tm, LANES), lambda i: (i, col_block)),
                  pl.BlockSpec((1, IDX_DIM), lambda i: (0, 0)),
                  pl.BlockSpec((1, IDX_DIM), lambda i: (0, 0))],
        out_specs=[pl.BlockSpec((tm, IDX_DIM), lambda i: (i, 0)),
                   pl.BlockSpec((N_IDX_HEADS, tm), lambda i: (0, i))],
        out_shape=[jax.ShapeDtypeStruct((t, IDX_DIM), BF16),
                   jax.ShapeDtypeStruct((N_IDX_HEADS, t), F32)],
        compiler_params=_cparams("parallel"),
        name="kidx_norm",
    )(p1, g.reshape(1, IDX_DIM), b.reshape(1, IDX_DIM))


def _head(h):
    return slice(h * HEAD_DIM, (h + 1) * HEAD_DIM)


def _attn_state(n_heads):
    return ([pltpu.VMEM((CK, TQ), F32) for _ in range(2 * n_heads)]
            + [pltpu.VMEM((HEAD_DIM, TQ), F32) for _ in range(n_heads)]
            + [pltpu.VMEM((1, TQ), F32) for _ in range(2 * n_heads)])


def _attn_reset(state, n_heads):
    state = state[2 * n_heads:]
    for h in range(n_heads):
        state[h][...] = jnp.zeros((HEAD_DIM, TQ), F32)
        state[n_heads + h][...] = jnp.full((1, TQ), NEG, F32)
        state[2 * n_heads + h][...] = jnp.zeros((1, TQ), F32)


def _attn_run(state, n_heads, i, q_ref, k_ref, consume):
    def scores(j, slot):
        start = pl.multiple_of(j * CK, CK)
        for h in range(n_heads):
            state[slot * n_heads + h][...] = lax.dot_general(
                k_ref[pl.ds(start, CK), _head(h)], q_ref[:, _head(h)], _NT,
                preferred_element_type=F32)

    scores(0, 0)

    def pair(p, c):
        j = 2 * p
        scores(j + 1, 1)
        consume(j, 0, False)
        scores(j + 2, 0)
        consume(j + 1, 1, False)
        return c

    lax.fori_loop(0, i // 2, pair, 0)

    @pl.when(i % 2 == 0)
    def _():
        consume(i, 0, True)

    @pl.when(i % 2 == 1)
    def _():
        scores(i, 1)
        consume(i - 1, 0, False)
        consume(i, 1, True)


def _attn_update(state, n_heads, h, slot, vt_blk, t_bias, c):
    s_ref, state = state[slot * n_heads + h], state[2 * n_heads:]
    acc_ref, m_ref, l_ref = state[h], state[n_heads + h], state[2 * n_heads + h]
    t = s_ref[...] + t_bias
    m_old = m_ref[...]
    m_new = jnp.maximum(m_old, jnp.max(t, axis=0, keepdims=True) + c)
    alpha = jnp.exp2(m_old - m_new)
    p = jnp.exp2(t - (m_new - c))
    l_ref[...] = alpha * l_ref[...] + jnp.sum(p, axis=0, keepdims=True)
    acc_ref[...] = alpha * acc_ref[...] + jnp.dot(vt_blk, p.astype(BF16), preferred_element_type=F32)
    m_ref[...] = m_new


def _attn_finish(state, n_heads, h):
    state = state[2 * n_heads:]
    return (state[h][...] * (1.0 / state[2 * n_heads + h][...])).T


def _key_offsets():
    return lax.broadcasted_iota(I32, (CK, TQ), 0), lax.broadcasted_iota(I32, (CK, TQ), 1)


def _sortable(x):
    b = pltpu.bitcast(x, I32)
    return b ^ ((b >> 31) & 0x7FFFFFFF)


def _dsa_kernel(slopes_ref, qi_ref, wi_ref, kidx_ref, qa_ref, ka_ref, vat_ref, o_ref,
                keys_ref, half_ref, sb_ref, *state, top_k, n_idx_bits):
    i = pl.program_id(0)
    rows, cols = _key_offsets()
    causal = rows <= cols
    nh = N_HEADS_A

    @pl.when(i == 0)
    def _():
        rows_f = rows.astype(F32)
        for h in range(nh):
            sb_ref[h] = slopes_ref[h] * rows_f

    def score_chunk(j, diagonal):
        start = pl.multiple_of(j * CK, CK)
        kc = kidx_ref[pl.ds(start, CK), :]
        acc = jnp.zeros((CK, TQ), F32)
        for h in range(N_IDX_HEADS):
            z = lax.dot_general(kc, qi_ref[0, h * TQ:(h + 1) * TQ, :], _NT,
                                preferred_element_type=F32)
            acc = acc + wi_ref[h:h + 1, :] * jnp.maximum(z, 0.0)
        if diagonal:
            acc = jnp.where(causal, acc, NEG)
        key = _sortable(acc)
        keys_ref[pl.ds(start, CK), :] = key
        half_ref[pl.ds(start, CK), :] = (key >> 16).astype(I16)

    def score_body(j, c):
        score_chunk(j, False)
        return c

    lax.fori_loop(0, i, score_body, 0)
    score_chunk(i, True)
    half_ref[pl.ds(pl.multiple_of((i + 1) * CK, CK), CK), :] = jnp.full((CK, TQ), I16_MIN, I16)

    def count16(cand, strict):
        cand = cand.astype(I16)

        def body(p, cnt):
            blk = half_ref[pl.ds(pl.multiple_of(p * (2 * CK), 2 * CK), 2 * CK), :]
            hit = jnp.where((blk > cand) if strict else (blk >= cand), jnp.int16(1), jnp.int16(0))
            for r in range(0, 2 * CK, PACKED_ROWS):
                cnt = cnt + hit[r:r + PACKED_ROWS, :]
            return cnt

        cnt = lax.fori_loop(0, (i + 2) // 2, body, jnp.zeros((PACKED_ROWS, TQ), I16))
        return jnp.sum(cnt.astype(I32), axis=0, keepdims=True)

    def search16(target, n_all):
        def bit_body(b, carry):
            thr, n_thr = carry
            cand = thr + jnp.left_shift(jnp.int32(1), 15 - b)
            n_cand = count16(cand, False)
            ok = n_cand >= target
            return jnp.where(ok, cand, thr), jnp.where(ok, n_cand, n_thr)
        return lax.fori_loop(0, 16, bit_body, (jnp.full((1, TQ), I16_MIN, I32), n_all))

    t_hi, n_ge_hi = search16(top_k, jnp.zeros((1, TQ), I32) + (i + 1) * CK)
    n_gt_hi = count16(t_hi, True)
    need = top_k - n_gt_hi

    def low_half(j, c):
        start = pl.multiple_of(j * CK, CK)
        key = keys_ref[pl.ds(start, CK), :]
        low = ((key ^ 0x8000) << 16) >> 16
        half_ref[pl.ds(start, CK), :] = jnp.where((key >> 16) == t_hi, low, I16_MIN).astype(I16)
        return c

    lax.fori_loop(0, i + 1, low_half, 0)
    t_lo, n_ge_lo = search16(need, n_ge_hi - n_gt_hi)
    thr = (t_hi << 16) + (t_lo - I16_MIN)

    @pl.when(jnp.max(n_gt_hi + n_ge_lo) > top_k)
    def _():
        def count32(pred):
            def body(j, cnt):
                start = pl.multiple_of(j * CK, CK)
                hit = jnp.where(pred(keys_ref[pl.ds(start, CK), :], rows + j * CK), 1, 0)
                return cnt + jnp.sum(hit.reshape(CK // SUBLANES, SUBLANES, TQ), axis=0)
            cnt = lax.fori_loop(0, i + 1, body, jnp.zeros((SUBLANES, TQ), I32))
            return jnp.sum(cnt, axis=0, keepdims=True)

        keep = top_k - count32(lambda key, idx: key > thr)

        def idx_bit(b, last):
            cand = last + jnp.left_shift(jnp.int32(1), n_idx_bits - 1 - b)
            below = count32(lambda key, idx: jnp.logical_and(key == thr, idx < cand))
            return jnp.where(below < keep, cand, last)

        last = lax.fori_loop(0, n_idx_bits, idx_bit, jnp.zeros((1, TQ), I32))

        def drop(j, c):
            start = pl.multiple_of(j * CK, CK)
            key = keys_ref[pl.ds(start, CK), :]
            late_tie = jnp.logical_and(key == thr, rows + j * CK > last)
            keys_ref[pl.ds(start, CK), :] = jnp.where(late_tie, key - 1, key)
            return c

        lax.fori_loop(0, i + 1, drop, 0)

    _attn_reset(state, nh)

    def consume(j, slot, diagonal):
        start = pl.multiple_of(j * CK, CK)
        sel = keys_ref[pl.ds(start, CK), :] >= thr
        if diagonal:
            sel = jnp.logical_and(sel, causal)
        mask_bias = jnp.where(sel, 0.0, NEG)
        off = ((j - i) * CK).astype(F32)
        for h in range(nh):
            _attn_update(state, nh, h, slot, vat_ref[_head(h), pl.ds(start, CK)],
                         sb_ref[h] + mask_bias, slopes_ref[h] * off)

    _attn_run(state, nh, i, qa_ref, ka_ref, consume)
    for h in range(nh):
        o_ref[:, _head(h)] = _attn_finish(state, nh, h).astype(o_ref.dtype)


def _dsa(slopes, qi_r, wi_t, kidx, qa, ka, vat, top_k):
    w, t = vat.shape
    nh = w // HEAD_DIM
    return pl.pallas_call(
        functools.partial(_dsa_kernel, top_k=top_k, n_idx_bits=max(1, (t - 1).bit_length())),
        grid_spec=pltpu.PrefetchScalarGridSpec(
            num_scalar_prefetch=0,
            grid=(t // TQ,),
            in_specs=[
                pl.BlockSpec(memory_space=pltpu.SMEM),
                pl.BlockSpec((1, N_IDX_HEADS * TQ, IDX_DIM), lambda i: (i, 0, 0)),
                pl.BlockSpec((N_IDX_HEADS, TQ), lambda i: (0, i)),
                _resident(kidx.shape),
                pl.BlockSpec((TQ, w), lambda i: (i, 0)),
                _resident((t, w)),
                _resident(vat.shape),
            ],
            out_specs=pl.BlockSpec((TQ, w), lambda i: (i, 0)),
            scratch_shapes=[pltpu.VMEM((t, TQ), I32), pltpu.VMEM((t + CK, TQ), I16),
                            pltpu.VMEM((nh, CK, TQ), F32)] + _attn_state(nh),
        ),
        out_shape=jax.ShapeDtypeStruct((t, w), BF16),
        compiler_params=_cparams("arbitrary"),
        name="dsa",
    )(slopes, qi_r, wi_t, kidx, qa, ka, vat)


def _moba_kernel(slopes_ref, qb_ref, kb_ref, vbt_ref, o_ref, kmean_ref, sel_ref, sb_ref, *state, n_kb):
    i = pl.program_id(0)
    rows, cols = _key_offsets()
    nh = N_HEADS_B

    @pl.when(i == 0)
    def _():
        rows_f = rows.astype(F32)
        for h in range(nh):
            sb_ref[h] = slopes_ref[h] * rows_f
            for n in range(n_kb):
                blk = kb_ref[n * CK:(n + 1) * CK, _head(h)].astype(F32)
                kmean_ref[h, n:n + 1, :] = jnp.mean(blk, axis=0, keepdims=True)

    blk_id = lax.broadcasted_iota(I32, (n_kb, TQ), 0)
    for h in range(nh):
        q_h = qb_ref[:, _head(h)]
        km = kmean_ref[h]
        km_hi = km.astype(BF16)
        km_lo = (km - km_hi.astype(F32)).astype(BF16)
        gate = (lax.dot_general(km_hi, q_h, _NT, preferred_element_type=F32)
                + lax.dot_general(km_lo, q_h, _NT, preferred_element_type=F32))
        gate = jnp.where(blk_id < i, gate, NEG)
        sel = jnp.full((n_kb, TQ), NEG, F32)
        for _ in range(MOBA_TOPK):
            best = jnp.max(gate, axis=0, keepdims=True)
            first = jnp.min(jnp.where(gate == best, blk_id, n_kb), axis=0, keepdims=True)
            pick = blk_id == first
            sel = jnp.where(pick, 0.0, sel)
            gate = jnp.where(pick, -jnp.inf, gate)
        sel_ref[h] = jnp.where(blk_id < i, sel, NEG)

    _attn_reset(state, nh)

    def consume(j, slot, diagonal):
        start = pl.multiple_of(j * CK, CK)
        off = ((j - i) * CK).astype(F32)
        for h in range(nh):
            if diagonal:
                t_bias = jnp.where(rows <= cols, sb_ref[h], NEG)
            else:
                t_bias = sb_ref[h] + sel_ref[h, pl.ds(j, 1), :]
            _attn_update(state, nh, h, slot, vbt_ref[_head(h), pl.ds(start, CK)],
                         t_bias, slopes_ref[h] * off)

    _attn_run(state, nh, i, qb_ref, kb_ref, consume)
    for h in range(nh):
        o_ref[:, _head(h)] = _attn_finish(state, nh, h).astype(o_ref.dtype)


def _moba(slopes, qkv, vbt):
    w, t = vbt.shape
    nh = w // HEAD_DIM
    assert t % MOBA_BLOCK == 0 and TQ == MOBA_BLOCK and CK == MOBA_BLOCK
    n_kb = t // MOBA_BLOCK
    return pl.pallas_call(
        functools.partial(_moba_kernel, n_kb=n_kb),
        grid_spec=pltpu.PrefetchScalarGridSpec(
            num_scalar_prefetch=0,
            grid=(t // TQ,),
            in_specs=[
                pl.BlockSpec(memory_space=pltpu.SMEM),
                pl.BlockSpec((TQ, w), lambda i: (i, 0)),
                _resident((t, w), col_block=1),
                _resident(vbt.shape),
            ],
            out_specs=pl.BlockSpec((TQ, w), lambda i: (i, 0)),
            scratch_shapes=[pltpu.VMEM((nh, n_kb, HEAD_DIM), F32), pltpu.VMEM((nh, n_kb, TQ), F32),
                            pltpu.VMEM((nh, CK, TQ), F32)] + _attn_state(nh),
        ),
        out_shape=jax.ShapeDtypeStruct((t, w), BF16),
        compiler_params=_cparams("arbitrary"),
        name="moba",
    )(slopes, qkv, qkv, vbt)


def _mem_kernel(qc_ref, mk_ref, mvt_ref, o_ref):
    for h in range(N_HEADS_C):
        s = lax.dot_general(mk_ref[:, _head(h)], qc_ref[:, _head(h)], _NT,
                            preferred_element_type=F32)
        m = jnp.max(s, axis=0, keepdims=True)
        p = jnp.exp2(s - m)
        l = jnp.sum(p, axis=0, keepdims=True)
        acc = jnp.dot(mvt_ref[_head(h), :], p.astype(BF16), preferred_element_type=F32)
        o_ref[:, _head(h)] = (acc * (1.0 / l)).T.astype(o_ref.dtype)


def _mem_attn(qc, mk, mvt):
    t, w = qc.shape
    return pl.pallas_call(
        _mem_kernel,
        grid=(t // TQ,),
        in_specs=[pl.BlockSpec((TQ, w), lambda i: (i, 0)),
                  _resident(mk.shape), _resident(mvt.shape)],
        out_specs=pl.BlockSpec((TQ, w), lambda i: (i, 0)),
        out_shape=jax.ShapeDtypeStruct((t, w), BF16),
        compiler_params=_cparams("parallel"),
        name="mem_attn",
    )(qc, mk, mvt)


def _sigmoid(x):
    return 1.0 / (1.0 + jnp.exp(-x))


def _merge_kernel(x_ref, wga_ref, wgb_ref, wgc_ref, bga_ref, bgb_ref, bgc_ref,
                  oa_ref, ob_ref, oc_ref, wua_ref, wub_ref, wuc_ref, o_ref):
    xb = x_ref[...].astype(BF16)

    def branch(wg_ref, bg_ref, oo_ref, wu_ref):
        g = _sigmoid(jnp.dot(xb, wg_ref[...], preferred_element_type=F32) + bg_ref[...])
        return g * jnp.dot(oo_ref[...], wu_ref[...], preferred_element_type=F32)

    acc = branch(wga_ref, bga_ref, oa_ref, wua_ref)
    acc = acc + branch(wgb_ref, bgb_ref, ob_ref, wub_ref)
    acc = acc + branch(wgc_ref, bgc_ref, oc_ref, wuc_ref)
    o_ref[...] = acc.astype(o_ref.dtype)


def _merge(x, w_gate, b_gate, o_a, o_b, o_c, w_up_a, w_up_b, w_up_c, *, tm, tn):
    t, d = x.shape
    nb = d // tn
    b_gate = b_gate.reshape(1, 3 * d)

    def wg(k):
        return pl.BlockSpec((d, tn), lambda i, j, k=k: (0, j + k * nb))

    def bg(k):
        return pl.BlockSpec((1, tn), lambda i, j, k=k: (0, j + k * nb))

    def act(w):
        return pl.BlockSpec((tm, w), lambda i, j: (i, 0))

    def wu(w):
        return pl.BlockSpec((w, tn), lambda i, j: (0, j))

    return pl.pallas_call(
        _merge_kernel,
        grid=(t // tm, nb),
        in_specs=[pl.BlockSpec((tm, d), lambda i, j: (i, 0)),
                  wg(0), wg(1), wg(2), bg(0), bg(1), bg(2),
                  act(W_A), act(W_B), act(W_C), wu(W_A), wu(W_B), wu(W_C)],
        out_specs=pl.BlockSpec((tm, tn), lambda i, j: (i, j)),
        out_shape=jax.ShapeDtypeStruct((t, d), BF16),
        compiler_params=_cparams("parallel", "parallel"),
        name="merge",
    )(x, w_gate, w_gate, w_gate, b_gate, b_gate, b_gate, o_a, o_b, o_c, w_up_a, w_up_b, w_up_c)


def _layer_norm(y, g, b):
    mu = jnp.mean(y, axis=-1, keepdims=True)
    yc = y - mu
    var = jnp.mean(yc * yc, axis=-1, keepdims=True)
    return yc * lax.rsqrt(var + NORM_EPS) * g + b


def _route(logits):
    lane = lax.broadcasted_iota(I32, logits.shape, 1)
    is_g = lane < N_GROUPS
    gl = jnp.where(is_g, logits, -jnp.inf)
    gmax = jnp.max(gl, axis=-1, keepdims=True)
    g_sel = jnp.min(jnp.where(gl == gmax, lane, ROUTE_LANES), axis=-1, keepdims=True)
    p_g = 1.0 / jnp.sum(jnp.where(is_g, jnp.exp(gl - gmax), 0.0), axis=-1, keepdims=True)
    e_id = lane - N_GROUPS
    in_grp = jnp.logical_and(e_id >= g_sel * EXPERTS_PER_GROUP, e_id < (g_sel + 1) * EXPERTS_PER_GROUP)
    el = jnp.where(in_grp, logits, -jnp.inf)
    emax = jnp.max(el, axis=-1, keepdims=True)
    ex = jnp.where(in_grp, jnp.exp(el - emax), 0.0)
    pe = ex / jnp.sum(ex, axis=-1, keepdims=True)
    pe = jnp.where(in_grp, pe, -1.0)
    p1 = jnp.max(pe, axis=-1, keepdims=True)
    i1 = jnp.min(jnp.where(pe == p1, e_id, ROUTE_LANES), axis=-1, keepdims=True)
    pe2 = jnp.where(e_id == i1, -1.0, pe)
    p2 = jnp.max(pe2, axis=-1, keepdims=True)
    i2 = jnp.min(jnp.where(pe2 == p2, e_id, ROUTE_LANES), axis=-1, keepdims=True)
    denom = p1 + p2
    w1 = p_g * (p1 / denom)
    w2 = p_g * (p2 / denom)
    return jnp.where(lane == 0, w1,
                     jnp.where(lane == 1, w2,
                               jnp.where(lane == 2, i1.astype(F32),
                                         jnp.where(lane == 3, i2.astype(F32), 0.0))))


def _oproj_kernel(x_ref, m_ref, wo_ref, g_ref, b_ref, wr_ref, br_ref, x1_ref, r_ref, *, alpha):
    y = alpha * x_ref[...] + jnp.dot(m_ref[...], wo_ref[...], preferred_element_type=F32)
    x1 = _layer_norm(y, g_ref[...], b_ref[...])
    x1_ref[...] = x1
    x_hi = x1.astype(BF16)
    x_lo = (x1 - x_hi.astype(F32)).astype(BF16)
    parts = (jnp.dot(x_hi, wr_ref[...], preferred_element_type=F32)
             + jnp.dot(x_lo, wr_ref[...], preferred_element_type=F32))
    logits = parts + pltpu.roll(parts, ROUTE_LANES // 2, axis=1) + br_ref[...]
    r_ref[...] = _route(logits)


def _oproj(x, merged, w_o, ln_g, ln_b, w_r, b_r, *, alpha, tm):
    t, d = x.shape
    return pl.pallas_call(
        functools.partial(_oproj_kernel, alpha=alpha),
        grid=(t // tm,),
        in_specs=[pl.BlockSpec((tm, d), lambda i: (i, 0)),
                  pl.BlockSpec((tm, d), lambda i: (i, 0)),
                  _resident(w_o.shape),
                  pl.BlockSpec((1, d), lambda i: (0, 0)),
                  pl.BlockSpec((1, d), lambda i: (0, 0)),
                  _resident(w_r.shape),
                  pl.BlockSpec((1, ROUTE_LANES), lambda i: (0, 0))],
        out_specs=[pl.BlockSpec((tm, d), lambda i: (i, 0)),
                   pl.BlockSpec((tm, ROUTE_LANES), lambda i: (i, 0))],
        out_shape=[jax.ShapeDtypeStruct((t, d), F32),
                   jax.ShapeDtypeStruct((t, ROUTE_LANES), F32)],
        compiler_params=_cparams("parallel"),
        name="oproj",
    )(x, merged, w_o, ln_g.reshape(1, d), ln_b.reshape(1, d), w_r, b_r)


def _row_copy(src_hbm, row, dst, k, sem):
    return pltpu.make_async_copy(src_hbm.at[pl.ds(row, 1), :], dst.at[pl.ds(k, 1), :], sem)


def _expert_kernel(tok_ref, te_ref, first_ref, nxt_ref, ws_ref, nu_ref, x_hbm, w1_hbm, w3_hbm, w2_hbm,
                   o_ref, buf, w1b, w3b, w2b, sem, wsem, *, layer):
    i = pl.program_id(0)
    n_used = nu_ref[0]
    slot = lax.rem(i, 2)

    def gather(tile, s):
        base = tile * MOE_TM

        def body(r, c):
            _row_copy(x_hbm, tok_ref[base + r], buf.at[s], r, sem.at[s]).start()
            return c

        lax.fori_loop(0, MOE_TM, body, 0, unroll=ROW_DMA_UNROLL)

    def weight_copies(e, s):
        return [pltpu.make_async_copy(src.at[layer, e], dst.at[s], wsem.at[s])
                for src, dst in ((w1_hbm, w1b), (w3_hbm, w3b), (w2_hbm, w2b))]

    @pl.when(i == 0)
    def _():
        gather(0, 0)
        for cp in weight_copies(te_ref[0], 0):
            cp.start()

    @pl.when(i < n_used)
    def _():
        ws = ws_ref[i]

        @pl.when(first_ref[i] == 1)
        def _():
            for cp in weight_copies(te_ref[i], ws):
                cp.wait()

            @pl.when(nxt_ref[i] >= 0)
            def _():
                for cp in weight_copies(nxt_ref[i], 1 - ws):
                    cp.start()

        pltpu.make_async_copy(x_hbm.at[pl.ds(0, MOE_TM), :], buf.at[slot], sem.at[slot]).wait()

        @pl.when(i + 1 < n_used)
        def _():
            gather(i + 1, 1 - slot)

        xs = buf[slot].astype(BF16)
        h1 = jnp.dot(xs, w1b[ws].astype(BF16), preferred_element_type=F32)
        h3 = jnp.dot(xs, w3b[ws].astype(BF16), preferred_element_type=F32)
        hid = (h1 * _sigmoid(h1)) * h3
        o_ref[...] = jnp.dot(hid.astype(BF16), w2b[ws].astype(BF16), preferred_element_type=F32)

    @pl.when(i >= n_used)
    def _():
        o_ref[...] = jnp.zeros_like(o_ref)


def _experts(plan, x1, w1, w3, w2, layer):
    r = plan[0].shape[0]
    d = x1.shape[1]
    de = w1.shape[-1]
    n_pref = len(plan)
    return pl.pallas_call(
        functools.partial(_expert_kernel, layer=layer),
        grid_spec=pltpu.PrefetchScalarGridSpec(
            num_scalar_prefetch=n_pref,
            grid=(r // MOE_TM,),
            in_specs=[pl.BlockSpec(memory_space=pl.ANY)] * 4,
            out_specs=pl.BlockSpec((MOE_TM, d), lambda i, *_: (i, 0)),
            scratch_shapes=[pltpu.VMEM((2, MOE_TM, d), F32),
                            pltpu.VMEM((2, d, de), F32), pltpu.VMEM((2, d, de), F32),
                            pltpu.VMEM((2, de, d), F32),
                            pltpu.SemaphoreType.DMA((2,)), pltpu.SemaphoreType.DMA((2,))],
        ),
        out_shape=jax.ShapeDtypeStruct((r, d), F32),
        compiler_params=_cparams("arbitrary"),
        name="moe_experts",
    )(*plan, x1, w1, w3, w2)


def _combine_kernel(pos_ref, x1_ref, route_ref, g_ref, b_ref, ys_hbm, o_ref, buf, sem, *, alpha):
    i = pl.program_id(0)
    slot = lax.rem(i, 2)

    def gather(tile, s):
        base = tile * MOE_TM

        def body(r, c):
            for k in range(2):
                _row_copy(ys_hbm, pos_ref[2 * (base + r) + k], buf.at[s, k], r, sem.at[s]).start()
            return c

        lax.fori_loop(0, MOE_TM, body, 0, unroll=ROW_DMA_UNROLL // 2)

    @pl.when(i == 0)
    def _():
        gather(0, 0)

    for k in range(2):
        pltpu.make_async_copy(ys_hbm.at[pl.ds(0, MOE_TM), :], buf.at[slot, k], sem.at[slot]).wait()

    @pl.when(i + 1 < pl.num_programs(0))
    def _():
        gather(i + 1, 1 - slot)

    route = route_ref[...]
    y = alpha * x1_ref[...] + (route[:, 0:1] * buf[slot, 0] + route[:, 1:2] * buf[slot, 1])
    o_ref[...] = _layer_norm(y, g_ref[...], b_ref[...])


def _combine(pos, x1, route, ln_g, ln_b, ys, *, alpha):
    t, d = x1.shape
    return pl.pallas_call(
        functools.partial(_combine_kernel, alpha=alpha),
        grid_spec=pltpu.PrefetchScalarGridSpec(
            num_scalar_prefetch=1,
            grid=(t // MOE_TM,),
            in_specs=[pl.BlockSpec((MOE_TM, d), lambda i, pos: (i, 0)),
                      pl.BlockSpec((MOE_TM, ROUTE_LANES), lambda i, pos: (i, 0)),
                      pl.BlockSpec((1, d), lambda i, pos: (0, 0)),
                      pl.BlockSpec((1, d), lambda i, pos: (0, 0)),
                      pl.BlockSpec(memory_space=pl.ANY)],
            out_specs=pl.BlockSpec((MOE_TM, d), lambda i, pos: (i, 0)),
            scratch_shapes=[pltpu.VMEM((2, 2, MOE_TM, d), F32), pltpu.SemaphoreType.DMA((2,))],
        ),
        out_shape=jax.ShapeDtypeStruct((t, d), F32),
        compiler_params=_cparams("arbitrary"),
        name="moe_combine",
    )(pos, x1, route, ln_g.reshape(1, d), ln_b.reshape(1, d), ys)


def _moe_plan(route, t):
    flat_e = route[:, 2:4].astype(I32).reshape(-1)
    n = flat_e.shape[0]
    n_rows = n + N_EXPERTS * MOE_TM
    n_tiles = n_rows // MOE_TM
    onehot = (flat_e[:, None] == jnp.arange(N_EXPERTS, dtype=I32)[None, :]).astype(I32)
    running = jnp.cumsum(onehot, axis=0)
    rank = jnp.sum(onehot * running, axis=1) - 1
    counts = running[-1]
    tiles_per = (counts + MOE_TM - 1) // MOE_TM
    tile_end = jnp.cumsum(tiles_per)
    grp_row0 = (tile_end - tiles_per) * MOE_TM
    pos = jnp.sum(onehot * grp_row0[None, :], axis=1) + rank
    row_token = jnp.zeros((n_rows,), I32).at[pos].set(jnp.arange(n, dtype=I32) // 2, unique_indices=True)
    n_used = tile_end[-1]
    tile_ids = jnp.minimum(jnp.arange(n_tiles, dtype=I32), n_used - 1)
    tile_expert = jnp.sum((tile_end[None, :] <= tile_ids[:, None]).astype(I32), axis=1)
    tile_expert = jnp.minimum(tile_expert, N_EXPERTS - 1)
    first = jnp.concatenate([jnp.ones((1,), I32), (tile_expert[1:] != tile_expert[:-1]).astype(I32)])
    w_slot = (jnp.cumsum(first) - 1) % 2
    e_ids = jnp.arange(N_EXPERTS, dtype=I32)
    later = jnp.logical_and(e_ids[None, :] > e_ids[:, None], tiles_per[None, :] > 0)
    next_e = jnp.min(jnp.where(later, e_ids[None, :], N_EXPERTS), axis=1)
    next_e = jnp.where(next_e == N_EXPERTS, -1, next_e)
    nxt = jnp.sum((tile_expert[:, None] == e_ids[None, :]).astype(I32) * next_e[None, :], axis=1)
    plan = (row_token, tile_expert, first, nxt.astype(I32), w_slot.astype(I32), n_used.reshape(1).astype(I32))
    return plan, pos


def kernel(x, mem, w_mem_kv, w_in, g_cq, g_ckv, g_kidx, b_kidx, w_uq, w_uqi, w_ukv,
           w_up_a, w_up_b, w_up_c, w_gate, b_gate, w_o, ln1_g, ln1_b,
           w_grp, b_grp, w_rt, b_rt, w1, w3, w2, ln2_g, ln2_b):
    bsz, t, d = x.shape
    assert bsz == 1 and t % TQ == 0
    depth = w_in.shape[0]
    alpha = (2 * depth) ** 0.25
    scale = HEAD_DIM ** -0.5 * LOG2E
    top_k = min(TOPK_A_MAX, t // 4)
    n_mem = mem.shape[1]
    tm = min(t, 512)

    slopes = LOG2E * 2.0 ** (-8.0 * jnp.arange(1, N_ALIBI + 1, dtype=F32) / N_ALIBI)
    slopes_a, slopes_b = slopes[0::2], slopes[1::2]

    mkv = _mm(mem[0], w_mem_kv.astype(BF16), out_dtype=BF16, tm=n_mem, tn=W_C)
    mk = mkv[:, :W_C]
    mvt = mkv[:, W_C:].T

    xl = x[0]
    for l in range(depth):
        wl = w_in[l]
        o = 0
        parts = []
        for width in (Q_RANK_A, KV_RANK_A, IDX_DIM, N_IDX_HEADS, W_B, W_B, W_B, W_C):
            parts.append(wl[:, o:o + width])
            o += width
        w_cq, w_ckv, w_ki, w_wi, w_qb, w_kb, w_vb, w_qc = parts
        n_small = Q_RANK_A + KV_RANK_A + IDX_DIM + N_IDX_HEADS
        pad = (-n_small) % LANES
        w_p1 = jnp.concatenate([w_cq, w_ckv, w_ki, w_wi * N_IDX_HEADS ** -0.5,
                                jnp.zeros((d, pad), F32)], axis=1).astype(BF16)
        w_p2 = jnp.concatenate([w_qb * scale, w_kb, w_vb, w_qc * scale], axis=1).astype(BF16)
        p1 = _mm(xl, w_p1, out_dtype=F32, tm=tm, tn=w_p1.shape[1])
        p2 = _mm(xl, w_p2, out_dtype=BF16, tm=tm, tn=w_p2.shape[1] // 2)
        assert Q_RANK_A % KV_RANK_A == 0 and (Q_RANK_A + KV_RANK_A) % LANES == 0

        w_q = jnp.concatenate([w_uq[l] * scale, w_uqi[l] * IDX_DIM ** -0.5], axis=1).astype(BF16)
        qq = _mm(p1, w_q, out_dtype=BF16, tm=tm, tn=w_q.shape[1], norm_g=g_cq[l])
        kv = _mm(p1, w_ukv[l].astype(BF16), out_dtype=BF16, tm=tm, tn=2 * W_A, norm_g=g_ckv[l],
                 x_col_block=Q_RANK_A // KV_RANK_A)
        kidx, wi_t = _kidx_norm(p1, (Q_RANK_A + KV_RANK_A) // LANES, g_kidx[l], b_kidx[l])
        qi_r = (qq[:, W_A:].reshape(t // TQ, TQ, N_IDX_HEADS, IDX_DIM).transpose(0, 2, 1, 3)
                .reshape(t // TQ, N_IDX_HEADS * TQ, IDX_DIM))
        o_a = _dsa(slopes_a, qi_r, wi_t, kidx, qq, kv, kv[:, W_A:].T, top_k)
        o_b = _moba(slopes_b, p2, p2[:, 2 * W_B:3 * W_B].T)
        o_c = _mem_attn(p2[:, 3 * W_B:], mk, mvt)

        merged = _merge(xl, w_gate[l].astype(BF16), b_gate[l], o_a, o_b, o_c,
                        w_up_a[l].astype(BF16), w_up_b[l].astype(BF16), w_up_c[l].astype(BF16),
                        tm=min(t, 1024), tn=512)

        n_route = N_GROUPS + N_EXPERTS
        w_r = jnp.concatenate([w_grp[l], w_rt[l]], axis=1)
        w_r_hi = w_r.astype(BF16)
        w_r_lo = (w_r - w_r_hi.astype(F32)).astype(BF16)
        lane_pad = jnp.zeros((d, ROUTE_LANES // 2 - n_route), BF16)
        w_r2 = jnp.concatenate([w_r_hi, lane_pad, w_r_lo, lane_pad], axis=1)
        b_r = jnp.concatenate([b_grp[l], b_rt[l],
                               jnp.zeros((ROUTE_LANES - n_route,), F32)]).reshape(1, ROUTE_LANES)
        x1, route = _oproj(xl, merged, w_o[l].astype(BF16), ln1_g[l], ln1_b[l], w_r2, b_r,
                           alpha=alpha, tm=tm)

        plan, pos = _moe_plan(route, t)
        ys = _experts(plan, x1, w1, w3, w2, l)
        xl = _combine(pos, x1, route, ln2_g[l], ln2_b[l], ys, alpha=alpha)
    return xl[None]
```

```python
import functools

import jax
import jax.numpy as jnp
from jax import lax
from jax.experimental import pallas as pl
from jax.experimental.pallas import tpu as pltpu

HEAD_DIM = 128
N_HEADS_A = 6
N_HEADS_B = 6
N_HEADS_C = 4
W_A = N_HEADS_A * HEAD_DIM
W_B = N_HEADS_B * HEAD_DIM
W_C = N_HEADS_C * HEAD_DIM
Q_RANK_A = 512
KV_RANK_A = 256
N_IDX_HEADS = 16
IDX_DIM = 64
TOPK_A_MAX = 256
MOBA_BLOCK = 256
MOBA_TOPK = 3
N_ALIBI = N_HEADS_A + N_HEADS_B
N_GROUPS = 4
EXPERTS_PER_GROUP = 8
N_EXPERTS = N_GROUPS * EXPERTS_PER_GROUP
D_EXPERT = 512
NORM_EPS = 1e-5
NEG = -1e30
LOG2E = 1.4426950408889634
I16_MIN = -(2 ** 15)

LANES = 128
SUBLANES = 8
PACKED_ROWS = 2 * SUBLANES
TQ = 256
CK = 256
ROUTE_LANES = 128
MOE_TM = 256
ROW_DMA_UNROLL = 8
BULK_DMA_PRIORITY = 1
VMEM_LIMIT = 56 * 1024 * 1024

F32 = jnp.float32
BF16 = jnp.bfloat16
I32 = jnp.int32
I16 = jnp.int16

_NT = (((1,), (1,)), ((), ()))


def _cparams(*sem):
    return pltpu.CompilerParams(dimension_semantics=sem, vmem_limit_bytes=VMEM_LIMIT)


def _resident(shape, col_block=0):
    index = (0,) * (len(shape) - 1) + (col_block,)
    return pl.BlockSpec(shape, lambda *_: index, pipeline_mode=pl.Buffered(1))


def _mm_kernel(*refs, has_norm, has_bias):
    x_ref, w_ref = refs[0], refs[1]
    k = 2
    x = x_ref[...]
    if has_norm:
        g_ref = refs[k]
        k += 1
        xf = x.astype(F32)
        x = xf * lax.rsqrt(jnp.mean(xf * xf, axis=-1, keepdims=True) + NORM_EPS) * g_ref[...]
    acc = jnp.dot(x.astype(BF16), w_ref[...], preferred_element_type=F32)
    if has_bias:
        acc = acc + refs[k][...]
        k += 1
    o_ref = refs[k]
    o_ref[...] = acc.astype(o_ref.dtype)


def _mm(x, w, *, out_dtype, tm, tn, norm_g=None, bias=None, x_col_block=0):
    m = x.shape[0]
    kdim, n = w.shape
    assert m % tm == 0 and n % tn == 0
    in_specs = [pl.BlockSpec((tm, kdim), lambda i, j: (i, x_col_block)),
                pl.BlockSpec((kdim, tn), lambda i, j: (0, j))]
    args = [x, w]
    if norm_g is not None:
        in_specs.append(pl.BlockSpec((1, kdim), lambda i, j: (0, 0)))
        args.append(norm_g.reshape(1, kdim).astype(F32))
    if bias is not None:
        in_specs.append(pl.BlockSpec((1, tn), lambda i, j: (0, j)))
        args.append(bias.reshape(1, n).astype(F32))
    return pl.pallas_call(
        functools.partial(_mm_kernel, has_norm=norm_g is not None, has_bias=bias is not None),
        grid=(m // tm, n // tn),
        in_specs=in_specs,
        out_specs=pl.BlockSpec((tm, tn), lambda i, j: (i, j)),
        out_shape=jax.ShapeDtypeStruct((m, n), out_dtype),
        compiler_params=_cparams("parallel", "parallel"),
        name="mm",
    )(*args)


def _kidx_kernel(x_ref, g_ref, b_ref, k_ref, w_ref):
    blk = x_ref[...]
    x = blk[:, :IDX_DIM]
    mu = jnp.mean(x, axis=-1, keepdims=True)
    xc = x - mu
    var = jnp.mean(xc * xc, axis=-1, keepdims=True)
    k_ref[...] = (xc * lax.rsqrt(var + NORM_EPS) * g_ref[...] + b_ref[...]).astype(k_ref.dtype)
    w_ref[...] = blk.T[IDX_DIM:IDX_DIM + N_IDX_HEADS, :]


def _kidx_norm(p1, col_block, g, b):
    t = p1.shape[0]
    tm = min(t, 1024)
    return pl.pallas_call(
        _kidx_kernel,
        grid=(t // tm,),
        in_specs=[pl.BlockSpec((tm, LANES), lambda i: (i, col_block)),
                  pl.BlockSpec((1, IDX_DIM), lambda i: (0, 0)),
                  pl.BlockSpec((1, IDX_DIM), lambda i: (0, 0))],
        out_specs=[pl.BlockSpec((tm, IDX_DIM), lambda i: (i, 0)),
                   pl.BlockSpec((N_IDX_HEADS, tm), lambda i: (0, i))],
        out_shape=[jax.ShapeDtypeStruct((t, IDX_DIM), BF16),
                   jax.ShapeDtypeStruct((N_IDX_HEADS, t), F32)],
        compiler_params=_cparams("parallel"),
        name="kidx_norm",
    )(p1, g.reshape(1, IDX_DIM), b.reshape(1, IDX_DIM))


def _head(h):
    return slice(h * HEAD_DIM, (h + 1) * HEAD_DIM)


def _attn_state(n_heads):
    return ([pltpu.VMEM((CK, TQ), F32) for _ in range(2 * n_heads)]
            + [pltpu.VMEM((HEAD_DIM, TQ), F32) for _ in range(n_heads)]
            + [pltpu.VMEM((1, TQ), F32) for _ in range(2 * n_heads)])


def _attn_reset(state, n_heads):
    state = state[2 * n_heads:]
    for h in range(n_heads):
        state[h][...] = jnp.zeros((HEAD_DIM, TQ), F32)
        state[n_heads + h][...] = jnp.full((1, TQ), NEG, F32)
        state[2 * n_heads + h][...] = jnp.zeros((1, TQ), F32)


def _attn_run(state, n_heads, i, q_ref, k_ref, consume):
    def scores(j, slot):
        start = pl.multiple_of(j * CK, CK)
        for h in range(n_heads):
            state[slot * n_heads + h][...] = lax.dot_general(
                k_ref[pl.ds(start, CK), _head(h)], q_ref[:, _head(h)], _NT,
                preferred_element_type=F32)

    scores(0, 0)

    def pair(p, c):
        j = 2 * p
        scores(j + 1, 1)
        consume(j, 0, False)
        scores(j + 2, 0)
        consume(j + 1, 1, False)
        return c

    lax.fori_loop(0, i // 2, pair, 0)

    @pl.when(i % 2 == 0)
    def _():
        consume(i, 0, True)

    @pl.when(i % 2 == 1)
    def _():
        scores(i, 1)
        consume(i - 1, 0, False)
        consume(i, 1, True)


def _attn_update(state, n_heads, h, slot, vt_blk, t_bias, c):
    s_ref, state = state[slot * n_heads + h], state[2 * n_heads:]
    acc_ref, m_ref, l_ref = state[h], state[n_heads + h], state[2 * n_heads + h]
    t = s_ref[...] + t_bias
    m_old = m_ref[...]
    m_new = jnp.maximum(m_old, jnp.max(t, axis=0, keepdims=True) + c)
    alpha = jnp.exp2(m_old - m_new)
    p = jnp.exp2(t - (m_new - c))
    l_ref[...] = alpha * l_ref[...] + jnp.sum(p, axis=0, keepdims=True)
    acc_ref[...] = alpha * acc_ref[...] + jnp.dot(vt_blk, p.astype(BF16), preferred_element_type=F32)
    m_ref[...] = m_new


def _attn_finish(state, n_heads, h):
    state = state[2 * n_heads:]
    return (state[h][...] * (1.0 / state[2 * n_heads + h][...])).T


def _key_offsets():
    return lax.broadcasted_iota(I32, (CK, TQ), 0), lax.broadcasted_iota(I32, (CK, TQ), 1)


def _sortable(x):
    b = pltpu.bitcast(x, I32)
    return b ^ ((b >> 31) & 0x7FFFFFFF)


def _dsa_kernel(slopes_ref, qi_ref, wi_ref, kidx_ref, qa_ref, ka_ref, vat_ref, o_ref,
                keys_ref, half_ref, sb_ref, *state, top_k, n_idx_bits):
    i = pl.program_id(0)
    rows, cols = _key_offsets()
    causal = rows <= cols
    nh = N_HEADS_A

    @pl.when(i == 0)
    def _():
        rows_f = rows.astype(F32)
        for h in range(nh):
            sb_ref[h] = slopes_ref[h] * rows_f

    def score_chunk(j, diagonal):
        start = pl.multiple_of(j * CK, CK)
        kc = kidx_ref[pl.ds(start, CK), :]
        acc = jnp.zeros((CK, TQ), F32)
        for h in range(N_IDX_HEADS):
            z = lax.dot_general(kc, qi_ref[0, h * TQ:(h + 1) * TQ, :], _NT,
                                preferred_element_type=F32)
            acc = acc + wi_ref[h:h + 1, :] * jnp.maximum(z, 0.0)
        if diagonal:
            acc = jnp.where(causal, acc, NEG)
        key = _sortable(acc)
        keys_ref[pl.ds(start, CK), :] = key
        half_ref[pl.ds(start, CK), :] = (key >> 16).astype(I16)

    def score_body(j, c):
        score_chunk(j, False)
        return c

    lax.fori_loop(0, i, score_body, 0)
    score_chunk(i, True)
    half_ref[pl.ds(pl.multiple_of((i + 1) * CK, CK), CK), :] = jnp.full((CK, TQ), I16_MIN, I16)

    def count16(cand, strict):
        cand = cand.astype(I16)

        def body(p, cnt):
            blk = half_ref[pl.ds(pl.multiple_of(p * (2 * CK), 2 * CK), 2 * CK), :]
            hit = jnp.where((blk > cand) if strict else (blk >= cand), jnp.int16(1), jnp.int16(0))
            for r in range(0, 2 * CK, PACKED_ROWS):
                cnt = cnt + hit[r:r + PACKED_ROWS, :]
            return cnt

        cnt = lax.fori_loop(0, (i + 2) // 2, body, jnp.zeros((PACKED_ROWS, TQ), I16))
        return jnp.sum(cnt.astype(I32), axis=0, keepdims=True)

    def search16(target, n_all):
        def bit_body(b, carry):
            thr, n_thr = carry
            cand = thr + jnp.left_shift(jnp.int32(1), 15 - b)
            n_cand = count16(cand, False)
            ok = n_cand >= target
            return jnp.where(ok, cand, thr), jnp.where(ok, n_cand, n_thr)
        return lax.fori_loop(0, 16, bit_body, (jnp.full((1, TQ), I16_MIN, I32), n_all))

    t_hi, n_ge_hi = search16(top_k, jnp.zeros((1, TQ), I32) + (i + 1) * CK)
    n_gt_hi = count16(t_hi, True)
    need = top_k - n_gt_hi

    def low_half(j, c):
        start = pl.multiple_of(j * CK, CK)
        key = keys_ref[pl.ds(start, CK), :]
        low = ((key ^ 0x8000) << 16) >> 16
        half_ref[pl.ds(start, CK), :] = jnp.where((key >> 16) == t_hi, low, I16_MIN).astype(I16)
        return c

    lax.fori_loop(0, i + 1, low_half, 0)
    t_lo, n_ge_lo = search16(need, n_ge_hi - n_gt_hi)
    thr = (t_hi << 16) + (t_lo - I16_MIN)

    @pl.when(jnp.max(n_gt_hi + n_ge_lo) > top_k)
    def _():
        def count32(pred):
            def body(j, cnt):
                start = pl.multiple_of(j * CK, CK)
                hit = jnp.where(pred(keys_ref[pl.ds(start, CK), :], rows + j * CK), 1, 0)
                return cnt + jnp.sum(hit.reshape(CK // SUBLANES, SUBLANES, TQ), axis=0)
            cnt = lax.fori_loop(0, i + 1, body, jnp.zeros((SUBLANES, TQ), I32))
            return jnp.sum(cnt, axis=0, keepdims=True)

        keep = top_k - count32(lambda key, idx: key > thr)

        def idx_bit(b, last):
            cand = last + jnp.left_shift(jnp.int32(1), n_idx_bits - 1 - b)
            below = count32(lambda key, idx: jnp.logical_and(key == thr, idx < cand))
            return jnp.where(below < keep, cand, last)

        last = lax.fori_loop(0, n_idx_bits, idx_bit, jnp.zeros((1, TQ), I32))

        def drop(j, c):
            start = pl.multiple_of(j * CK, CK)
            key = keys_ref[pl.ds(start, CK), :]
            late_tie = jnp.logical_and(key == thr, rows + j * CK > last)
            keys_ref[pl.ds(start, CK), :] = jnp.where(late_tie, key - 1, key)
            return c

        lax.fori_loop(0, i + 1, drop, 0)

    _attn_reset(state, nh)

    def consume(j, slot, diagonal):
        start = pl.multiple_of(j * CK, CK)
        sel = keys_ref[pl.ds(start, CK), :] >= thr
        if diagonal:
            sel = jnp.logical_and(sel, causal)
        mask_bias = jnp.where(sel, 0.0, NEG)
        off = ((j - i) * CK).astype(F32)
        for h in range(nh):
            _attn_update(state, nh, h, slot, vat_ref[_head(h), pl.ds(start, CK)],
                         sb_ref[h] + mask_bias, slopes_ref[h] * off)

    _attn_run(state, nh, i, qa_ref, ka_ref, consume)
    for h in range(nh):
        o_ref[:, _head(h)] = _attn_finish(state, nh, h).astype(o_ref.dtype)


def _dsa(slopes, qi_r, wi_t, kidx, qa, ka, vat, top_k):
    w, t = vat.shape
    nh = w // HEAD_DIM
    return pl.pallas_call(
        functools.partial(_dsa_kernel, top_k=top_k, n_idx_bits=max(1, (t - 1).bit_length())),
        grid_spec=pltpu.PrefetchScalarGridSpec(
            num_scalar_prefetch=0,
            grid=(t // TQ,),
            in_specs=[
                pl.BlockSpec(memory_space=pltpu.SMEM),
                pl.BlockSpec((1, N_IDX_HEADS * TQ, IDX_DIM), lambda i: (i, 0, 0)),
                pl.BlockSpec((N_IDX_HEADS, TQ), lambda i: (0, i)),
                _resident(kidx.shape),
                pl.BlockSpec((TQ, w), lambda i: (i, 0)),
                _resident((t, w)),
                _resident(vat.shape),
            ],
            out_specs=pl.BlockSpec((TQ, w), lambda i: (i, 0)),
            scratch_shapes=[pltpu.VMEM((t, TQ), I32), pltpu.VMEM((t + CK, TQ), I16),
                            pltpu.VMEM((nh, CK, TQ), F32)] + _attn_state(nh),
        ),
        out_shape=jax.ShapeDtypeStruct((t, w), BF16),
        compiler_params=_cparams("arbitrary"),
        name="dsa",
    )(slopes, qi_r, wi_t, kidx, qa, ka, vat)


def _moba_kernel(slopes_ref, qb_ref, kb_ref, vbt_ref, o_ref, kmean_ref, sel_ref, sb_ref, *state, n_kb):
    i = pl.program_id(0)
    rows, cols = _key_offsets()
    nh = N_HEADS_B

    @pl.when(i == 0)
    def _():
        rows_f = rows.astype(F32)
        for h in range(nh):
            sb_ref[h] = slopes_ref[h] * rows_f
            for n in range(n_kb):
                blk = kb_ref[n * CK:(n + 1) * CK, _head(h)].astype(F32)
                kmean_ref[h, n:n + 1, :] = jnp.mean(blk, axis=0, keepdims=True)

    blk_id = lax.broadcasted_iota(I32, (n_kb, TQ), 0)
    for h in range(nh):
        q_h = qb_ref[:, _head(h)]
        km = kmean_ref[h]
        km_hi = km.astype(BF16)
        km_lo = (km - km_hi.astype(F32)).astype(BF16)
        gate = (lax.dot_general(km_hi, q_h, _NT, preferred_element_type=F32)
                + lax.dot_general(km_lo, q_h, _NT, preferred_element_type=F32))
        gate = jnp.where(blk_id < i, gate, NEG)
        sel = jnp.full((n_kb, TQ), NEG, F32)
        for _ in range(MOBA_TOPK):
            best = jnp.max(gate, axis=0, keepdims=True)
            first = jnp.min(jnp.where(gate == best, blk_id, n_kb), axis=0, keepdims=True)
            pick = blk_id == first
            sel = jnp.where(pick, 0.0, sel)
            gate = jnp.where(pick, -jnp.inf, gate)
        sel_ref[h] = jnp.where(blk_id < i, sel, NEG)

    _attn_reset(state, nh)

    def consume(j, slot, diagonal):
        start = pl.multiple_of(j * CK, CK)
        off = ((j - i) * CK).astype(F32)
        for h in range(nh):
            if diagonal:
                t_bias = jnp.where(rows <= cols, sb_ref[h], NEG)
            else:
                t_bias = sb_ref[h] + sel_ref[h, pl.ds(j, 1), :]
            _attn_update(state, nh, h, slot, vbt_ref[_head(h), pl.ds(start, CK)],
                         t_bias, slopes_ref[h] * off)

    _attn_run(state, nh, i, qb_ref, kb_ref, consume)
    for h in range(nh):
        o_ref[:, _head(h)] = _attn_finish(state, nh, h).astype(o_ref.dtype)


def _moba(slopes, qkv, vbt):
    w, t = vbt.shape
    nh = w // HEAD_DIM
    assert t % MOBA_BLOCK == 0 and TQ == MOBA_BLOCK and CK == MOBA_BLOCK
    n_kb = t // MOBA_BLOCK
    return pl.pallas_call(
        functools.partial(_moba_kernel, n_kb=n_kb),
        grid_spec=pltpu.PrefetchScalarGridSpec(
            num_scalar_prefetch=0,
            grid=(t // TQ,),
            in_specs=[
                pl.BlockSpec(memory_space=pltpu.SMEM),
                pl.BlockSpec((TQ, w), lambda i: (i, 0)),
                _resident((t, w), col_block=1),
                _resident(vbt.shape),
            ],
            out_specs=pl.BlockSpec((TQ, w), lambda i: (i, 0)),
            scratch_shapes=[pltpu.VMEM((nh, n_kb, HEAD_DIM), F32), pltpu.VMEM((nh, n_kb, TQ), F32),
                            pltpu.VMEM((nh, CK, TQ), F32)] + _attn_state(nh),
        ),
        out_shape=jax.ShapeDtypeStruct((t, w), BF16),
        compiler_params=_cparams("arbitrary"),
        name="moba",
    )(slopes, qkv, qkv, vbt)


def _mem_kernel(qc_ref, mk_ref, mvt_ref, o_ref):
    for h in range(N_HEADS_C):
        s = lax.dot_general(mk_ref[:, _head(h)], qc_ref[:, _head(h)], _NT,
                            preferred_element_type=F32)
        m = jnp.max(s, axis=0, keepdims=True)
        p = jnp.exp2(s - m)
        l = jnp.sum(p, axis=0, keepdims=True)
        acc = jnp.dot(mvt_ref[_head(h), :], p.astype(BF16), preferred_element_type=F32)
        o_ref[:, _head(h)] = (acc * (1.0 / l)).T.astype(o_ref.dtype)


def _mem_attn(qc, mk, mvt):
    t, w = qc.shape
    return pl.pallas_call(
        _mem_kernel,
        grid=(t // TQ,),
        in_specs=[pl.BlockSpec((TQ, w), lambda i: (i, 0)),
                  _resident(mk.shape), _resident(mvt.shape)],
        out_specs=pl.BlockSpec((TQ, w), lambda i: (i, 0)),
        out_shape=jax.ShapeDtypeStruct((t, w), BF16),
        compiler_params=_cparams("parallel"),
        name="mem_attn",
    )(qc, mk, mvt)


def _sigmoid(x):
    return 1.0 / (1.0 + jnp.exp(-x))


def _merge_kernel(x_ref, wga_ref, wgb_ref, wgc_ref, bga_ref, bgb_ref, bgc_ref,
                  oa_ref, ob_ref, oc_ref, wua_ref, wub_ref, wuc_ref, o_ref):
    xb = x_ref[...].astype(BF16)

    def branch(wg_ref, bg_ref, oo_ref, wu_ref):
        g = _sigmoid(jnp.dot(xb, wg_ref[...], preferred_element_type=F32) + bg_ref[...])
        return g * jnp.dot(oo_ref[...], wu_ref[...], preferred_element_type=F32)

    acc = branch(wga_ref, bga_ref, oa_ref, wua_ref)
    acc = acc + branch(wgb_ref, bgb_ref, ob_ref, wub_ref)
    acc = acc + branch(wgc_ref, bgc_ref, oc_ref, wuc_ref)
    o_ref[...] = acc.astype(o_ref.dtype)


def _merge(x, w_gate, b_gate, o_a, o_b, o_c, w_up_a, w_up_b, w_up_c, *, tm, tn):
    t, d = x.shape
    nb = d // tn
    b_gate = b_gate.reshape(1, 3 * d)

    def wg(k):
        return pl.BlockSpec((d, tn), lambda i, j, k=k: (0, j + k * nb))

    def bg(k):
        return pl.BlockSpec((1, tn), lambda i, j, k=k: (0, j + k * nb))

    def act(w):
        return pl.BlockSpec((tm, w), lambda i, j: (i, 0))

    def wu(w):
        return pl.BlockSpec((w, tn), lambda i, j: (0, j))

    return pl.pallas_call(
        _merge_kernel,
        grid=(t // tm, nb),
        in_specs=[pl.BlockSpec((tm, d), lambda i, j: (i, 0)),
                  wg(0), wg(1), wg(2), bg(0), bg(1), bg(2),
                  act(W_A), act(W_B), act(W_C), wu(W_A), wu(W_B), wu(W_C)],
        out_specs=pl.BlockSpec((tm, tn), lambda i, j: (i, j)),
        out_shape=jax.ShapeDtypeStruct((t, d), BF16),
        compiler_params=_cparams("parallel", "parallel"),
        name="merge",
    )(x, w_gate, w_gate, w_gate, b_gate, b_gate, b_gate, o_a, o_b, o_c, w_up_a, w_up_b, w_up_c)


def _layer_norm(y, g, b):
    mu = jnp.mean(y, axis=-1, keepdims=True)
    yc = y - mu
    var = jnp.mean(yc * yc, axis=-1, keepdims=True)
    return yc * lax.rsqrt(var + NORM_EPS) * g + b


def _route(logits):
    lane = lax.broadcasted_iota(I32, logits.shape, 1)
    is_g = lane < N_GROUPS
    gl = jnp.where(is_g, logits, -jnp.inf)
    gmax = jnp.max(gl, axis=-1, keepdims=True)
    g_sel = jnp.min(jnp.where(gl == gmax, lane, ROUTE_LANES), axis=-1, keepdims=True)
    p_g = 1.0 / jnp.sum(jnp.where(is_g, jnp.exp(gl - gmax), 0.0), axis=-1, keepdims=True)
    e_id = lane - N_GROUPS
    in_grp = jnp.logical_and(e_id >= g_sel * EXPERTS_PER_GROUP, e_id < (g_sel + 1) * EXPERTS_PER_GROUP)
    el = jnp.where(in_grp, logits, -jnp.inf)
    emax = jnp.max(el, axis=-1, keepdims=True)
    ex = jnp.where(in_grp, jnp.exp(el - emax), 0.0)
    pe = ex / jnp.sum(ex, axis=-1, keepdims=True)
    pe = jnp.where(in_grp, pe, -1.0)
    p1 = jnp.max(pe, axis=-1, keepdims=True)
    i1 = jnp.min(jnp.where(pe == p1, e_id, ROUTE_LANES), axis=-1, keepdims=True)
    pe2 = jnp.where(e_id == i1, -1.0, pe)
    p2 = jnp.max(pe2, axis=-1, keepdims=True)
    i2 = jnp.min(jnp.where(pe2 == p2, e_id, ROUTE_LANES), axis=-1, keepdims=True)
    denom = p1 + p2
    w1 = p_g * (p1 / denom)
    w2 = p_g * (p2 / denom)
    return jnp.where(lane == 0, w1,
                     jnp.where(lane == 1, w2,
                               jnp.where(lane == 2, i1.astype(F32),
                                         jnp.where(lane == 3, i2.astype(F32), 0.0))))


def _pack_halves(y):
    half = y.shape[1] // 2
    lo = pltpu.bitcast(y[:, :half].astype(BF16).astype(F32), I32)
    hi = pltpu.bitcast(y[:, half:].astype(BF16).astype(F32), I32)
    return (hi & -65536) | lax.shift_right_logical(lo, 16)


def _unpack_halves(w):
    lo = pltpu.bitcast(w << 16, F32)
    hi = pltpu.bitcast(w & -65536, F32)
    return jnp.concatenate([lo, hi], axis=1)


def _oproj_kernel(x_ref, m_ref, wo_ref, g_ref, b_ref, wr_ref, br_ref, x1_ref, xp_ref, r_ref, *, alpha):
    y = alpha * x_ref[...] + jnp.dot(m_ref[...], wo_ref[...], preferred_element_type=F32)
    x1 = _layer_norm(y, g_ref[...], b_ref[...])
    x1_ref[...] = x1
    xp_ref[...] = _pack_halves(x1)
    x_hi = x1.astype(BF16)
    x_lo = (x1 - x_hi.astype(F32)).astype(BF16)
    parts = (jnp.dot(x_hi, wr_ref[...], preferred_element_type=F32)
             + jnp.dot(x_lo, wr_ref[...], preferred_element_type=F32))
    logits = parts + pltpu.roll(parts, ROUTE_LANES // 2, axis=1) + br_ref[...]
    r_ref[...] = _route(logits)


def _oproj(x, merged, w_o, ln_g, ln_b, w_r, b_r, *, alpha, tm):
    t, d = x.shape
    return pl.pallas_call(
        functools.partial(_oproj_kernel, alpha=alpha),
        grid=(t // tm,),
        in_specs=[pl.BlockSpec((tm, d), lambda i: (i, 0)),
                  pl.BlockSpec((tm, d), lambda i: (i, 0)),
                  _resident(w_o.shape),
                  pl.BlockSpec((1, d), lambda i: (0, 0)),
                  pl.BlockSpec((1, d), lambda i: (0, 0)),
                  _resident(w_r.shape),
                  pl.BlockSpec((1, ROUTE_LANES), lambda i: (0, 0))],
        out_specs=[pl.BlockSpec((tm, d), lambda i: (i, 0)),
                   pl.BlockSpec((tm, d // 2), lambda i: (i, 0)),
                   pl.BlockSpec((tm, ROUTE_LANES), lambda i: (i, 0))],
        out_shape=[jax.ShapeDtypeStruct((t, d), F32),
                   jax.ShapeDtypeStruct((t, d // 2), I32),
                   jax.ShapeDtypeStruct((t, ROUTE_LANES), F32)],
        compiler_params=_cparams("parallel"),
        name="oproj",
    )(x, merged, w_o, ln_g.reshape(1, d), ln_b.reshape(1, d), w_r, b_r)


def _row_copy(src_hbm, row, dst, k, sem):
    return pltpu.make_async_copy(src_hbm.at[pl.ds(row, 1), :], dst.at[pl.ds(k, 1), :], sem)


def _expert_kernel(tok_ref, te_ref, first_ref, nxt_ref, ws_ref, nu_ref, x_hbm, w1_hbm, w3_hbm, w2_hbm,
                   o_ref, buf, w1b, w3b, w2b, sem, wsem, *, layer):
    i = pl.program_id(0)
    n_used = nu_ref[0]
    slot = lax.rem(i, 2)

    def gather(tile, s):
        base = tile * MOE_TM

        def body(r, c):
            _row_copy(x_hbm, tok_ref[base + r], buf.at[s], r, sem.at[s]).start()
            return c

        lax.fori_loop(0, MOE_TM, body, 0, unroll=ROW_DMA_UNROLL)

    def weight_copies(e, s):
        return [pltpu.make_async_copy(src.at[layer, e], dst.at[s], wsem.at[s])
                for src, dst in ((w1_hbm, w1b), (w3_hbm, w3b), (w2_hbm, w2b))]

    @pl.when(i == 0)
    def _():
        gather(0, 0)
        for cp in weight_copies(te_ref[0], 0):
            cp.start(priority=BULK_DMA_PRIORITY)

    @pl.when(i < n_used)
    def _():
        ws = ws_ref[i]

        @pl.when(first_ref[i] == 1)
        def _():
            for cp in weight_copies(te_ref[i], ws):
                cp.wait()

            @pl.when(nxt_ref[i] >= 0)
            def _():
                for cp in weight_copies(nxt_ref[i], 1 - ws):
                    cp.start(priority=BULK_DMA_PRIORITY)

        pltpu.make_async_copy(x_hbm.at[pl.ds(0, MOE_TM), :], buf.at[slot], sem.at[slot]).wait()

        @pl.when(i + 1 < n_used)
        def _():
            gather(i + 1, 1 - slot)

        xs = _unpack_halves(buf[slot]).astype(BF16)
        h1 = jnp.dot(xs, w1b[ws].astype(BF16), preferred_element_type=F32)
        h3 = jnp.dot(xs, w3b[ws].astype(BF16), preferred_element_type=F32)
        hid = (h1 * _sigmoid(h1)) * h3
        o_ref[...] = _pack_halves(jnp.dot(hid.astype(BF16), w2b[ws].astype(BF16), preferred_element_type=F32))

    @pl.when(i >= n_used)
    def _():
        o_ref[...] = jnp.zeros_like(o_ref)


def _experts(plan, x1p, w1, w3, w2, layer):
    r = plan[0].shape[0]
    d = 2 * x1p.shape[1]
    de = w1.shape[-1]
    n_pref = len(plan)
    return pl.pallas_call(
        functools.partial(_expert_kernel, layer=layer),
        grid_spec=pltpu.PrefetchScalarGridSpec(
            num_scalar_prefetch=n_pref,
            grid=(r // MOE_TM,),
            in_specs=[pl.BlockSpec(memory_space=pl.ANY)] * 4,
            out_specs=pl.BlockSpec((MOE_TM, d // 2), lambda i, *_: (i, 0)),
            scratch_shapes=[pltpu.VMEM((2, MOE_TM, d // 2), I32),
                            pltpu.VMEM((2, d, de), F32), pltpu.VMEM((2, d, de), F32),
                            pltpu.VMEM((2, de, d), F32),
                            pltpu.SemaphoreType.DMA((2,)), pltpu.SemaphoreType.DMA((2,))],
        ),
        out_shape=jax.ShapeDtypeStruct((r, d // 2), I32),
        compiler_params=_cparams("arbitrary"),
        name="moe_experts",
    )(*plan, x1p, w1, w3, w2)


def _combine_kernel(pos_ref, x1_ref, route_ref, g_ref, b_ref, ys_hbm, o_ref, buf, sem, *, alpha):
    i = pl.program_id(0)
    slot = lax.rem(i, 2)

    def gather(tile, s):
        base = tile * MOE_TM

        def body(r, c):
            for k in range(2):
                _row_copy(ys_hbm, pos_ref[2 * (base + r) + k], buf.at[s, k], r, sem.at[s]).start()
            return c

        lax.fori_loop(0, MOE_TM, body, 0, unroll=ROW_DMA_UNROLL // 2)

    @pl.when(i == 0)
    def _():
        gather(0, 0)

    for k in range(2):
        pltpu.make_async_copy(ys_hbm.at[pl.ds(0, MOE_TM), :], buf.at[slot, k], sem.at[slot]).wait()

    @pl.when(i + 1 < pl.num_programs(0))
    def _():
        gather(i + 1, 1 - slot)

    route = route_ref[...]
    y = alpha * x1_ref[...] + (route[:, 0:1] * _unpack_halves(buf[slot, 0])
                               + route[:, 1:2] * _unpack_halves(buf[slot, 1]))
    o_ref[...] = _layer_norm(y, g_ref[...], b_ref[...])


def _combine(pos, x1, route, ln_g, ln_b, ys, *, alpha):
    t, d = x1.shape
    return pl.pallas_call(
        functools.partial(_combine_kernel, alpha=alpha),
        grid_spec=pltpu.PrefetchScalarGridSpec(
            num_scalar_prefetch=1,
            grid=(t // MOE_TM,),
            in_specs=[pl.BlockSpec((MOE_TM, d), lambda i, pos: (i, 0)),
                      pl.BlockSpec((MOE_TM, ROUTE_LANES), lambda i, pos: (i, 0)),
                      pl.BlockSpec((1, d), lambda i, pos: (0, 0)),
                      pl.BlockSpec((1, d), lambda i, pos: (0, 0)),
                      pl.BlockSpec(memory_space=pl.ANY)],
            out_specs=pl.BlockSpec((MOE_TM, d), lambda i, pos: (i, 0)),
            scratch_shapes=[pltpu.VMEM((2, 2, MOE_TM, d // 2), I32), pltpu.SemaphoreType.DMA((2,))],
        ),
        out_shape=jax.ShapeDtypeStruct((t, d), F32),
        compiler_params=_cparams("arbitrary"),
        name="moe_combine",
    )(pos, x1, route, ln_g.reshape(1, d), ln_b.reshape(1, d), ys)


def _moe_plan(route, t):
    flat_e = route[:, 2:4].astype(I32).reshape(-1)
    n = flat_e.shape[0]
    n_rows = n + N_EXPERTS * MOE_TM
    n_tiles = n_rows // MOE_TM
    onehot = (flat_e[:, None] == jnp.arange(N_EXPERTS, dtype=I32)[None, :]).astype(I32)
    running = jnp.cumsum(onehot, axis=0)
    rank = jnp.sum(onehot * running, axis=1) - 1
    counts = running[-1]
    tiles_per = (counts + MOE_TM - 1) // MOE_TM
    tile_end = jnp.cumsum(tiles_per)
    grp_row0 = (tile_end - tiles_per) * MOE_TM
    pos = jnp.sum(onehot * grp_row0[None, :], axis=1) + rank
    row_token = jnp.zeros((n_rows,), I32).at[pos].set(jnp.arange(n, dtype=I32) // 2, unique_indices=True)
    n_used = tile_end[-1]
    tile_ids = jnp.minimum(jnp.arange(n_tiles, dtype=I32), n_used - 1)
    tile_expert = jnp.sum((tile_end[None, :] <= tile_ids[:, None]).astype(I32), axis=1)
    tile_expert = jnp.minimum(tile_expert, N_EXPERTS - 1)
    first = jnp.concatenate([jnp.ones((1,), I32), (tile_expert[1:] != tile_expert[:-1]).astype(I32)])
    w_slot = (jnp.cumsum(first) - 1) % 2
    e_ids = jnp.arange(N_EXPERTS, dtype=I32)
    later = jnp.logical_and(e_ids[None, :] > e_ids[:, None], tiles_per[None, :] > 0)
    next_e = jnp.min(jnp.where(later, e_ids[None, :], N_EXPERTS), axis=1)
    next_e = jnp.where(next_e == N_EXPERTS, -1, next_e)
    nxt = jnp.sum((tile_expert[:, None] == e_ids[None, :]).astype(I32) * next_e[None, :], axis=1)
    plan = (row_token, tile_expert, first, nxt.astype(I32), w_slot.astype(I32), n_used.reshape(1).astype(I32))
    return plan, pos


def kernel(x, mem, w_mem_kv, w_in, g_cq, g_ckv, g_kidx, b_kidx, w_uq, w_uqi, w_ukv,
           w_up_a, w_up_b, w_up_c, w_gate, b_gate, w_o, ln1_g, ln1_b,
           w_grp, b_grp, w_rt, b_rt, w1, w3, w2, ln2_g, ln2_b):
    bsz, t, d = x.shape
    assert bsz == 1 and t % TQ == 0
    depth = w_in.shape[0]
    alpha = (2 * depth) ** 0.25
    scale = HEAD_DIM ** -0.5 * LOG2E
    top_k = min(TOPK_A_MAX, t // 4)
    n_mem = mem.shape[1]
    tm = min(t, 512)

    slopes = LOG2E * 2.0 ** (-8.0 * jnp.arange(1, N_ALIBI + 1, dtype=F32) / N_ALIBI)
    slopes_a, slopes_b = slopes[0::2], slopes[1::2]

    mkv = _mm(mem[0], w_mem_kv.astype(BF16), out_dtype=BF16, tm=n_mem, tn=W_C)
    mk = mkv[:, :W_C]
    mvt = mkv[:, W_C:].T

    xl = x[0]
    for l in range(depth):
        wl = w_in[l]
        o = 0
        parts = []
        for width in (Q_RANK_A, KV_RANK_A, IDX_DIM, N_IDX_HEADS, W_B, W_B, W_B, W_C):
            parts.append(wl[:, o:o + width])
            o += width
        w_cq, w_ckv, w_ki, w_wi, w_qb, w_kb, w_vb, w_qc = parts
        n_small = Q_RANK_A + KV_RANK_A + IDX_DIM + N_IDX_HEADS
        pad = (-n_small) % LANES
        w_p1 = jnp.concatenate([w_cq, w_ckv, w_ki, w_wi * N_IDX_HEADS ** -0.5,
                                jnp.zeros((d, pad), F32)], axis=1).astype(BF16)
        w_p2 = jnp.concatenate([w_qb * scale, w_kb, w_vb, w_qc * scale], axis=1).astype(BF16)
        p1 = _mm(xl, w_p1, out_dtype=F32, tm=tm, tn=w_p1.shape[1])
        p2 = _mm(xl, w_p2, out_dtype=BF16, tm=tm, tn=w_p2.shape[1] // 2)
        assert Q_RANK_A % KV_RANK_A == 0 and (Q_RANK_A + KV_RANK_A) % LANES == 0

        w_q = jnp.concatenate([w_uq[l] * scale, w_uqi[l] * IDX_DIM ** -0.5], axis=1).astype(BF16)
        qq = _mm(p1, w_q, out_dtype=BF16, tm=tm, tn=w_q.shape[1], norm_g=g_cq[l])
        kv = _mm(p1, w_ukv[l].astype(BF16), out_dtype=BF16, tm=tm, tn=2 * W_A, norm_g=g_ckv[l],
                 x_col_block=Q_RANK_A // KV_RANK_A)
        kidx, wi_t = _kidx_norm(p1, (Q_RANK_A + KV_RANK_A) // LANES, g_kidx[l], b_kidx[l])
        qi_r = (qq[:, W_A:].reshape(t // TQ, TQ, N_IDX_HEADS, IDX_DIM).transpose(0, 2, 1, 3)
                .reshape(t // TQ, N_IDX_HEADS * TQ, IDX_DIM))
        o_a = _dsa(slopes_a, qi_r, wi_t, kidx, qq, kv, kv[:, W_A:].T, top_k)
        o_b = _moba(slopes_b, p2, p2[:, 2 * W_B:3 * W_B].T)
        o_c = _mem_attn(p2[:, 3 * W_B:], mk, mvt)

        merged = _merge(xl, w_gate[l].astype(BF16), b_gate[l], o_a, o_b, o_c,
                        w_up_a[l].astype(BF16), w_up_b[l].astype(BF16), w_up_c[l].astype(BF16),
                        tm=min(t, 1024), tn=512)

        n_route = N_GROUPS + N_EXPERTS
        w_r = jnp.concatenate([w_grp[l], w_rt[l]], axis=1)
        w_r_hi = w_r.astype(BF16)
        w_r_lo = (w_r - w_r_hi.astype(F32)).astype(BF16)
        lane_pad = jnp.zeros((d, ROUTE_LANES // 2 - n_route), BF16)
        w_r2 = jnp.concatenate([w_r_hi, lane_pad, w_r_lo, lane_pad], axis=1)
        b_r = jnp.concatenate([b_grp[l], b_rt[l],
                               jnp.zeros((ROUTE_LANES - n_route,), F32)]).reshape(1, ROUTE_LANES)
        x1, x1p, route = _oproj(xl, merged, w_o[l].astype(BF16), ln1_g[l], ln1_b[l], w_r2, b_r,
                           alpha=alpha, tm=tm)

        plan, pos = _moe_plan(route, t)
        ys = _experts(plan, x1p, w1, w3, w2, l)
        xl = _combine(pos, x1, route, ln2_g[l], ln2_b[l], ys, alpha=alpha)
    return xl[None]
```

```python
import functools

import jax
import jax.numpy as jnp
from jax import lax
from jax.experimental import pallas as pl
from jax.experimental.pallas import tpu as pltpu

HEAD_DIM = 128
N_HEADS_A = 6
N_HEADS_B = 6
N_HEADS_C = 4
W_A = N_HEADS_A * HEAD_DIM
W_B = N_HEADS_B * HEAD_DIM
W_C = N_HEADS_C * HEAD_DIM
Q_RANK_A = 512
KV_RANK_A = 256
N_IDX_HEADS = 16
IDX_DIM = 64
TOPK_A_MAX = 256
MOBA_BLOCK = 256
MOBA_TOPK = 3
N_ALIBI = N_HEADS_A + N_HEADS_B
N_GROUPS = 4
EXPERTS_PER_GROUP = 8
N_EXPERTS = N_GROUPS * EXPERTS_PER_GROUP
D_EXPERT = 512
NORM_EPS = 1e-5
NEG = -1e30
LOG2E = 1.4426950408889634
I16_MIN = -(2 ** 15)

LANES = 128
SUBLANES = 8
PACKED_ROWS = 2 * SUBLANES
TQ = 256
CK = 256
ROUTE_LANES = 128
MOE_TM = 256
ROW_DMA_UNROLL = 8
BULK_DMA_PRIORITY = 1
WEIGHT_DMA_SPLIT = 4
VMEM_LIMIT = 56 * 1024 * 1024

F32 = jnp.float32
BF16 = jnp.bfloat16
I32 = jnp.int32
I16 = jnp.int16

_NT = (((1,), (1,)), ((), ()))


def _cparams(*sem):
    return pltpu.CompilerParams(dimension_semantics=sem, vmem_limit_bytes=VMEM_LIMIT)


def _resident(shape, col_block=0):
    index = (0,) * (len(shape) - 1) + (col_block,)
    return pl.BlockSpec(shape, lambda *_: index, pipeline_mode=pl.Buffered(1))


def _mm_kernel(*refs, has_norm, has_bias):
    x_ref, w_ref = refs[0], refs[1]
    k = 2
    x = x_ref[...]
    if has_norm:
        g_ref = refs[k]
        k += 1
        xf = x.astype(F32)
        x = xf * lax.rsqrt(jnp.mean(xf * xf, axis=-1, keepdims=True) + NORM_EPS) * g_ref[...]
    acc = jnp.dot(x.astype(BF16), w_ref[...], preferred_element_type=F32)
    if has_bias:
        acc = acc + refs[k][...]
        k += 1
    o_ref = refs[k]
    o_ref[...] = acc.astype(o_ref.dtype)


def _mm(x, w, *, out_dtype, tm, tn, norm_g=None, bias=None, x_col_block=0):
    m = x.shape[0]
    kdim, n = w.shape
    assert m % tm == 0 and n % tn == 0
    in_specs = [pl.BlockSpec((tm, kdim), lambda i, j: (i, x_col_block)),
                pl.BlockSpec((kdim, tn), lambda i, j: (0, j))]
    args = [x, w]
    if norm_g is not None:
        in_specs.append(pl.BlockSpec((1, kdim), lambda i, j: (0, 0)))
        args.append(norm_g.reshape(1, kdim).astype(F32))
    if bias is not None:
        in_specs.append(pl.BlockSpec((1, tn), lambda i, j: (0, j)))
        args.append(bias.reshape(1, n).astype(F32))
    return pl.pallas_call(
        functools.partial(_mm_kernel, has_norm=norm_g is not None, has_bias=bias is not None),
        grid=(m // tm, n // tn),
        in_specs=in_specs,
        out_specs=pl.BlockSpec((tm, tn), lambda i, j: (i, j)),
        out_shape=jax.ShapeDtypeStruct((m, n), out_dtype),
        compiler_params=_cparams("parallel", "parallel"),
        name="mm",
    )(*args)


def _kidx_kernel(x_ref, g_ref, b_ref, k_ref, w_ref):
    blk = x_ref[...]
    x = blk[:, :IDX_DIM]
    mu = jnp.mean(x, axis=-1, keepdims=True)
    xc = x - mu
    var = jnp.mean(xc * xc, axis=-1, keepdims=True)
    k_ref[...] = (xc * lax.rsqrt(var + NORM_EPS) * g_ref[...] + b_ref[...]).astype(k_ref.dtype)
    w_ref[...] = blk.T[IDX_DIM:IDX_DIM + N_IDX_HEADS, :]


def _kidx_norm(p1, col_block, g, b):
    t = p1.shape[0]
    tm = min(t, 1024)
    return pl.pallas_call(
        _kidx_kernel,
        grid=(t // tm,),
        in_specs=[pl.BlockSpec((tm, LANES), lambda i: (i, col_block)),
                  pl.BlockSpec((1, IDX_DIM), lambda i: (0, 0)),
                  pl.BlockSpec((1, IDX_DIM), lambda i: (0, 0))],
        out_specs=[pl.BlockSpec((tm, IDX_DIM), lambda i: (i, 0)),
                   pl.BlockSpec((N_IDX_HEADS, tm), lambda i: (0, i))],
        out_shape=[jax.ShapeDtypeStruct((t, IDX_DIM), BF16),
                   jax.ShapeDtypeStruct((N_IDX_HEADS, t), F32)],
        compiler_params=_cparams("parallel"),
        name="kidx_norm",
    )(p1, g.reshape(1, IDX_DIM), b.reshape(1, IDX_DIM))


def _head(h):
    return slice(h * HEAD_DIM, (h + 1) * HEAD_DIM)


def _attn_state(n_heads):
    return ([pltpu.VMEM((CK, TQ), F32) for _ in range(2 * n_heads)]
            + [pltpu.VMEM((HEAD_DIM, TQ), F32) for _ in range(n_heads)]
            + [pltpu.VMEM((1, TQ), F32) for _ in range(2 * n_heads)])


def _attn_reset(state, n_heads):
    state = state[2 * n_heads:]
    for h in range(n_heads):
        state[h][...] = jnp.zeros((HEAD_DIM, TQ), F32)
        state[n_heads + h][...] = jnp.full((1, TQ), NEG, F32)
        state[2 * n_heads + h][...] = jnp.zeros((1, TQ), F32)


def _attn_run(state, n_heads, i, q_ref, k_ref, consume):
    def scores(j, slot):
        start = pl.multiple_of(j * CK, CK)
        for h in range(n_heads):
            state[slot * n_heads + h][...] = lax.dot_general(
                k_ref[pl.ds(start, CK), _head(h)], q_ref[:, _head(h)], _NT,
                preferred_element_type=F32)

    scores(0, 0)

    def pair(p, c):
        j = 2 * p
        scores(j + 1, 1)
        consume(j, 0, False)
        scores(j + 2, 0)
        consume(j + 1, 1, False)
        return c

    lax.fori_loop(0, i // 2, pair, 0)

    @pl.when(i % 2 == 0)
    def _():
        consume(i, 0, True)

    @pl.when(i % 2 == 1)
    def _():
        scores(i, 1)
        consume(i - 1, 0, False)
        consume(i, 1, True)


def _attn_update(state, n_heads, h, slot, vt_blk, t_bias, c):
    s_ref, state = state[slot * n_heads + h], state[2 * n_heads:]
    acc_ref, m_ref, l_ref = state[h], state[n_heads + h], state[2 * n_heads + h]
    t = s_ref[...] + t_bias
    m_old = m_ref[...]
    m_new = jnp.maximum(m_old, jnp.max(t, axis=0, keepdims=True) + c)
    alpha = jnp.exp2(m_old - m_new)
    p = jnp.exp2(t - (m_new - c))
    l_ref[...] = alpha * l_ref[...] + jnp.sum(p, axis=0, keepdims=True)
    acc_ref[...] = alpha * acc_ref[...] + jnp.dot(vt_blk, p.astype(BF16), preferred_element_type=F32)
    m_ref[...] = m_new


def _attn_finish(state, n_heads, h):
    state = state[2 * n_heads:]
    return (state[h][...] * (1.0 / state[2 * n_heads + h][...])).T


def _key_offsets():
    return lax.broadcasted_iota(I32, (CK, TQ), 0), lax.broadcasted_iota(I32, (CK, TQ), 1)


def _sortable(x):
    b = pltpu.bitcast(x, I32)
    return b ^ ((b >> 31) & 0x7FFFFFFF)


def _dsa_kernel(slopes_ref, qi_ref, wi_ref, kidx_ref, qa_ref, ka_ref, vat_ref, o_ref,
                keys_ref, half_ref, sb_ref, *state, top_k, n_idx_bits):
    i = pl.program_id(0)
    rows, cols = _key_offsets()
    causal = rows <= cols
    nh = N_HEADS_A

    @pl.when(i == 0)
    def _():
        rows_f = rows.astype(F32)
        for h in range(nh):
            sb_ref[h] = slopes_ref[h] * rows_f

    def score_chunk(j, diagonal):
        start = pl.multiple_of(j * CK, CK)
        kc = kidx_ref[pl.ds(start, CK), :]
        acc = jnp.zeros((CK, TQ), F32)
        for h in range(N_IDX_HEADS):
            z = lax.dot_general(kc, qi_ref[0, h * TQ:(h + 1) * TQ, :], _NT,
                                preferred_element_type=F32)
            acc = acc + wi_ref[h:h + 1, :] * jnp.maximum(z, 0.0)
        if diagonal:
            acc = jnp.where(causal, acc, NEG)
        key = _sortable(acc)
        keys_ref[pl.ds(start, CK), :] = key
        half_ref[pl.ds(start, CK), :] = (key >> 16).astype(I16)

    def score_body(j, c):
        score_chunk(j, False)
        return c

    lax.fori_loop(0, i, score_body, 0)
    score_chunk(i, True)
    half_ref[pl.ds(pl.multiple_of((i + 1) * CK, CK), CK), :] = jnp.full((CK, TQ), I16_MIN, I16)

    def count16(cand, strict):
        cand = cand.astype(I16)

        def body(p, cnt):
            blk = half_ref[pl.ds(pl.multiple_of(p * (2 * CK), 2 * CK), 2 * CK), :]
            hit = jnp.where((blk > cand) if strict else (blk >= cand), jnp.int16(1), jnp.int16(0))
            for r in range(0, 2 * CK, PACKED_ROWS):
                cnt = cnt + hit[r:r + PACKED_ROWS, :]
            return cnt

        cnt = lax.fori_loop(0, (i + 2) // 2, body, jnp.zeros((PACKED_ROWS, TQ), I16))
        return jnp.sum(cnt.astype(I32), axis=0, keepdims=True)

    def search16(target, n_all):
        def bit_body(b, carry):
            thr, n_thr = carry
            cand = thr + jnp.left_shift(jnp.int32(1), 15 - b)
            n_cand = count16(cand, False)
            ok = n_cand >= target
            return jnp.where(ok, cand, thr), jnp.where(ok, n_cand, n_thr)
        return lax.fori_loop(0, 16, bit_body, (jnp.full((1, TQ), I16_MIN, I32), n_all))

    t_hi, n_ge_hi = search16(top_k, jnp.zeros((1, TQ), I32) + (i + 1) * CK)
    n_gt_hi = count16(t_hi, True)
    need = top_k - n_gt_hi

    def low_half(j, c):
        start = pl.multiple_of(j * CK, CK)
        key = keys_ref[pl.ds(start, CK), :]
        low = ((key ^ 0x8000) << 16) >> 16
        half_ref[pl.ds(start, CK), :] = jnp.where((key >> 16) == t_hi, low, I16_MIN).astype(I16)
        return c

    lax.fori_loop(0, i + 1, low_half, 0)
    t_lo, n_ge_lo = search16(need, n_ge_hi - n_gt_hi)
    thr = (t_hi << 16) + (t_lo - I16_MIN)

    @pl.when(jnp.max(n_gt_hi + n_ge_lo) > top_k)
    def _():
        def count32(pred):
            def body(j, cnt):
                start = pl.multiple_of(j * CK, CK)
                hit = jnp.where(pred(keys_ref[pl.ds(start, CK), :], rows + j * CK), 1, 0)
                return cnt + jnp.sum(hit.reshape(CK // SUBLANES, SUBLANES, TQ), axis=0)
            cnt = lax.fori_loop(0, i + 1, body, jnp.zeros((SUBLANES, TQ), I32))
            return jnp.sum(cnt, axis=0, keepdims=True)

        keep = top_k - count32(lambda key, idx: key > thr)

        def idx_bit(b, last):
            cand = last + jnp.left_shift(jnp.int32(1), n_idx_bits - 1 - b)
            below = count32(lambda key, idx: jnp.logical_and(key == thr, idx < cand))
            return jnp.where(below < keep, cand, last)

        last = lax.fori_loop(0, n_idx_bits, idx_bit, jnp.zeros((1, TQ), I32))

        def drop(j, c):
            start = pl.multiple_of(j * CK, CK)
            key = keys_ref[pl.ds(start, CK), :]
            late_tie = jnp.logical_and(key == thr, rows + j * CK > last)
            keys_ref[pl.ds(start, CK), :] = jnp.where(late_tie, key - 1, key)
            return c

        lax.fori_loop(0, i + 1, drop, 0)

    _attn_reset(state, nh)

    def consume(j, slot, diagonal):
        start = pl.multiple_of(j * CK, CK)
        sel = keys_ref[pl.ds(start, CK), :] >= thr
        if diagonal:
            sel = jnp.logical_and(sel, causal)
        mask_bias = jnp.where(sel, 0.0, NEG)
        off = ((j - i) * CK).astype(F32)
        for h in range(nh):
            _attn_update(state, nh, h, slot, vat_ref[_head(h), pl.ds(start, CK)],
                         sb_ref[h] + mask_bias, slopes_ref[h] * off)

    _attn_run(state, nh, i, qa_ref, ka_ref, consume)
    for h in range(nh):
        o_ref[:, _head(h)] = _attn_finish(state, nh, h).astype(o_ref.dtype)


def _dsa(slopes, qi_r, wi_t, kidx, qa, ka, vat, top_k):
    w, t = vat.shape
    nh = w // HEAD_DIM
    return pl.pallas_call(
        functools.partial(_dsa_kernel, top_k=top_k, n_idx_bits=max(1, (t - 1).bit_length())),
        grid_spec=pltpu.PrefetchScalarGridSpec(
            num_scalar_prefetch=0,
            grid=(t // TQ,),
            in_specs=[
                pl.BlockSpec(memory_space=pltpu.SMEM),
                pl.BlockSpec((1, N_IDX_HEADS * TQ, IDX_DIM), lambda i: (i, 0, 0)),
                pl.BlockSpec((N_IDX_HEADS, TQ), lambda i: (0, i)),
                _resident(kidx.shape),
                pl.BlockSpec((TQ, w), lambda i: (i, 0)),
                _resident((t, w)),
                _resident(vat.shape),
            ],
            out_specs=pl.BlockSpec((TQ, w), lambda i: (i, 0)),
            scratch_shapes=[pltpu.VMEM((t, TQ), I32), pltpu.VMEM((t + CK, TQ), I16),
                            pltpu.VMEM((nh, CK, TQ), F32)] + _attn_state(nh),
        ),
        out_shape=jax.ShapeDtypeStruct((t, w), BF16),
        compiler_params=_cparams("arbitrary"),
        name="dsa",
    )(slopes, qi_r, wi_t, kidx, qa, ka, vat)


def _moba_kernel(slopes_ref, qb_ref, kb_ref, vbt_ref, o_ref, kmean_ref, sel_ref, sb_ref, *state, n_kb):
    i = pl.program_id(0)
    rows, cols = _key_offsets()
    nh = N_HEADS_B

    @pl.when(i == 0)
    def _():
        rows_f = rows.astype(F32)
        for h in range(nh):
            sb_ref[h] = slopes_ref[h] * rows_f
            for n in range(n_kb):
                blk = kb_ref[n * CK:(n + 1) * CK, _head(h)].astype(F32)
                kmean_ref[h, n:n + 1, :] = jnp.mean(blk, axis=0, keepdims=True)

    blk_id = lax.broadcasted_iota(I32, (n_kb, TQ), 0)
    for h in range(nh):
        q_h = qb_ref[:, _head(h)]
        km = kmean_ref[h]
        km_hi = km.astype(BF16)
        km_lo = (km - km_hi.astype(F32)).astype(BF16)
        gate = (lax.dot_general(km_hi, q_h, _NT, preferred_element_type=F32)
                + lax.dot_general(km_lo, q_h, _NT, preferred_element_type=F32))
        gate = jnp.where(blk_id < i, gate, NEG)
        sel = jnp.full((n_kb, TQ), NEG, F32)
        for _ in range(MOBA_TOPK):
            best = jnp.max(gate, axis=0, keepdims=True)
            first = jnp.min(jnp.where(gate == best, blk_id, n_kb), axis=0, keepdims=True)
            pick = blk_id == first
            sel = jnp.where(pick, 0.0, sel)
            gate = jnp.where(pick, -jnp.inf, gate)
        sel_ref[h] = jnp.where(blk_id < i, sel, NEG)

    _attn_reset(state, nh)

    def consume(j, slot, diagonal):
        start = pl.multiple_of(j * CK, CK)
        off = ((j - i) * CK).astype(F32)
        for h in range(nh):
            if diagonal:
                t_bias = jnp.where(rows <= cols, sb_ref[h], NEG)
            else:
                t_bias = sb_ref[h] + sel_ref[h, pl.ds(j, 1), :]
            _attn_update(state, nh, h, slot, vbt_ref[_head(h), pl.ds(start, CK)],
                         t_bias, slopes_ref[h] * off)

    _attn_run(state, nh, i, qb_ref, kb_ref, consume)
    for h in range(nh):
        o_ref[:, _head(h)] = _attn_finish(state, nh, h).astype(o_ref.dtype)


def _moba(slopes, qkv, vbt):
    w, t = vbt.shape
    nh = w // HEAD_DIM
    assert t % MOBA_BLOCK == 0 and TQ == MOBA_BLOCK and CK == MOBA_BLOCK
    n_kb = t // MOBA_BLOCK
    return pl.pallas_call(
        functools.partial(_moba_kernel, n_kb=n_kb),
        grid_spec=pltpu.PrefetchScalarGridSpec(
            num_scalar_prefetch=0,
            grid=(t // TQ,),
            in_specs=[
                pl.BlockSpec(memory_space=pltpu.SMEM),
                pl.BlockSpec((TQ, w), lambda i: (i, 0)),
                _resident((t, w), col_block=1),
                _resident(vbt.shape),
            ],
            out_specs=pl.BlockSpec((TQ, w), lambda i: (i, 0)),
            scratch_shapes=[pltpu.VMEM((nh, n_kb, HEAD_DIM), F32), pltpu.VMEM((nh, n_kb, TQ), F32),
                            pltpu.VMEM((nh, CK, TQ), F32)] + _attn_state(nh),
        ),
        out_shape=jax.ShapeDtypeStruct((t, w), BF16),
        compiler_params=_cparams("arbitrary"),
        name="moba",
    )(slopes, qkv, qkv, vbt)


def _mem_kernel(qc_ref, mk_ref, mvt_ref, o_ref):
    for h in range(N_HEADS_C):
        s = lax.dot_general(mk_ref[:, _head(h)], qc_ref[:, _head(h)], _NT,
                            preferred_element_type=F32)
        m = jnp.max(s, axis=0, keepdims=True)
        p = jnp.exp2(s - m)
        l = jnp.sum(p, axis=0, keepdims=True)
        acc = jnp.dot(mvt_ref[_head(h), :], p.astype(BF16), preferred_element_type=F32)
        o_ref[:, _head(h)] = (acc * (1.0 / l)).T.astype(o_ref.dtype)


def _mem_attn(qc, mk, mvt):
    t, w = qc.shape
    return pl.pallas_call(
        _mem_kernel,
        grid=(t // TQ,),
        in_specs=[pl.BlockSpec((TQ, w), lambda i: (i, 0)),
                  _resident(mk.shape), _resident(mvt.shape)],
        out_specs=pl.BlockSpec((TQ, w), lambda i: (i, 0)),
        out_shape=jax.ShapeDtypeStruct((t, w), BF16),
        compiler_params=_cparams("parallel"),
        name="mem_attn",
    )(qc, mk, mvt)


def _sigmoid(x):
    return 1.0 / (1.0 + jnp.exp(-x))


def _merge_kernel(x_ref, wga_ref, wgb_ref, wgc_ref, bga_ref, bgb_ref, bgc_ref,
                  oa_ref, ob_ref, oc_ref, wua_ref, wub_ref, wuc_ref, o_ref):
    xb = x_ref[...].astype(BF16)

    def branch(wg_ref, bg_ref, oo_ref, wu_ref):
        g = _sigmoid(jnp.dot(xb, wg_ref[...], preferred_element_type=F32) + bg_ref[...])
        return g * jnp.dot(oo_ref[...], wu_ref[...], preferred_element_type=F32)

    acc = branch(wga_ref, bga_ref, oa_ref, wua_ref)
    acc = acc + branch(wgb_ref, bgb_ref, ob_ref, wub_ref)
    acc = acc + branch(wgc_ref, bgc_ref, oc_ref, wuc_ref)
    o_ref[...] = acc.astype(o_ref.dtype)


def _merge(x, w_gate, b_gate, o_a, o_b, o_c, w_up_a, w_up_b, w_up_c, *, tm, tn):
    t, d = x.shape
    nb = d // tn
    b_gate = b_gate.reshape(1, 3 * d)

    def wg(k):
        return pl.BlockSpec((d, tn), lambda i, j, k=k: (0, j + k * nb))

    def bg(k):
        return pl.BlockSpec((1, tn), lambda i, j, k=k: (0, j + k * nb))

    def act(w):
        return pl.BlockSpec((tm, w), lambda i, j: (i, 0))

    def wu(w):
        return pl.BlockSpec((w, tn), lambda i, j: (0, j))

    return pl.pallas_call(
        _merge_kernel,
        grid=(t // tm, nb),
        in_specs=[pl.BlockSpec((tm, d), lambda i, j: (i, 0)),
                  wg(0), wg(1), wg(2), bg(0), bg(1), bg(2),
                  act(W_A), act(W_B), act(W_C), wu(W_A), wu(W_B), wu(W_C)],
        out_specs=pl.BlockSpec((tm, tn), lambda i, j: (i, j)),
        out_shape=jax.ShapeDtypeStruct((t, d), BF16),
        compiler_params=_cparams("parallel", "parallel"),
        name="merge",
    )(x, w_gate, w_gate, w_gate, b_gate, b_gate, b_gate, o_a, o_b, o_c, w_up_a, w_up_b, w_up_c)


def _layer_norm(y, g, b):
    mu = jnp.mean(y, axis=-1, keepdims=True)
    yc = y - mu
    var = jnp.mean(yc * yc, axis=-1, keepdims=True)
    return yc * lax.rsqrt(var + NORM_EPS) * g + b


def _route(logits):
    lane = lax.broadcasted_iota(I32, logits.shape, 1)
    is_g = lane < N_GROUPS
    gl = jnp.where(is_g, logits, -jnp.inf)
    gmax = jnp.max(gl, axis=-1, keepdims=True)
    g_sel = jnp.min(jnp.where(gl == gmax, lane, ROUTE_LANES), axis=-1, keepdims=True)
    p_g = 1.0 / jnp.sum(jnp.where(is_g, jnp.exp(gl - gmax), 0.0), axis=-1, keepdims=True)
    e_id = lane - N_GROUPS
    in_grp = jnp.logical_and(e_id >= g_sel * EXPERTS_PER_GROUP, e_id < (g_sel + 1) * EXPERTS_PER_GROUP)
    el = jnp.where(in_grp, logits, -jnp.inf)
    emax = jnp.max(el, axis=-1, keepdims=True)
    ex = jnp.where(in_grp, jnp.exp(el - emax), 0.0)
    pe = ex / jnp.sum(ex, axis=-1, keepdims=True)
    pe = jnp.where(in_grp, pe, -1.0)
    p1 = jnp.max(pe, axis=-1, keepdims=True)
    i1 = jnp.min(jnp.where(pe == p1, e_id, ROUTE_LANES), axis=-1, keepdims=True)
    pe2 = jnp.where(e_id == i1, -1.0, pe)
    p2 = jnp.max(pe2, axis=-1, keepdims=True)
    i2 = jnp.min(jnp.where(pe2 == p2, e_id, ROUTE_LANES), axis=-1, keepdims=True)
    denom = p1 + p2
    w1 = p_g * (p1 / denom)
    w2 = p_g * (p2 / denom)
    return jnp.where(lane == 0, w1,
                     jnp.where(lane == 1, w2,
                               jnp.where(lane == 2, i1.astype(F32),
                                         jnp.where(lane == 3, i2.astype(F32), 0.0))))


def _pack_halves(y):
    half = y.shape[1] // 2
    lo = pltpu.bitcast(y[:, :half].astype(BF16).astype(F32), I32)
    hi = pltpu.bitcast(y[:, half:].astype(BF16).astype(F32), I32)
    return (hi & -65536) | lax.shift_right_logical(lo, 16)


def _unpack_halves(w):
    lo = pltpu.bitcast(w << 16, F32)
    hi = pltpu.bitcast(w & -65536, F32)
    return jnp.concatenate([lo, hi], axis=1)


def _oproj_kernel(x_ref, m_ref, wo_ref, g_ref, b_ref, wr_ref, br_ref, x1_ref, xp_ref, r_ref, *, alpha):
    y = alpha * x_ref[...] + jnp.dot(m_ref[...], wo_ref[...], preferred_element_type=F32)
    x1 = _layer_norm(y, g_ref[...], b_ref[...])
    x1_ref[...] = x1
    xp_ref[...] = _pack_halves(x1)
    x_hi = x1.astype(BF16)
    x_lo = (x1 - x_hi.astype(F32)).astype(BF16)
    parts = (jnp.dot(x_hi, wr_ref[...], preferred_element_type=F32)
             + jnp.dot(x_lo, wr_ref[...], preferred_element_type=F32))
    logits = parts + pltpu.roll(parts, ROUTE_LANES // 2, axis=1) + br_ref[...]
    r_ref[...] = _route(logits)


def _oproj(x, merged, w_o, ln_g, ln_b, w_r, b_r, *, alpha, tm):
    t, d = x.shape
    return pl.pallas_call(
        functools.partial(_oproj_kernel, alpha=alpha),
        grid=(t // tm,),
        in_specs=[pl.BlockSpec((tm, d), lambda i: (i, 0)),
                  pl.BlockSpec((tm, d), lambda i: (i, 0)),
                  _resident(w_o.shape),
                  pl.BlockSpec((1, d), lambda i: (0, 0)),
                  pl.BlockSpec((1, d), lambda i: (0, 0)),
                  _resident(w_r.shape),
                  pl.BlockSpec((1, ROUTE_LANES), lambda i: (0, 0))],
        out_specs=[pl.BlockSpec((tm, d), lambda i: (i, 0)),
                   pl.BlockSpec((tm, d // 2), lambda i: (i, 0)),
                   pl.BlockSpec((tm, ROUTE_LANES), lambda i: (i, 0))],
        out_shape=[jax.ShapeDtypeStruct((t, d), F32),
                   jax.ShapeDtypeStruct((t, d // 2), I32),
                   jax.ShapeDtypeStruct((t, ROUTE_LANES), F32)],
        compiler_params=_cparams("parallel"),
        name="oproj",
    )(x, merged, w_o, ln_g.reshape(1, d), ln_b.reshape(1, d), w_r, b_r)


def _row_copy(src_hbm, row, dst, k, sem):
    return pltpu.make_async_copy(src_hbm.at[pl.ds(row, 1), :], dst.at[pl.ds(k, 1), :], sem)


def _expert_kernel(tok_ref, te_ref, first_ref, nxt_ref, ws_ref, nu_ref, x_hbm, w1_hbm, w3_hbm, w2_hbm,
                   o_ref, buf, w1b, w3b, w2b, sem, wsem, *, layer):
    i = pl.program_id(0)
    n_used = nu_ref[0]
    slot = lax.rem(i, 2)

    def gather(tile, s):
        base = tile * MOE_TM

        def body(r, c):
            _row_copy(x_hbm, tok_ref[base + r], buf.at[s], r, sem.at[s]).start()
            return c

        lax.fori_loop(0, MOE_TM, body, 0, unroll=ROW_DMA_UNROLL)

    def weight_copies(e, s):
        copies = []
        for src, dst in ((w1_hbm, w1b), (w3_hbm, w3b), (w2_hbm, w2b)):
            band = dst.shape[1] // WEIGHT_DMA_SPLIT
            for k in range(WEIGHT_DMA_SPLIT):
                rows = pl.ds(k * band, band)
                copies.append(pltpu.make_async_copy(src.at[layer, e, rows], dst.at[s, rows], wsem.at[s]))
        return copies

    @pl.when(i == 0)
    def _():
        gather(0, 0)
        for cp in weight_copies(te_ref[0], 0):
            cp.start(priority=BULK_DMA_PRIORITY)

    @pl.when(i < n_used)
    def _():
        ws = ws_ref[i]

        @pl.when(first_ref[i] == 1)
        def _():
            for cp in weight_copies(te_ref[i], ws):
                cp.wait()

            @pl.when(nxt_ref[i] >= 0)
            def _():
                for cp in weight_copies(nxt_ref[i], 1 - ws):
                    cp.start(priority=BULK_DMA_PRIORITY)

        pltpu.make_async_copy(x_hbm.at[pl.ds(0, MOE_TM), :], buf.at[slot], sem.at[slot]).wait()

        @pl.when(i + 1 < n_used)
        def _():
            gather(i + 1, 1 - slot)

        xs = _unpack_halves(buf[slot]).astype(BF16)
        h1 = jnp.dot(xs, w1b[ws].astype(BF16), preferred_element_type=F32)
        h3 = jnp.dot(xs, w3b[ws].astype(BF16), preferred_element_type=F32)
        hid = (h1 * _sigmoid(h1)) * h3
        o_ref[...] = _pack_halves(jnp.dot(hid.astype(BF16), w2b[ws].astype(BF16), preferred_element_type=F32))

    @pl.when(i >= n_used)
    def _():
        o_ref[...] = jnp.zeros_like(o_ref)


def _experts(plan, x1p, w1, w3, w2, layer):
    r = plan[0].shape[0]
    d = 2 * x1p.shape[1]
    de = w1.shape[-1]
    n_pref = len(plan)
    return pl.pallas_call(
        functools.partial(_expert_kernel, layer=layer),
        grid_spec=pltpu.PrefetchScalarGridSpec(
            num_scalar_prefetch=n_pref,
            grid=(r // MOE_TM,),
            in_specs=[pl.BlockSpec(memory_space=pl.ANY)] * 4,
            out_specs=pl.BlockSpec((MOE_TM, d // 2), lambda i, *_: (i, 0)),
            scratch_shapes=[pltpu.VMEM((2, MOE_TM, d // 2), I32),
                            pltpu.VMEM((2, d, de), F32), pltpu.VMEM((2, d, de), F32),
                            pltpu.VMEM((2, de, d), F32),
                            pltpu.SemaphoreType.DMA((2,)), pltpu.SemaphoreType.DMA((2,))],
        ),
        out_shape=jax.ShapeDtypeStruct((r, d // 2), I32),
        compiler_params=_cparams("arbitrary"),
        name="moe_experts",
    )(*plan, x1p, w1, w3, w2)


def _combine_kernel(pos_ref, x1_ref, route_ref, g_ref, b_ref, ys_hbm, o_ref, buf, sem, *, alpha):
    i = pl.program_id(0)
    slot = lax.rem(i, 2)

    def gather(tile, s):
        base = tile * MOE_TM

        def body(r, c):
            for k in range(2):
                _row_copy(ys_hbm, pos_ref[2 * (base + r) + k], buf.at[s, k], r, sem.at[s]).start()
            return c

        lax.fori_loop(0, MOE_TM, body, 0, unroll=ROW_DMA_UNROLL // 2)

    @pl.when(i == 0)
    def _():
        gather(0, 0)

    for k in range(2):
        pltpu.make_async_copy(ys_hbm.at[pl.ds(0, MOE_TM), :], buf.at[slot, k], sem.at[slot]).wait()

    @pl.when(i + 1 < pl.num_programs(0))
    def _():
        gather(i + 1, 1 - slot)

    route = route_ref[...]
    y = alpha * x1_ref[...] + (route[:, 0:1] * _unpack_halves(buf[slot, 0])
                               + route[:, 1:2] * _unpack_halves(buf[slot, 1]))
    o_ref[...] = _layer_norm(y, g_ref[...], b_ref[...])


def _combine(pos, x1, route, ln_g, ln_b, ys, *, alpha):
    t, d = x1.shape
    return pl.pallas_call(
        functools.partial(_combine_kernel, alpha=alpha),
        grid_spec=pltpu.PrefetchScalarGridSpec(
            num_scalar_prefetch=1,
            grid=(t // MOE_TM,),
            in_specs=[pl.BlockSpec((MOE_TM, d), lambda i, pos: (i, 0)),
                      pl.BlockSpec((MOE_TM, ROUTE_LANES), lambda i, pos: (i, 0)),
                      pl.BlockSpec((1, d), lambda i, pos: (0, 0)),
                      pl.BlockSpec((1, d), lambda i, pos: (0, 0)),
                      pl.BlockSpec(memory_space=pl.ANY)],
            out_specs=pl.BlockSpec((MOE_TM, d), lambda i, pos: (i, 0)),
            scratch_shapes=[pltpu.VMEM((2, 2, MOE_TM, d // 2), I32), pltpu.SemaphoreType.DMA((2,))],
        ),
        out_shape=jax.ShapeDtypeStruct((t, d), F32),
        compiler_params=_cparams("arbitrary"),
        name="moe_combine",
    )(pos, x1, route, ln_g.reshape(1, d), ln_b.reshape(1, d), ys)


def _moe_plan(route, t):
    flat_e = route[:, 2:4].astype(I32).reshape(-1)
    n = flat_e.shape[0]
    n_rows = n + N_EXPERTS * MOE_TM
    n_tiles = n_rows // MOE_TM
    onehot = (flat_e[:, None] == jnp.arange(N_EXPERTS, dtype=I32)[None, :]).astype(I32)
    running = jnp.cumsum(onehot, axis=0)
    rank = jnp.sum(onehot * running, axis=1) - 1
    counts = running[-1]
    tiles_per = (counts + MOE_TM - 1) // MOE_TM
    tile_end = jnp.cumsum(tiles_per)
    grp_row0 = (tile_end - tiles_per) * MOE_TM
    pos = jnp.sum(onehot * grp_row0[None, :], axis=1) + rank
    row_token = jnp.zeros((n_rows,), I32).at[pos].set(jnp.arange(n, dtype=I32) // 2, unique_indices=True)
    n_used = tile_end[-1]
    tile_ids = jnp.minimum(jnp.arange(n_tiles, dtype=I32), n_used - 1)
    tile_expert = jnp.sum((tile_end[None, :] <= tile_ids[:, None]).astype(I32), axis=1)
    tile_expert = jnp.minimum(tile_expert, N_EXPERTS - 1)
    first = jnp.concatenate([jnp.ones((1,), I32), (tile_expert[1:] != tile_expert[:-1]).astype(I32)])
    w_slot = (jnp.cumsum(first) - 1) % 2
    e_ids = jnp.arange(N_EXPERTS, dtype=I32)
    later = jnp.logical_and(e_ids[None, :] > e_ids[:, None], tiles_per[None, :] > 0)
    next_e = jnp.min(jnp.where(later, e_ids[None, :], N_EXPERTS), axis=1)
    next_e = jnp.where(next_e == N_EXPERTS, -1, next_e)
    nxt = jnp.sum((tile_expert[:, None] == e_ids[None, :]).astype(I32) * next_e[None, :], axis=1)
    plan = (row_token, tile_expert, first, nxt.astype(I32), w_slot.astype(I32), n_used.reshape(1).astype(I32))
    return plan, pos


def kernel(x, mem, w_mem_kv, w_in, g_cq, g_ckv, g_kidx, b_kidx, w_uq, w_uqi, w_ukv,
           w_up_a, w_up_b, w_up_c, w_gate, b_gate, w_o, ln1_g, ln1_b,
           w_grp, b_grp, w_rt, b_rt, w1, w3, w2, ln2_g, ln2_b):
    bsz, t, d = x.shape
    assert bsz == 1 and t % TQ == 0
    depth = w_in.shape[0]
    alpha = (2 * depth) ** 0.25
    scale = HEAD_DIM ** -0.5 * LOG2E
    top_k = min(TOPK_A_MAX, t // 4)
    n_mem = mem.shape[1]
    tm = min(t, 512)

    slopes = LOG2E * 2.0 ** (-8.0 * jnp.arange(1, N_ALIBI + 1, dtype=F32) / N_ALIBI)
    slopes_a, slopes_b = slopes[0::2], slopes[1::2]

    mkv = _mm(mem[0], w_mem_kv.astype(BF16), out_dtype=BF16, tm=n_mem, tn=W_C)
    mk = mkv[:, :W_C]
    mvt = mkv[:, W_C:].T

    xl = x[0]
    for l in range(depth):
        wl = w_in[l]
        o = 0
        parts = []
        for width in (Q_RANK_A, KV_RANK_A, IDX_DIM, N_IDX_HEADS, W_B, W_B, W_B, W_C):
            parts.append(wl[:, o:o + width])
            o += width
        w_cq, w_ckv, w_ki, w_wi, w_qb, w_kb, w_vb, w_qc = parts
        n_small = Q_RANK_A + KV_RANK_A + IDX_DIM + N_IDX_HEADS
        pad = (-n_small) % LANES
        w_p1 = jnp.concatenate([w_cq, w_ckv, w_ki, w_wi * N_IDX_HEADS ** -0.5,
                                jnp.zeros((d, pad), F32)], axis=1).astype(BF16)
        w_p2 = jnp.concatenate([w_qb * scale, w_kb, w_vb, w_qc * scale], axis=1).astype(BF16)
        p1 = _mm(xl, w_p1, out_dtype=F32, tm=tm, tn=w_p1.shape[1])
        p2 = _mm(xl, w_p2, out_dtype=BF16, tm=tm, tn=w_p2.shape[1] // 2)
        assert Q_RANK_A % KV_RANK_A == 0 and (Q_RANK_A + KV_RANK_A) % LANES == 0

        w_q = jnp.concatenate([w_uq[l] * scale, w_uqi[l] * IDX_DIM ** -0.5], axis=1).astype(BF16)
        qq = _mm(p1, w_q, out_dtype=BF16, tm=tm, tn=w_q.shape[1], norm_g=g_cq[l])
        kv = _mm(p1, w_ukv[l].astype(BF16), out_dtype=BF16, tm=tm, tn=2 * W_A, norm_g=g_ckv[l],
                 x_col_block=Q_RANK_A // KV_RANK_A)
        kidx, wi_t = _kidx_norm(p1, (Q_RANK_A + KV_RANK_A) // LANES, g_kidx[l], b_kidx[l])
        qi_r = (qq[:, W_A:].reshape(t // TQ, TQ, N_IDX_HEADS, IDX_DIM).transpose(0, 2, 1, 3)
                .reshape(t // TQ, N_IDX_HEADS * TQ, IDX_DIM))
        o_a = _dsa(slopes_a, qi_r, wi_t, kidx, qq, kv, kv[:, W_A:].T, top_k)
        o_b = _moba(slopes_b, p2, p2[:, 2 * W_B:3 * W_B].T)
        o_c = _mem_attn(p2[:, 3 * W_B:], mk, mvt)

        merged = _merge(xl, w_gate[l].astype(BF16), b_gate[l], o_a, o_b, o_c,
                        w_up_a[l].astype(BF16), w_up_b[l].astype(BF16), w_up_c[l].astype(BF16),
                        tm=min(t, 1024), tn=512)

        n_route = N_GROUPS + N_EXPERTS
        w_r = jnp.concatenate([w_grp[l], w_rt[l]], axis=1)
        w_r_hi = w_r.astype(BF16)
        w_r_lo = (w_r - w_r_hi.astype(F32)).astype(BF16)
        lane_pad = jnp.zeros((d, ROUTE_LANES // 2 - n_route), BF16)
        w_r2 = jnp.concatenate([w_r_hi, lane_pad, w_r_lo, lane_pad], axis=1)
        b_r = jnp.concatenate([b_grp[l], b_rt[l],
                               jnp.zeros((ROUTE_LANES - n_route,), F32)]).reshape(1, ROUTE_LANES)
        x1, x1p, route = _oproj(xl, merged, w_o[l].astype(BF16), ln1_g[l], ln1_b[l], w_r2, b_r,
                           alpha=alpha, tm=tm)

        plan, pos = _moe_plan(route, t)
        ys = _experts(plan, x1p, w1, w3, w2, l)
        xl = _combine(pos, x1, route, ln2_g[l], ln2_b[l], ys, alpha=alpha)
    return xl[None]
```

```python
import functools

import jax
import jax.numpy as jnp
from jax import lax
from jax.experimental import pallas as pl
from jax.experimental.pallas import tpu as pltpu

HEAD_DIM = 128
N_HEADS_A = 6
N_HEADS_B = 6
N_HEADS_C = 4
W_A = N_HEADS_A * HEAD_DIM
W_B = N_HEADS_B * HEAD_DIM
W_C = N_HEADS_C * HEAD_DIM
Q_RANK_A = 512
KV_RANK_A = 256
N_IDX_HEADS = 16
IDX_DIM = 64
TOPK_A_MAX = 256
MOBA_BLOCK = 256
MOBA_TOPK = 3
N_ALIBI = N_HEADS_A + N_HEADS_B
N_GROUPS = 4
EXPERTS_PER_GROUP = 8
N_EXPERTS = N_GROUPS * EXPERTS_PER_GROUP
D_EXPERT = 512
NORM_EPS = 1e-5
NEG = -1e30
LOG2E = 1.4426950408889634
INT_MIN = -(2 ** 31)

LANES = 128
SUBLANES = 8
TQ = 256
CK = 256
ROUTE_LANES = 128
MOE_TM = 256
ROW_DMA_UNROLL = 8
BULK_DMA_PRIORITY = 1
VMEM_LIMIT = 56 * 1024 * 1024

F32 = jnp.float32
BF16 = jnp.bfloat16
I32 = jnp.int32

_NT = (((1,), (1,)), ((), ()))


def _cparams(*sem):
    return pltpu.CompilerParams(dimension_semantics=sem, vmem_limit_bytes=VMEM_LIMIT)


def _resident(shape, col_block=0):
    index = (0,) * (len(shape) - 1) + (col_block,)
    return pl.BlockSpec(shape, lambda *_: index, pipeline_mode=pl.Buffered(1))


def _mm_kernel(*refs, has_norm, has_bias):
    x_ref, w_ref = refs[0], refs[1]
    k = 2
    x = x_ref[...]
    if has_norm:
        g_ref = refs[k]
        k += 1
        xf = x.astype(F32)
        x = xf * lax.rsqrt(jnp.mean(xf * xf, axis=-1, keepdims=True) + NORM_EPS) * g_ref[...]
    acc = jnp.dot(x.astype(BF16), w_ref[...], preferred_element_type=F32)
    if has_bias:
        acc = acc + refs[k][...]
        k += 1
    o_ref = refs[k]
    o_ref[...] = acc.astype(o_ref.dtype)


def _mm(x, w, *, out_dtype, tm, tn, norm_g=None, bias=None, x_col_block=0):
    m = x.shape[0]
    kdim, n = w.shape
    assert m % tm == 0 and n % tn == 0
    in_specs = [pl.BlockSpec((tm, kdim), lambda i, j: (i, x_col_block)),
                pl.BlockSpec((kdim, tn), lambda i, j: (0, j))]
    args = [x, w]
    if norm_g is not None:
        in_specs.append(pl.BlockSpec((1, kdim), lambda i, j: (0, 0)))
        args.append(norm_g.reshape(1, kdim).astype(F32))
    if bias is not None:
        in_specs.append(pl.BlockSpec((1, tn), lambda i, j: (0, j)))
        args.append(bias.reshape(1, n).astype(F32))
    return pl.pallas_call(
        functools.partial(_mm_kernel, has_norm=norm_g is not None, has_bias=bias is not None),
        grid=(m // tm, n // tn),
        in_specs=in_specs,
        out_specs=pl.BlockSpec((tm, tn), lambda i, j: (i, j)),
        out_shape=jax.ShapeDtypeStruct((m, n), out_dtype),
        compiler_params=_cparams("parallel", "parallel"),
        name="mm",
    )(*args)


def _kidx_kernel(x_ref, g_ref, b_ref, k_ref, w_ref):
    blk = x_ref[...]
    x = blk[:, :IDX_DIM]
    mu = jnp.mean(x, axis=-1, keepdims=True)
    xc = x - mu
    var = jnp.mean(xc * xc, axis=-1, keepdims=True)
    k_ref[...] = (xc * lax.rsqrt(var + NORM_EPS) * g_ref[...] + b_ref[...]).astype(k_ref.dtype)
    w_ref[...] = blk.T[IDX_DIM:IDX_DIM + N_IDX_HEADS, :]


def _kidx_norm(p1, col_block, g, b):
    t = p1.shape[0]
    tm = min(t, 1024)
    return pl.pallas_call(
        _kidx_kernel,
        grid=(t // tm,),
        in_specs=[pl.BlockSpec((tm, LANES), lambda i: (i, col_block)),
                  pl.BlockSpec((1, IDX_DIM), lambda i: (0, 0)),
                  pl.BlockSpec((1, IDX_DIM), lambda i: (0, 0))],
        out_specs=[pl.BlockSpec((tm, IDX_DIM), lambda i: (i, 0)),
                   pl.BlockSpec((N_IDX_HEADS, tm), lambda i: (0, i))],
        out_shape=[jax.ShapeDtypeStruct((t, IDX_DIM), BF16),
                   jax.ShapeDtypeStruct((N_IDX_HEADS, t), F32)],
        compiler_params=_cparams("parallel"),
        name="kidx_norm",
    )(p1, g.reshape(1, IDX_DIM), b.reshape(1, IDX_DIM))


def _head(h):
    return slice(h * HEAD_DIM, (h + 1) * HEAD_DIM)


def _attn_state(n_heads):
    return ([pltpu.VMEM((CK, TQ), F32) for _ in range(2 * n_heads)]
            + [pltpu.VMEM((HEAD_DIM, TQ), F32) for _ in range(n_heads)]
            + [pltpu.VMEM((1, TQ), F32) for _ in range(2 * n_heads)])


def _attn_reset(state, n_heads):
    state = state[2 * n_heads:]
    for h in range(n_heads):
        state[h][...] = jnp.zeros((HEAD_DIM, TQ), F32)
        state[n_heads + h][...] = jnp.full((1, TQ), NEG, F32)
        state[2 * n_heads + h][...] = jnp.zeros((1, TQ), F32)


def _attn_run(state, n_heads, i, q_ref, k_ref, consume):
    def scores(j, slot):
        start = pl.multiple_of(j * CK, CK)
        for h in range(n_heads):
            state[slot * n_heads + h][...] = lax.dot_general(
                k_ref[pl.ds(start, CK), _head(h)], q_ref[:, _head(h)], _NT,
                preferred_element_type=F32)

    scores(0, 0)

    def pair(p, c):
        j = 2 * p
        scores(j + 1, 1)
        consume(j, 0, False)
        scores(j + 2, 0)
        consume(j + 1, 1, False)
        return c

    lax.fori_loop(0, i // 2, pair, 0)

    @pl.when(i % 2 == 0)
    def _():
        consume(i, 0, True)

    @pl.when(i % 2 == 1)
    def _():
        scores(i, 1)
        consume(i - 1, 0, False)
        consume(i, 1, True)


def _attn_update(state, n_heads, h, slot, vt_blk, t_bias, c):
    s_ref, state = state[slot * n_heads + h], state[2 * n_heads:]
    acc_ref, m_ref, l_ref = state[h], state[n_heads + h], state[2 * n_heads + h]
    t = s_ref[...] + t_bias
    m_old = m_ref[...]
    m_new = jnp.maximum(m_old, jnp.max(t, axis=0, keepdims=True) + c)
    alpha = jnp.exp2(m_old - m_new)
    p = jnp.exp2(t - (m_new - c))
    l_ref[...] = alpha * l_ref[...] + jnp.sum(p, axis=0, keepdims=True)
    acc_ref[...] = alpha * acc_ref[...] + jnp.dot(vt_blk, p.astype(BF16), preferred_element_type=F32)
    m_ref[...] = m_new


def _attn_finish(state, n_heads, h):
    state = state[2 * n_heads:]
    return (state[h][...] * (1.0 / state[2 * n_heads + h][...])).T


def _key_offsets():
    return lax.broadcasted_iota(I32, (CK, TQ), 0), lax.broadcasted_iota(I32, (CK, TQ), 1)


def _from_sortable(code):
    return pltpu.bitcast(code ^ ((code >> 31) & 0x7FFFFFFF), F32)


def _dsa_kernel(slopes_ref, qi_ref, wi_ref, kidx_ref, qa_ref, ka_ref, vat_ref, o_ref,
                keys_ref, sb_ref, *state, top_k, n_idx_bits):
    i = pl.program_id(0)
    rows, cols = _key_offsets()
    causal = rows <= cols
    nh = N_HEADS_A

    @pl.when(i == 0)
    def _():
        rows_f = rows.astype(F32)
        for h in range(nh):
            sb_ref[h] = slopes_ref[h] * rows_f

    def score_chunk(j, diagonal):
        start = pl.multiple_of(j * CK, CK)
        kc = kidx_ref[pl.ds(start, CK), :]
        acc = jnp.zeros((CK, TQ), F32)
        for h in range(N_IDX_HEADS):
            z = lax.dot_general(kc, qi_ref[0, h * TQ:(h + 1) * TQ, :], _NT,
                                preferred_element_type=F32)
            acc = acc + wi_ref[h:h + 1, :] * jnp.maximum(z, 0.0)
        if diagonal:
            acc = jnp.where(causal, acc, NEG)
        keys_ref[pl.ds(start, CK), :] = acc

    def score_body(j, c):
        score_chunk(j, False)
        return c

    lax.fori_loop(0, i, score_body, 0)
    score_chunk(i, True)

    def count32(pred):
        def body(j, cnt):
            start = pl.multiple_of(j * CK, CK)
            hit = jnp.where(pred(keys_ref[pl.ds(start, CK), :], rows + j * CK), 1, 0)
            return cnt + jnp.sum(hit.reshape(CK // SUBLANES, SUBLANES, TQ), axis=0)
        cnt = lax.fori_loop(0, i + 1, body, jnp.zeros((SUBLANES, TQ), I32))
        return jnp.sum(cnt, axis=0, keepdims=True)

    def bit_body(b, carry):
        code, n_code = carry
        cand = code + jnp.left_shift(jnp.int32(1), 31 - b)
        cand_f = _from_sortable(cand)
        n_cand = count32(lambda score, idx: score >= cand_f)
        ok = n_cand >= top_k
        return jnp.where(ok, cand, code), jnp.where(ok, n_cand, n_code)

    code, n_ge = lax.fori_loop(0, 32, bit_body, (jnp.full((1, TQ), INT_MIN, I32),
                                                  jnp.zeros((1, TQ), I32) + (i + 1) * CK))
    thr = _from_sortable(code)

    @pl.when(jnp.max(n_ge) > top_k)
    def _():
        keep = top_k - count32(lambda score, idx: score > thr)

        def idx_bit(b, last):
            cand = last + jnp.left_shift(jnp.int32(1), n_idx_bits - 1 - b)
            below = count32(lambda score, idx: jnp.logical_and(score == thr, idx < cand))
            return jnp.where(below < keep, cand, last)

        last = lax.fori_loop(0, n_idx_bits, idx_bit, jnp.zeros((1, TQ), I32))

        def drop(j, c):
            start = pl.multiple_of(j * CK, CK)
            score = keys_ref[pl.ds(start, CK), :]
            late_tie = jnp.logical_and(score == thr, rows + j * CK > last)
            keys_ref[pl.ds(start, CK), :] = jnp.where(late_tie, -jnp.inf, score)
            return c

        lax.fori_loop(0, i + 1, drop, 0)

    _attn_reset(state, nh)

    def consume(j, slot, diagonal):
        start = pl.multiple_of(j * CK, CK)
        sel = keys_ref[pl.ds(start, CK), :] >= thr
        if diagonal:
            sel = jnp.logical_and(sel, causal)
        mask_bias = jnp.where(sel, 0.0, NEG)
        off = ((j - i) * CK).astype(F32)
        for h in range(nh):
            _attn_update(state, nh, h, slot, vat_ref[_head(h), pl.ds(start, CK)],
                         sb_ref[h] + mask_bias, slopes_ref[h] * off)

    _attn_run(state, nh, i, qa_ref, ka_ref, consume)
    for h in range(nh):
        o_ref[:, _head(h)] = _attn_finish(state, nh, h).astype(o_ref.dtype)


def _dsa(slopes, qi_r, wi_t, kidx, qa, ka, vat, top_k):
    w, t = vat.shape
    nh = w // HEAD_DIM
    return pl.pallas_call(
        functools.partial(_dsa_kernel, top_k=top_k, n_idx_bits=max(1, (t - 1).bit_length())),
        grid_spec=pltpu.PrefetchScalarGridSpec(
            num_scalar_prefetch=0,
            grid=(t // TQ,),
            in_specs=[
                pl.BlockSpec(memory_space=pltpu.SMEM),
                pl.BlockSpec((1, N_IDX_HEADS * TQ, IDX_DIM), lambda i: (i, 0, 0)),
                pl.BlockSpec((N_IDX_HEADS, TQ), lambda i: (0, i)),
                _resident(kidx.shape),
                pl.BlockSpec((TQ, w), lambda i: (i, 0)),
                _resident((t, w)),
                _resident(vat.shape),
            ],
            out_specs=pl.BlockSpec((TQ, w), lambda i: (i, 0)),
            scratch_shapes=[pltpu.VMEM((t, TQ), F32),
                            pltpu.VMEM((nh, CK, TQ), F32)] + _attn_state(nh),
        ),
        out_shape=jax.ShapeDtypeStruct((t, w), BF16),
        compiler_params=_cparams("arbitrary"),
        name="dsa",
    )(slopes, qi_r, wi_t, kidx, qa, ka, vat)


def _moba_kernel(slopes_ref, qb_ref, kb_ref, vbt_ref, o_ref, kmean_ref, sel_ref, sb_ref, *state, n_kb):
    i = pl.program_id(0)
    rows, cols = _key_offsets()
    nh = N_HEADS_B

    @pl.when(i == 0)
    def _():
        rows_f = rows.astype(F32)
        for h in range(nh):
            sb_ref[h] = slopes_ref[h] * rows_f
            for n in range(n_kb):
                blk = kb_ref[n * CK:(n + 1) * CK, _head(h)].astype(F32)
                kmean_ref[h, n:n + 1, :] = jnp.mean(blk, axis=0, keepdims=True)

    blk_id = lax.broadcasted_iota(I32, (n_kb, TQ), 0)
    for h in range(nh):
        q_h = qb_ref[:, _head(h)]
        km = kmean_ref[h]
        km_hi = km.astype(BF16)
        km_lo = (km - km_hi.astype(F32)).astype(BF16)
        gate = (lax.dot_general(km_hi, q_h, _NT, preferred_element_type=F32)
                + lax.dot_general(km_lo, q_h, _NT, preferred_element_type=F32))
        gate = jnp.where(blk_id < i, gate, NEG)
        sel = jnp.full((n_kb, TQ), NEG, F32)
        for _ in range(MOBA_TOPK):
            best = jnp.max(gate, axis=0, keepdims=True)
            first = jnp.min(jnp.where(gate == best, blk_id, n_kb), axis=0, keepdims=True)
            pick = blk_id == first
            sel = jnp.where(pick, 0.0, sel)
            gate = jnp.where(pick, -jnp.inf, gate)
        sel_ref[h] = jnp.where(blk_id < i, sel, NEG)

    _attn_reset(state, nh)

    def consume(j, slot, diagonal):
        start = pl.multiple_of(j * CK, CK)
        off = ((j - i) * CK).astype(F32)
        for h in range(nh):
            if diagonal:
                t_bias = jnp.where(rows <= cols, sb_ref[h], NEG)
            else:
                t_bias = sb_ref[h] + sel_ref[h, pl.ds(j, 1), :]
            _attn_update(state, nh, h, slot, vbt_ref[_head(h), pl.ds(start, CK)],
                         t_bias, slopes_ref[h] * off)

    _attn_run(state, nh, i, qb_ref, kb_ref, consume)
    for h in range(nh):
        o_ref[:, _head(h)] = _attn_finish(state, nh, h).astype(o_ref.dtype)


def _moba(slopes, qkv, vbt):
    w, t = vbt.shape
    nh = w // HEAD_DIM
    assert t % MOBA_BLOCK == 0 and TQ == MOBA_BLOCK and CK == MOBA_BLOCK
    n_kb = t // MOBA_BLOCK
    return pl.pallas_call(
        functools.partial(_moba_kernel, n_kb=n_kb),
        grid_spec=pltpu.PrefetchScalarGridSpec(
            num_scalar_prefetch=0,
            grid=(t // TQ,),
            in_specs=[
                pl.BlockSpec(memory_space=pltpu.SMEM),
                pl.BlockSpec((TQ, w), lambda i: (i, 0)),
                _resident((t, w), col_block=1),
                _resident(vbt.shape),
            ],
            out_specs=pl.BlockSpec((TQ, w), lambda i: (i, 0)),
            scratch_shapes=[pltpu.VMEM((nh, n_kb, HEAD_DIM), F32), pltpu.VMEM((nh, n_kb, TQ), F32),
                            pltpu.VMEM((nh, CK, TQ), F32)] + _attn_state(nh),
        ),
        out_shape=jax.ShapeDtypeStruct((t, w), BF16),
        compiler_params=_cparams("arbitrary"),
        name="moba",
    )(slopes, qkv, qkv, vbt)


def _mem_kernel(qc_ref, mk_ref, mvt_ref, o_ref):
    for h in range(N_HEADS_C):
        s = lax.dot_general(mk_ref[:, _head(h)], qc_ref[:, _head(h)], _NT,
                            preferred_element_type=F32)
        m = jnp.max(s, axis=0, keepdims=True)
        p = jnp.exp2(s - m)
        l = jnp.sum(p, axis=0, keepdims=True)
        acc = jnp.dot(mvt_ref[_head(h), :], p.astype(BF16), preferred_element_type=F32)
        o_ref[:, _head(h)] = (acc * (1.0 / l)).T.astype(o_ref.dtype)


def _mem_attn(qc, mk, mvt):
    t, w = qc.shape
    return pl.pallas_call(
        _mem_kernel,
        grid=(t // TQ,),
        in_specs=[pl.BlockSpec((TQ, w), lambda i: (i, 0)),
                  _resident(mk.shape), _resident(mvt.shape)],
        out_specs=pl.BlockSpec((TQ, w), lambda i: (i, 0)),
        out_shape=jax.ShapeDtypeStruct((t, w), BF16),
        compiler_params=_cparams("parallel"),
        name="mem_attn",
    )(qc, mk, mvt)


def _sigmoid(x):
    return 1.0 / (1.0 + jnp.exp(-x))


def _merge_kernel(x_ref, wga_ref, wgb_ref, wgc_ref, bga_ref, bgb_ref, bgc_ref,
                  oa_ref, ob_ref, oc_ref, wua_ref, wub_ref, wuc_ref, o_ref):
    xb = x_ref[...].astype(BF16)

    def branch(wg_ref, bg_ref, oo_ref, wu_ref):
        g = _sigmoid(jnp.dot(xb, wg_ref[...], preferred_element_type=F32) + bg_ref[...])
        return g * jnp.dot(oo_ref[...], wu_ref[...], preferred_element_type=F32)

    acc = branch(wga_ref, bga_ref, oa_ref, wua_ref)
    acc = acc + branch(wgb_ref, bgb_ref, ob_ref, wub_ref)
    acc = acc + branch(wgc_ref, bgc_ref, oc_ref, wuc_ref)
    o_ref[...] = acc.astype(o_ref.dtype)


def _merge(x, w_gate, b_gate, o_a, o_b, o_c, w_up_a, w_up_b, w_up_c, *, tm, tn):
    t, d = x.shape
    nb = d // tn
    b_gate = b_gate.reshape(1, 3 * d)

    def wg(k):
        return pl.BlockSpec((d, tn), lambda i, j, k=k: (0, j + k * nb))

    def bg(k):
        return pl.BlockSpec((1, tn), lambda i, j, k=k: (0, j + k * nb))

    def act(w):
        return pl.BlockSpec((tm, w), lambda i, j: (i, 0))

    def wu(w):
        return pl.BlockSpec((w, tn), lambda i, j: (0, j))

    return pl.pallas_call(
        _merge_kernel,
        grid=(t // tm, nb),
        in_specs=[pl.BlockSpec((tm, d), lambda i, j: (i, 0)),
                  wg(0), wg(1), wg(2), bg(0), bg(1), bg(2),
                  act(W_A), act(W_B), act(W_C), wu(W_A), wu(W_B), wu(W_C)],
        out_specs=pl.BlockSpec((tm, tn), lambda i, j: (i, j)),
        out_shape=jax.ShapeDtypeStruct((t, d), BF16),
        compiler_params=_cparams("parallel", "parallel"),
        name="merge",
    )(x, w_gate, w_gate, w_gate, b_gate, b_gate, b_gate, o_a, o_b, o_c, w_up_a, w_up_b, w_up_c)


def _layer_norm(y, g, b):
    mu = jnp.mean(y, axis=-1, keepdims=True)
    yc = y - mu
    var = jnp.mean(yc * yc, axis=-1, keepdims=True)
    return yc * lax.rsqrt(var + NORM_EPS) * g + b


def _route(logits):
    lane = lax.broadcasted_iota(I32, logits.shape, 1)
    is_g = lane < N_GROUPS
    gl = jnp.where(is_g, logits, -jnp.inf)
    gmax = jnp.max(gl, axis=-1, keepdims=True)
    g_sel = jnp.min(jnp.where(gl == gmax, lane, ROUTE_LANES), axis=-1, keepdims=True)
    p_g = 1.0 / jnp.sum(jnp.where(is_g, jnp.exp(gl - gmax), 0.0), axis=-1, keepdims=True)
    e_id = lane - N_GROUPS
    in_grp = jnp.logical_and(e_id >= g_sel * EXPERTS_PER_GROUP, e_id < (g_sel + 1) * EXPERTS_PER_GROUP)
    el = jnp.where(in_grp, logits, -jnp.inf)
    emax = jnp.max(el, axis=-1, keepdims=True)
    ex = jnp.where(in_grp, jnp.exp(el - emax), 0.0)
    pe = ex / jnp.sum(ex, axis=-1, keepdims=True)
    pe = jnp.where(in_grp, pe, -1.0)
    p1 = jnp.max(pe, axis=-1, keepdims=True)
    i1 = jnp.min(jnp.where(pe == p1, e_id, ROUTE_LANES), axis=-1, keepdims=True)
    pe2 = jnp.where(e_id == i1, -1.0, pe)
    p2 = jnp.max(pe2, axis=-1, keepdims=True)
    i2 = jnp.min(jnp.where(pe2 == p2, e_id, ROUTE_LANES), axis=-1, keepdims=True)
    denom = p1 + p2
    w1 = p_g * (p1 / denom)
    w2 = p_g * (p2 / denom)
    return jnp.where(lane == 0, w1,
                     jnp.where(lane == 1, w2,
                               jnp.where(lane == 2, i1.astype(F32),
                                         jnp.where(lane == 3, i2.astype(F32), 0.0))))


def _oproj_kernel(x_ref, m_ref, wo_ref, g_ref, b_ref, wr_ref, br_ref, x1_ref, r_ref, *, alpha):
    y = alpha * x_ref[...] + jnp.dot(m_ref[...], wo_ref[...], preferred_element_type=F32)
    x1 = _layer_norm(y, g_ref[...], b_ref[...])
    x1_ref[...] = x1
    x_hi = x1.astype(BF16)
    x_lo = (x1 - x_hi.astype(F32)).astype(BF16)
    parts = (jnp.dot(x_hi, wr_ref[...], preferred_element_type=F32)
             + jnp.dot(x_lo, wr_ref[...], preferred_element_type=F32))
    logits = parts + pltpu.roll(parts, ROUTE_LANES // 2, axis=1) + br_ref[...]
    r_ref[...] = _route(logits)


def _oproj(x, merged, w_o, ln_g, ln_b, w_r, b_r, *, alpha, tm):
    t, d = x.shape
    return pl.pallas_call(
        functools.partial(_oproj_kernel, alpha=alpha),
        grid=(t // tm,),
        in_specs=[pl.BlockSpec((tm, d), lambda i: (i, 0)),
                  pl.BlockSpec((tm, d), lambda i: (i, 0)),
                  _resident(w_o.shape),
                  pl.BlockSpec((1, d), lambda i: (0, 0)),
                  pl.BlockSpec((1, d), lambda i: (0, 0)),
                  _resident(w_r.shape),
                  pl.BlockSpec((1, ROUTE_LANES), lambda i: (0, 0))],
        out_specs=[pl.BlockSpec((tm, d), lambda i: (i, 0)),
                   pl.BlockSpec((tm, ROUTE_LANES), lambda i: (i, 0))],
        out_shape=[jax.ShapeDtypeStruct((t, d), F32),
                   jax.ShapeDtypeStruct((t, ROUTE_LANES), F32)],
        compiler_params=_cparams("parallel"),
        name="oproj",
    )(x, merged, w_o, ln_g.reshape(1, d), ln_b.reshape(1, d), w_r, b_r)


def _row_copy(src_hbm, row, dst, k, sem):
    return pltpu.make_async_copy(src_hbm.at[pl.ds(row, 1), :], dst.at[pl.ds(k, 1), :], sem)


def _expert_kernel(tok_ref, te_ref, first_ref, nxt_ref, ws_ref, ng_ref, nu_ref, x_hbm, w1_hbm, w3_hbm, w2_hbm,
                   o_ref, buf, w1b, w3b, w2b, sem, wsem, *, layer):
    i = pl.program_id(0)
    n_used = nu_ref[0]
    slot = lax.rem(i, 2)

    def gather(tile, s):
        base = tile * MOE_TM

        def body(g, c):
            r0 = pl.multiple_of(g * ROW_DMA_UNROLL, ROW_DMA_UNROLL)
            for k in range(ROW_DMA_UNROLL):
                _row_copy(x_hbm, tok_ref[base + r0 + k], buf.at[s], r0 + k, sem.at[s]).start()
            return c

        lax.fori_loop(0, ng_ref[tile], body, 0)

    def gathered(tile, s):
        def body(g, c):
            r0 = pl.multiple_of(g * ROW_DMA_UNROLL, ROW_DMA_UNROLL)
            pltpu.make_async_copy(x_hbm.at[pl.ds(0, ROW_DMA_UNROLL), :],
                                  buf.at[s, pl.ds(r0, ROW_DMA_UNROLL), :], sem.at[s]).wait()
            return c

        lax.fori_loop(0, ng_ref[tile], body, 0)

    def weight_copies(e, s):
        return [pltpu.make_async_copy(src.at[layer, e], dst.at[s], wsem.at[s])
                for src, dst in ((w1_hbm, w1b), (w3_hbm, w3b), (w2_hbm, w2b))]

    @pl.when(i == 0)
    def _():
        buf[...] = jnp.zeros_like(buf)
        gather(0, 0)
        for cp in weight_copies(te_ref[0], 0):
            cp.start(priority=BULK_DMA_PRIORITY)

    @pl.when(i < n_used)
    def _():
        ws = ws_ref[i]

        @pl.when(first_ref[i] == 1)
        def _():
            for cp in weight_copies(te_ref[i], ws):
                cp.wait()

            @pl.when(nxt_ref[i] >= 0)
            def _():
                for cp in weight_copies(nxt_ref[i], 1 - ws):
                    cp.start(priority=BULK_DMA_PRIORITY)

        gathered(i, slot)

        @pl.when(i + 1 < n_used)
        def _():
            gather(i + 1, 1 - slot)

        xs = buf[slot].astype(BF16)
        h1 = jnp.dot(xs, w1b[ws].astype(BF16), preferred_element_type=F32)
        h3 = jnp.dot(xs, w3b[ws].astype(BF16), preferred_element_type=F32)
        hid = (h1 * _sigmoid(h1)) * h3
        o_ref[...] = jnp.dot(hid.astype(BF16), w2b[ws].astype(BF16), preferred_element_type=F32)

    @pl.when(i >= n_used)
    def _():
        o_ref[...] = jnp.zeros_like(o_ref)


def _experts(plan, x1, w1, w3, w2, layer):
    r = plan[0].shape[0]
    d = x1.shape[1]
    de = w1.shape[-1]
    n_pref = len(plan)
    return pl.pallas_call(
        functools.partial(_expert_kernel, layer=layer),
        grid_spec=pltpu.PrefetchScalarGridSpec(
            num_scalar_prefetch=n_pref,
            grid=(r // MOE_TM,),
            in_specs=[pl.BlockSpec(memory_space=pl.ANY)] * 4,
            out_specs=pl.BlockSpec((MOE_TM, d), lambda i, *_: (i, 0)),
            scratch_shapes=[pltpu.VMEM((2, MOE_TM, d), F32),
                            pltpu.VMEM((2, d, de), F32), pltpu.VMEM((2, d, de), F32),
                            pltpu.VMEM((2, de, d), F32),
                            pltpu.SemaphoreType.DMA((2,)), pltpu.SemaphoreType.DMA((2,))],
        ),
        out_shape=jax.ShapeDtypeStruct((r, d), F32),
        compiler_params=_cparams("arbitrary"),
        name="moe_experts",
    )(*plan, x1, w1, w3, w2)


def _combine_kernel(pos_ref, x1_ref, route_ref, g_ref, b_ref, ys_hbm, o_ref, buf, sem, *, alpha):
    i = pl.program_id(0)
    slot = lax.rem(i, 2)

    def gather(tile, s):
        base = tile * MOE_TM

        def body(r, c):
            for k in range(2):
                _row_copy(ys_hbm, pos_ref[2 * (base + r) + k], buf.at[s, k], r, sem.at[s]).start()
            return c

        lax.fori_loop(0, MOE_TM, body, 0, unroll=ROW_DMA_UNROLL // 2)

    @pl.when(i == 0)
    def _():
        gather(0, 0)

    for k in range(2):
        pltpu.make_async_copy(ys_hbm.at[pl.ds(0, MOE_TM), :], buf.at[slot, k], sem.at[slot]).wait()

    @pl.when(i + 1 < pl.num_programs(0))
    def _():
        gather(i + 1, 1 - slot)

    route = route_ref[...]
    y = alpha * x1_ref[...] + (route[:, 0:1] * buf[slot, 0] + route[:, 1:2] * buf[slot, 1])
    o_ref[...] = _layer_norm(y, g_ref[...], b_ref[...])


def _combine(pos, x1, route, ln_g, ln_b, ys, *, alpha):
    t, d = x1.shape
    return pl.pallas_call(
        functools.partial(_combine_kernel, alpha=alpha),
        grid_spec=pltpu.PrefetchScalarGridSpec(
            num_scalar_prefetch=1,
            grid=(t // MOE_TM,),
            in_specs=[pl.BlockSpec((MOE_TM, d), lambda i, pos: (i, 0)),
                      pl.BlockSpec((MOE_TM, ROUTE_LANES), lambda i, pos: (i, 0)),
                      pl.BlockSpec((1, d), lambda i, pos: (0, 0)),
                      pl.BlockSpec((1, d), lambda i, pos: (0, 0)),
                      pl.BlockSpec(memory_space=pl.ANY)],
            out_specs=pl.BlockSpec((MOE_TM, d), lambda i, pos: (i, 0)),
            scratch_shapes=[pltpu.VMEM((2, 2, MOE_TM, d), F32), pltpu.SemaphoreType.DMA((2,))],
        ),
        out_shape=jax.ShapeDtypeStruct((t, d), F32),
        compiler_params=_cparams("arbitrary"),
        name="moe_combine",
    )(pos, x1, route, ln_g.reshape(1, d), ln_b.reshape(1, d), ys)


def _moe_plan(route, t):
    flat_e = route[:, 2:4].astype(I32).reshape(-1)
    n = flat_e.shape[0]
    n_rows = n + N_EXPERTS * MOE_TM
    n_tiles = n_rows // MOE_TM
    onehot = (flat_e[:, None] == jnp.arange(N_EXPERTS, dtype=I32)[None, :]).astype(I32)
    running = jnp.cumsum(onehot, axis=0)
    rank = jnp.sum(onehot * running, axis=1) - 1
    counts = running[-1]
    tiles_per = (counts + MOE_TM - 1) // MOE_TM
    tile_end = jnp.cumsum(tiles_per)
    grp_row0 = (tile_end - tiles_per) * MOE_TM
    pos = jnp.sum(onehot * grp_row0[None, :], axis=1) + rank
    row_token = jnp.zeros((n_rows,), I32).at[pos].set(jnp.arange(n, dtype=I32) // 2, unique_indices=True)
    n_used = tile_end[-1]
    tile_ids = jnp.minimum(jnp.arange(n_tiles, dtype=I32), n_used - 1)
    tile_expert = jnp.sum((tile_end[None, :] <= tile_ids[:, None]).astype(I32), axis=1)
    tile_expert = jnp.minimum(tile_expert, N_EXPERTS - 1)
    first = jnp.concatenate([jnp.ones((1,), I32), (tile_expert[1:] != tile_expert[:-1]).astype(I32)])
    w_slot = (jnp.cumsum(first) - 1) % 2
    e_ids = jnp.arange(N_EXPERTS, dtype=I32)
    later = jnp.logical_and(e_ids[None, :] > e_ids[:, None], tiles_per[None, :] > 0)
    next_e = jnp.min(jnp.where(later, e_ids[None, :], N_EXPERTS), axis=1)
    next_e = jnp.where(next_e == N_EXPERTS, -1, next_e)
    tile_of = (tile_expert[:, None] == e_ids[None, :]).astype(I32)
    nxt = jnp.sum(tile_of * next_e[None, :], axis=1)
    t_ids = jnp.arange(n_tiles, dtype=I32)
    rows_left = jnp.sum(tile_of * counts[None, :], axis=1) - (t_ids * MOE_TM - jnp.sum(tile_of * grp_row0[None, :], axis=1))
    rows_valid = jnp.where(t_ids < n_used, jnp.clip(rows_left, 0, MOE_TM), 0)
    n_groups = (rows_valid + ROW_DMA_UNROLL - 1) // ROW_DMA_UNROLL
    plan = (row_token, tile_expert, first, nxt.astype(I32), w_slot.astype(I32), n_groups.astype(I32),
            n_used.reshape(1).astype(I32))
    return plan, pos


def kernel(x, mem, w_mem_kv, w_in, g_cq, g_ckv, g_kidx, b_kidx, w_uq, w_uqi, w_ukv,
           w_up_a, w_up_b, w_up_c, w_gate, b_gate, w_o, ln1_g, ln1_b,
           w_grp, b_grp, w_rt, b_rt, w1, w3, w2, ln2_g, ln2_b):
    bsz, t, d = x.shape
    assert bsz == 1 and t % TQ == 0
    depth = w_in.shape[0]
    alpha = (2 * depth) ** 0.25
    scale = HEAD_DIM ** -0.5 * LOG2E
    top_k = min(TOPK_A_MAX, t // 4)
    n_mem = mem.shape[1]
    tm = min(t, 512)

    slopes = LOG2E * 2.0 ** (-8.0 * jnp.arange(1, N_ALIBI + 1, dtype=F32) / N_ALIBI)
    slopes_a, slopes_b = slopes[0::2], slopes[1::2]

    mkv = _mm(mem[0], w_mem_kv.astype(BF16), out_dtype=BF16, tm=n_mem, tn=W_C)
    mk = mkv[:, :W_C]
    mvt = mkv[:, W_C:].T

    xl = x[0]
    for l in range(depth):
        wl = w_in[l]
        o = 0
        parts = []
        for width in (Q_RANK_A, KV_RANK_A, IDX_DIM, N_IDX_HEADS, W_B, W_B, W_B, W_C):
            parts.append(wl[:, o:o + width])
            o += width
        w_cq, w_ckv, w_ki, w_wi, w_qb, w_kb, w_vb, w_qc = parts
        n_small = Q_RANK_A + KV_RANK_A + IDX_DIM + N_IDX_HEADS
        pad = (-n_small) % LANES
        w_p1 = jnp.concatenate([w_cq, w_ckv, w_ki, w_wi * N_IDX_HEADS ** -0.5,
                                jnp.zeros((d, pad), F32)], axis=1).astype(BF16)
        w_p2 = jnp.concatenate([w_qb * scale, w_kb, w_vb, w_qc * scale], axis=1).astype(BF16)
        p1 = _mm(xl, w_p1, out_dtype=F32, tm=tm, tn=w_p1.shape[1])
        p2 = _mm(xl, w_p2, out_dtype=BF16, tm=tm, tn=w_p2.shape[1] // 2)
        assert Q_RANK_A % KV_RANK_A == 0 and (Q_RANK_A + KV_RANK_A) % LANES == 0

        w_q = jnp.concatenate([w_uq[l] * scale, w_uqi[l] * IDX_DIM ** -0.5], axis=1).astype(BF16)
        qq = _mm(p1, w_q, out_dtype=BF16, tm=tm, tn=w_q.shape[1], norm_g=g_cq[l])
        kv = _mm(p1, w_ukv[l].astype(BF16), out_dtype=BF16, tm=tm, tn=2 * W_A, norm_g=g_ckv[l],
                 x_col_block=Q_RANK_A // KV_RANK_A)
        kidx, wi_t = _kidx_norm(p1, (Q_RANK_A + KV_RANK_A) // LANES, g_kidx[l], b_kidx[l])
        qi_r = (qq[:, W_A:].reshape(t // TQ, TQ, N_IDX_HEADS, IDX_DIM).transpose(0, 2, 1, 3)
                .reshape(t // TQ, N_IDX_HEADS * TQ, IDX_DIM))
        o_a = _dsa(slopes_a, qi_r, wi_t, kidx, qq, kv, kv[:, W_A:].T, top_k)
        o_b = _moba(slopes_b, p2, p2[:, 2 * W_B:3 * W_B].T)
        o_c = _mem_attn(p2[:, 3 * W_B:], mk, mvt)

        merged = _merge(xl, w_gate[l].astype(BF16), b_gate[l], o_a, o_b, o_c,
                        w_up_a[l].astype(BF16), w_up_b[l].astype(BF16), w_up_c[l].astype(BF16),
                        tm=min(t, 1024), tn=512)

        n_route = N_GROUPS + N_EXPERTS
        w_r = jnp.concatenate([w_grp[l], w_rt[l]], axis=1)
        w_r_hi = w_r.astype(BF16)
        w_r_lo = (w_r - w_r_hi.astype(F32)).astype(BF16)
        lane_pad = jnp.zeros((d, ROUTE_LANES // 2 - n_route), BF16)
        w_r2 = jnp.concatenate([w_r_hi, lane_pad, w_r_lo, lane_pad], axis=1)
        b_r = jnp.concatenate([b_grp[l], b_rt[l],
                               jnp.zeros((ROUTE_LANES - n_route,), F32)]).reshape(1, ROUTE_LANES)
        x1, route = _oproj(xl, merged, w_o[l].astype(BF16), ln1_g[l], ln1_b[l], w_r2, b_r,
                           alpha=alpha, tm=tm)

        plan, pos = _moe_plan(route, t)
        ys = _experts(plan, x1, w1, w3, w2, l)
        xl = _combine(pos, x1, route, ln2_g[l], ln2_b[l], ys, alpha=alpha)
    return xl[None]
```

```python
import functools

import jax
import jax.numpy as jnp
from jax import lax
from jax.experimental import pallas as pl
from jax.experimental.pallas import tpu as pltpu

HEAD_DIM = 128
N_HEADS_A = 6
N_HEADS_B = 6
N_HEADS_C = 4
W_A = N_HEADS_A * HEAD_DIM
W_B = N_HEADS_B * HEAD_DIM
W_C = N_HEADS_C * HEAD_DIM
Q_RANK_A = 512
KV_RANK_A = 256
N_IDX_HEADS = 16
IDX_DIM = 64
TOPK_A_MAX = 256
MOBA_BLOCK = 256
MOBA_TOPK = 3
N_ALIBI = N_HEADS_A + N_HEADS_B
N_GROUPS = 4
EXPERTS_PER_GROUP = 8
N_EXPERTS = N_GROUPS * EXPERTS_PER_GROUP
D_EXPERT = 512
NORM_EPS = 1e-5
NEG = -1e30
LOG2E = 1.4426950408889634
INT_MIN = -(2 ** 31)

LANES = 128
SUBLANES = 8
TQ = 256
CK = 256
ROUTE_LANES = 128
MOE_TM = 256
ROW_DMA_UNROLL = 8
BULK_DMA_PRIORITY = 1
VMEM_LIMIT = 56 * 1024 * 1024

F32 = jnp.float32
BF16 = jnp.bfloat16
I32 = jnp.int32

_NT = (((1,), (1,)), ((), ()))


def _cparams(*sem):
    return pltpu.CompilerParams(dimension_semantics=sem, vmem_limit_bytes=VMEM_LIMIT)


def _resident(shape, col_block=0):
    index = (0,) * (len(shape) - 1) + (col_block,)
    return pl.BlockSpec(shape, lambda *_: index, pipeline_mode=pl.Buffered(1))


def _mm_kernel(*refs, has_norm, has_bias):
    x_ref, w_ref = refs[0], refs[1]
    k = 2
    x = x_ref[...]
    if has_norm:
        g_ref = refs[k]
        k += 1
        xf = x.astype(F32)
        x = xf * lax.rsqrt(jnp.mean(xf * xf, axis=-1, keepdims=True) + NORM_EPS) * g_ref[...]
    acc = jnp.dot(x.astype(BF16), w_ref[...], preferred_element_type=F32)
    if has_bias:
        acc = acc + refs[k][...]
        k += 1
    o_ref = refs[k]
    o_ref[...] = acc.astype(o_ref.dtype)


def _mm(x, w, *, out_dtype, tm, tn, norm_g=None, bias=None, x_col_block=0):
    m = x.shape[0]
    kdim, n = w.shape
    assert m % tm == 0 and n % tn == 0
    in_specs = [pl.BlockSpec((tm, kdim), lambda i, j: (i, x_col_block)),
                pl.BlockSpec((kdim, tn), lambda i, j: (0, j))]
    args = [x, w]
    if norm_g is not None:
        in_specs.append(pl.BlockSpec((1, kdim), lambda i, j: (0, 0)))
        args.append(norm_g.reshape(1, kdim).astype(F32))
    if bias is not None:
        in_specs.append(pl.BlockSpec((1, tn), lambda i, j: (0, j)))
        args.append(bias.reshape(1, n).astype(F32))
    return pl.pallas_call(
        functools.partial(_mm_kernel, has_norm=norm_g is not None, has_bias=bias is not None),
        grid=(m // tm, n // tn),
        in_specs=in_specs,
        out_specs=pl.BlockSpec((tm, tn), lambda i, j: (i, j)),
        out_shape=jax.ShapeDtypeStruct((m, n), out_dtype),
        compiler_params=_cparams("parallel", "parallel"),
        name="mm",
    )(*args)


def _kidx_kernel(x_ref, g_ref, b_ref, k_ref, w_ref):
    blk = x_ref[...]
    x = blk[:, :IDX_DIM]
    mu = jnp.mean(x, axis=-1, keepdims=True)
    xc = x - mu
    var = jnp.mean(xc * xc, axis=-1, keepdims=True)
    k_ref[...] = (xc * lax.rsqrt(var + NORM_EPS) * g_ref[...] + b_ref[...]).astype(k_ref.dtype)
    w_ref[...] = blk.T[IDX_DIM:IDX_DIM + N_IDX_HEADS, :]


def _kidx_norm(p1, col_block, g, b):
    t = p1.shape[0]
    tm = min(t, 1024)
    return pl.pallas_call(
        _kidx_kernel,
        grid=(t // tm,),
        in_specs=[pl.BlockSpec((tm, LANES), lambda i: (i, col_block)),
                  pl.BlockSpec((1, IDX_DIM), lambda i: (0, 0)),
                  pl.BlockSpec((1, IDX_DIM), lambda i: (0, 0))],
        out_specs=[pl.BlockSpec((tm, IDX_DIM), lambda i: (i, 0)),
                   pl.BlockSpec((N_IDX_HEADS, tm), lambda i: (0, i))],
        out_shape=[jax.ShapeDtypeStruct((t, IDX_DIM), BF16),
                   jax.ShapeDtypeStruct((N_IDX_HEADS, t), F32)],
        compiler_params=_cparams("parallel"),
        name="kidx_norm",
    )(p1, g.reshape(1, IDX_DIM), b.reshape(1, IDX_DIM))


def _head(h):
    return slice(h * HEAD_DIM, (h + 1) * HEAD_DIM)


def _attn_state(n_heads):
    return ([pltpu.VMEM((CK, TQ), F32) for _ in range(2 * n_heads)]
            + [pltpu.VMEM((HEAD_DIM, TQ), F32) for _ in range(n_heads)]
            + [pltpu.VMEM((1, TQ), F32) for _ in range(2 * n_heads)])


def _attn_reset(state, n_heads):
    state = state[2 * n_heads:]
    for h in range(n_heads):
        state[h][...] = jnp.zeros((HEAD_DIM, TQ), F32)
        state[n_heads + h][...] = jnp.full((1, TQ), NEG, F32)
        state[2 * n_heads + h][...] = jnp.zeros((1, TQ), F32)


def _attn_run(state, n_heads, i, q_ref, k_ref, consume):
    def scores(j, slot):
        start = pl.multiple_of(j * CK, CK)
        for h in range(n_heads):
            state[slot * n_heads + h][...] = lax.dot_general(
                k_ref[pl.ds(start, CK), _head(h)], q_ref[:, _head(h)], _NT,
                preferred_element_type=F32)

    scores(0, 0)

    def pair(p, c):
        j = 2 * p
        scores(j + 1, 1)
        consume(j, 0, False)
        scores(j + 2, 0)
        consume(j + 1, 1, False)
        return c

    lax.fori_loop(0, i // 2, pair, 0)

    @pl.when(i % 2 == 0)
    def _():
        consume(i, 0, True)

    @pl.when(i % 2 == 1)
    def _():
        scores(i, 1)
        consume(i - 1, 0, False)
        consume(i, 1, True)


def _attn_update(state, n_heads, h, slot, vt_blk, t_bias, c):
    s_ref, state = state[slot * n_heads + h], state[2 * n_heads:]
    acc_ref, m_ref, l_ref = state[h], state[n_heads + h], state[2 * n_heads + h]
    t = s_ref[...] + t_bias
    m_old = m_ref[...]
    m_new = jnp.maximum(m_old, jnp.max(t, axis=0, keepdims=True) + c)
    alpha = jnp.exp2(m_old - m_new)
    p = jnp.exp2(t - (m_new - c))
    l_ref[...] = alpha * l_ref[...] + jnp.sum(p, axis=0, keepdims=True)
    acc_ref[...] = alpha * acc_ref[...] + jnp.dot(vt_blk, p.astype(BF16), preferred_element_type=F32)
    m_ref[...] = m_new


def _attn_finish(state, n_heads, h):
    state = state[2 * n_heads:]
    return (state[h][...] * (1.0 / state[2 * n_heads + h][...])).T


def _key_offsets():
    return lax.broadcasted_iota(I32, (CK, TQ), 0), lax.broadcasted_iota(I32, (CK, TQ), 1)


def _from_sortable(code):
    return pltpu.bitcast(code ^ ((code >> 31) & 0x7FFFFFFF), F32)


def _dsa_kernel(slopes_ref, qi_ref, wi_ref, kidx_ref, qa_ref, ka_ref, vat_ref, o_ref,
                keys_ref, sb_ref, *state, top_k, n_idx_bits):
    i = pl.program_id(0)
    rows, cols = _key_offsets()
    causal = rows <= cols
    nh = N_HEADS_A

    @pl.when(i == 0)
    def _():
        rows_f = rows.astype(F32)
        for h in range(nh):
            sb_ref[h] = slopes_ref[h] * rows_f

    def score_chunk(j, diagonal):
        start = pl.multiple_of(j * CK, CK)
        kc = kidx_ref[pl.ds(start, CK), :]
        acc = jnp.zeros((CK, TQ), F32)
        for h in range(N_IDX_HEADS):
            z = lax.dot_general(kc, qi_ref[0, h * TQ:(h + 1) * TQ, :], _NT,
                                preferred_element_type=F32)
            acc = acc + wi_ref[h:h + 1, :] * jnp.maximum(z, 0.0)
        if diagonal:
            acc = jnp.where(causal, acc, NEG)
        keys_ref[pl.ds(start, CK), :] = acc

    def score_body(j, c):
        score_chunk(j, False)
        return c

    lax.fori_loop(0, i, score_body, 0)
    score_chunk(i, True)
    keys_ref[pl.ds(pl.multiple_of((i + 1) * CK, CK), CK), :] = jnp.full((CK, TQ), -jnp.inf, F32)

    rows2 = lax.broadcasted_iota(I32, (2 * CK, TQ), 0)

    def count32(pred):
        def body(p, cnt):
            start = pl.multiple_of(p * (2 * CK), 2 * CK)
            hit = jnp.where(pred(keys_ref[pl.ds(start, 2 * CK), :], start), 1, 0)
            return cnt + jnp.sum(hit.reshape(2 * CK // SUBLANES, SUBLANES, TQ), axis=0)
        cnt = lax.fori_loop(0, (i + 2) // 2, body, jnp.zeros((SUBLANES, TQ), I32))
        return jnp.sum(cnt, axis=0, keepdims=True)

    def more_bits(carry):
        b, _, n_code = carry
        return jnp.logical_and(b < 32, jnp.max(n_code) > top_k)

    def bit_body(carry):
        b, code, n_code = carry
        cand = code + jnp.left_shift(jnp.int32(1), 31 - b)
        cand_f = _from_sortable(cand)
        n_cand = count32(lambda score, start: score >= cand_f)
        ok = n_cand >= top_k
        return b + 1, jnp.where(ok, cand, code), jnp.where(ok, n_cand, n_code)

    _, code, n_ge = lax.while_loop(more_bits, bit_body,
                                   (jnp.int32(0), jnp.full((1, TQ), INT_MIN, I32),
                                    jnp.zeros((1, TQ), I32) + (i + 1) * CK))
    thr = jnp.where(code == INT_MIN, -jnp.inf, _from_sortable(code))

    @pl.when(jnp.max(n_ge) > top_k)
    def _():
        keep = top_k - count32(lambda score, start: score > thr)

        def idx_bit(b, last):
            cand = last + jnp.left_shift(jnp.int32(1), n_idx_bits - 1 - b)
            below = count32(lambda score, start: jnp.logical_and(score == thr, rows2 + start < cand))
            return jnp.where(below < keep, cand, last)

        last = lax.fori_loop(0, n_idx_bits, idx_bit, jnp.zeros((1, TQ), I32))

        def drop(j, c):
            start = pl.multiple_of(j * CK, CK)
            score = keys_ref[pl.ds(start, CK), :]
            late_tie = jnp.logical_and(score == thr, rows + j * CK > last)
            keys_ref[pl.ds(start, CK), :] = jnp.where(late_tie, -jnp.inf, score)
            return c

        lax.fori_loop(0, i + 1, drop, 0)

    _attn_reset(state, nh)

    def consume(j, slot, diagonal):
        start = pl.multiple_of(j * CK, CK)
        sel = keys_ref[pl.ds(start, CK), :] >= thr
        if diagonal:
            sel = jnp.logical_and(sel, causal)
        mask_bias = jnp.where(sel, 0.0, NEG)
        off = ((j - i) * CK).astype(F32)
        for h in range(nh):
            _attn_update(state, nh, h, slot, vat_ref[_head(h), pl.ds(start, CK)],
                         sb_ref[h] + mask_bias, slopes_ref[h] * off)

    _attn_run(state, nh, i, qa_ref, ka_ref, consume)
    for h in range(nh):
        o_ref[:, _head(h)] = _attn_finish(state, nh, h).astype(o_ref.dtype)


def _dsa(slopes, qi_r, wi_t, kidx, qa, ka, vat, top_k):
    w, t = vat.shape
    nh = w // HEAD_DIM
    return pl.pallas_call(
        functools.partial(_dsa_kernel, top_k=top_k, n_idx_bits=max(1, (t - 1).bit_length())),
        grid_spec=pltpu.PrefetchScalarGridSpec(
            num_scalar_prefetch=0,
            grid=(t // TQ,),
            in_specs=[
                pl.BlockSpec(memory_space=pltpu.SMEM),
                pl.BlockSpec((1, N_IDX_HEADS * TQ, IDX_DIM), lambda i: (i, 0, 0)),
                pl.BlockSpec((N_IDX_HEADS, TQ), lambda i: (0, i)),
                _resident(kidx.shape),
                pl.BlockSpec((TQ, w), lambda i: (i, 0)),
                _resident((t, w)),
                _resident(vat.shape),
            ],
            out_specs=pl.BlockSpec((TQ, w), lambda i: (i, 0)),
            scratch_shapes=[pltpu.VMEM((t + CK, TQ), F32),
                            pltpu.VMEM((nh, CK, TQ), F32)] + _attn_state(nh),
        ),
        out_shape=jax.ShapeDtypeStruct((t, w), BF16),
        compiler_params=_cparams("arbitrary"),
        name="dsa",
    )(slopes, qi_r, wi_t, kidx, qa, ka, vat)


def _moba_kernel(slopes_ref, qb_ref, kb_ref, vbt_ref, o_ref, kmean_ref, sel_ref, sb_ref, *state, n_kb):
    i = pl.program_id(0)
    rows, cols = _key_offsets()
    nh = N_HEADS_B

    @pl.when(i == 0)
    def _():
        rows_f = rows.astype(F32)
        for h in range(nh):
            sb_ref[h] = slopes_ref[h] * rows_f
            for n in range(n_kb):
                blk = kb_ref[n * CK:(n + 1) * CK, _head(h)].astype(F32)
                kmean_ref[h, n:n + 1, :] = jnp.mean(blk, axis=0, keepdims=True)

    blk_id = lax.broadcasted_iota(I32, (n_kb, TQ), 0)
    for h in range(nh):
        q_h = qb_ref[:, _head(h)]
        km = kmean_ref[h]
        km_hi = km.astype(BF16)
        km_lo = (km - km_hi.astype(F32)).astype(BF16)
        gate = (lax.dot_general(km_hi, q_h, _NT, preferred_element_type=F32)
                + lax.dot_general(km_lo, q_h, _NT, preferred_element_type=F32))
        gate = jnp.where(blk_id < i, gate, NEG)
        sel = jnp.full((n_kb, TQ), NEG, F32)
        for _ in range(MOBA_TOPK):
            best = jnp.max(gate, axis=0, keepdims=True)
            first = jnp.min(jnp.where(gate == best, blk_id, n_kb), axis=0, keepdims=True)
            pick = blk_id == first
            sel = jnp.where(pick, 0.0, sel)
            gate = jnp.where(pick, -jnp.inf, gate)
        sel_ref[h] = jnp.where(blk_id < i, sel, NEG)

    _attn_reset(state, nh)

    def consume(j, slot, diagonal):
        start = pl.multiple_of(j * CK, CK)
        off = ((j - i) * CK).astype(F32)
        for h in range(nh):
            if diagonal:
                t_bias = jnp.where(rows <= cols, sb_ref[h], NEG)
            else:
                t_bias = sb_ref[h] + sel_ref[h, pl.ds(j, 1), :]
            _attn_update(state, nh, h, slot, vbt_ref[_head(h), pl.ds(start, CK)],
                         t_bias, slopes_ref[h] * off)

    _attn_run(state, nh, i, qb_ref, kb_ref, consume)
    for h in range(nh):
        o_ref[:, _head(h)] = _attn_finish(state, nh, h).astype(o_ref.dtype)


def _moba(slopes, qkv, vbt):
    w, t = vbt.shape
    nh = w // HEAD_DIM
    assert t % MOBA_BLOCK == 0 and TQ == MOBA_BLOCK and CK == MOBA_BLOCK
    n_kb = t // MOBA_BLOCK
    return pl.pallas_call(
        functools.partial(_moba_kernel, n_kb=n_kb),
        grid_spec=pltpu.PrefetchScalarGridSpec(
            num_scalar_prefetch=0,
            grid=(t // TQ,),
            in_specs=[
                pl.BlockSpec(memory_space=pltpu.SMEM),
                pl.BlockSpec((TQ, w), lambda i: (i, 0)),
                _resident((t, w), col_block=1),
                _resident(vbt.shape),
            ],
            out_specs=pl.BlockSpec((TQ, w), lambda i: (i, 0)),
            scratch_shapes=[pltpu.VMEM((nh, n_kb, HEAD_DIM), F32), pltpu.VMEM((nh, n_kb, TQ), F32),
                            pltpu.VMEM((nh, CK, TQ), F32)] + _attn_state(nh),
        ),
        out_shape=jax.ShapeDtypeStruct((t, w), BF16),
        compiler_params=_cparams("arbitrary"),
        name="moba",
    )(slopes, qkv, qkv, vbt)


def _mem_kernel(qc_ref, mk_ref, mvt_ref, o_ref):
    for h in range(N_HEADS_C):
        s = lax.dot_general(mk_ref[:, _head(h)], qc_ref[:, _head(h)], _NT,
                            preferred_element_type=F32)
        m = jnp.max(s, axis=0, keepdims=True)
        p = jnp.exp2(s - m)
        l = jnp.sum(p, axis=0, keepdims=True)
        acc = jnp.dot(mvt_ref[_head(h), :], p.astype(BF16), preferred_element_type=F32)
        o_ref[:, _head(h)] = (acc * (1.0 / l)).T.astype(o_ref.dtype)


def _mem_attn(qc, mk, mvt):
    t, w = qc.shape
    return pl.pallas_call(
        _mem_kernel,
        grid=(t // TQ,),
        in_specs=[pl.BlockSpec((TQ, w), lambda i: (i, 0)),
                  _resident(mk.shape), _resident(mvt.shape)],
        out_specs=pl.BlockSpec((TQ, w), lambda i: (i, 0)),
        out_shape=jax.ShapeDtypeStruct((t, w), BF16),
        compiler_params=_cparams("parallel"),
        name="mem_attn",
    )(qc, mk, mvt)


def _sigmoid(x):
    return 1.0 / (1.0 + jnp.exp(-x))


def _merge_kernel(x_ref, wga_ref, wgb_ref, wgc_ref, bga_ref, bgb_ref, bgc_ref,
                  oa_ref, ob_ref, oc_ref, wua_ref, wub_ref, wuc_ref, o_ref):
    xb = x_ref[...].astype(BF16)

    def branch(wg_ref, bg_ref, oo_ref, wu_ref):
        g = _sigmoid(jnp.dot(xb, wg_ref[...], preferred_element_type=F32) + bg_ref[...])
        return g * jnp.dot(oo_ref[...], wu_ref[...], preferred_element_type=F32)

    acc = branch(wga_ref, bga_ref, oa_ref, wua_ref)
    acc = acc + branch(wgb_ref, bgb_ref, ob_ref, wub_ref)
    acc = acc + branch(wgc_ref, bgc_ref, oc_ref, wuc_ref)
    o_ref[...] = acc.astype(o_ref.dtype)


def _merge(x, w_gate, b_gate, o_a, o_b, o_c, w_up_a, w_up_b, w_up_c, *, tm, tn):
    t, d = x.shape
    nb = d // tn
    b_gate = b_gate.reshape(1, 3 * d)

    def wg(k):
        return pl.BlockSpec((d, tn), lambda i, j, k=k: (0, j + k * nb))

    def bg(k):
        return pl.BlockSpec((1, tn), lambda i, j, k=k: (0, j + k * nb))

    def act(w):
        return pl.BlockSpec((tm, w), lambda i, j: (i, 0))

    def wu(w):
        return pl.BlockSpec((w, tn), lambda i, j: (0, j))

    return pl.pallas_call(
        _merge_kernel,
        grid=(t // tm, nb),
        in_specs=[pl.BlockSpec((tm, d), lambda i, j: (i, 0)),
                  wg(0), wg(1), wg(2), bg(0), bg(1), bg(2),
                  act(W_A), act(W_B), act(W_C), wu(W_A), wu(W_B), wu(W_C)],
        out_specs=pl.BlockSpec((tm, tn), lambda i, j: (i, j)),
        out_shape=jax.ShapeDtypeStruct((t, d), BF16),
        compiler_params=_cparams("parallel", "parallel"),
        name="merge",
    )(x, w_gate, w_gate, w_gate, b_gate, b_gate, b_gate, o_a, o_b, o_c, w_up_a, w_up_b, w_up_c)


def _layer_norm(y, g, b):
    mu = jnp.mean(y, axis=-1, keepdims=True)
    yc = y - mu
    var = jnp.mean(yc * yc, axis=-1, keepdims=True)
    return yc * lax.rsqrt(var + NORM_EPS) * g + b


def _route(logits):
    lane = lax.broadcasted_iota(I32, logits.shape, 1)
    is_g = lane < N_GROUPS
    gl = jnp.where(is_g, logits, -jnp.inf)
    gmax = jnp.max(gl, axis=-1, keepdims=True)
    g_sel = jnp.min(jnp.where(gl == gmax, lane, ROUTE_LANES), axis=-1, keepdims=True)
    p_g = 1.0 / jnp.sum(jnp.where(is_g, jnp.exp(gl - gmax), 0.0), axis=-1, keepdims=True)
    e_id = lane - N_GROUPS
    in_grp = jnp.logical_and(e_id >= g_sel * EXPERTS_PER_GROUP, e_id < (g_sel + 1) * EXPERTS_PER_GROUP)
    el = jnp.where(in_grp, logits, -jnp.inf)
    emax = jnp.max(el, axis=-1, keepdims=True)
    ex = jnp.where(in_grp, jnp.exp(el - emax), 0.0)
    pe = ex / jnp.sum(ex, axis=-1, keepdims=True)
    pe = jnp.where(in_grp, pe, -1.0)
    p1 = jnp.max(pe, axis=-1, keepdims=True)
    i1 = jnp.min(jnp.where(pe == p1, e_id, ROUTE_LANES), axis=-1, keepdims=True)
    pe2 = jnp.where(e_id == i1, -1.0, pe)
    p2 = jnp.max(pe2, axis=-1, keepdims=True)
    i2 = jnp.min(jnp.where(pe2 == p2, e_id, ROUTE_LANES), axis=-1, keepdims=True)
    denom = p1 + p2
    w1 = p_g * (p1 / denom)
    w2 = p_g * (p2 / denom)
    return jnp.where(lane == 0, w1,
                     jnp.where(lane == 1, w2,
                               jnp.where(lane == 2, i1.astype(F32),
                                         jnp.where(lane == 3, i2.astype(F32), 0.0))))


def _oproj_kernel(x_ref, m_ref, wo_ref, g_ref, b_ref, wr_ref, br_ref, x1_ref, r_ref, *, alpha):
    y = alpha * x_ref[...] + jnp.dot(m_ref[...], wo_ref[...], preferred_element_type=F32)
    x1 = _layer_norm(y, g_ref[...], b_ref[...])
    x1_ref[...] = x1
    x_hi = x1.astype(BF16)
    x_lo = (x1 - x_hi.astype(F32)).astype(BF16)
    parts = (jnp.dot(x_hi, wr_ref[...], preferred_element_type=F32)
             + jnp.dot(x_lo, wr_ref[...], preferred_element_type=F32))
    logits = parts + pltpu.roll(parts, ROUTE_LANES // 2, axis=1) + br_ref[...]
    r_ref[...] = _route(logits)


def _oproj(x, merged, w_o, ln_g, ln_b, w_r, b_r, *, alpha, tm):
    t, d = x.shape
    return pl.pallas_call(
        functools.partial(_oproj_kernel, alpha=alpha),
        grid=(t // tm,),
        in_specs=[pl.BlockSpec((tm, d), lambda i: (i, 0)),
                  pl.BlockSpec((tm, d), lambda i: (i, 0)),
                  _resident(w_o.shape),
                  pl.BlockSpec((1, d), lambda i: (0, 0)),
                  pl.BlockSpec((1, d), lambda i: (0, 0)),
                  _resident(w_r.shape),
                  pl.BlockSpec((1, ROUTE_LANES), lambda i: (0, 0))],
        out_specs=[pl.BlockSpec((tm, d), lambda i: (i, 0)),
                   pl.BlockSpec((tm, ROUTE_LANES), lambda i: (i, 0))],
        out_shape=[jax.ShapeDtypeStruct((t, d), F32),
                   jax.ShapeDtypeStruct((t, ROUTE_LANES), F32)],
        compiler_params=_cparams("parallel"),
        name="oproj",
    )(x, merged, w_o, ln_g.reshape(1, d), ln_b.reshape(1, d), w_r, b_r)


def _row_copy(src_hbm, row, dst, k, sem):
    return pltpu.make_async_copy(src_hbm.at[pl.ds(row, 1), :], dst.at[pl.ds(k, 1), :], sem)


def _expert_kernel(tok_ref, te_ref, first_ref, nxt_ref, ws_ref, ng_ref, nu_ref, x_hbm, w1_hbm, w3_hbm, w2_hbm,
                   o_ref, buf, w1b, w3b, w2b, sem, wsem, *, layer):
    i = pl.program_id(0)
    n_used = nu_ref[0]
    slot = lax.rem(i, 2)

    def gather(tile, s):
        base = tile * MOE_TM

        def body(g, c):
            r0 = pl.multiple_of(g * ROW_DMA_UNROLL, ROW_DMA_UNROLL)
            for k in range(ROW_DMA_UNROLL):
                _row_copy(x_hbm, tok_ref[base + r0 + k], buf.at[s], r0 + k, sem.at[s]).start()
            return c

        lax.fori_loop(0, ng_ref[tile], body, 0)

    def gathered(tile, s):
        def body(g, c):
            r0 = pl.multiple_of(g * ROW_DMA_UNROLL, ROW_DMA_UNROLL)
            pltpu.make_async_copy(x_hbm.at[pl.ds(0, ROW_DMA_UNROLL), :],
                                  buf.at[s, pl.ds(r0, ROW_DMA_UNROLL), :], sem.at[s]).wait()
            return c

        lax.fori_loop(0, ng_ref[tile], body, 0)

    def weight_copies(e, s):
        return [pltpu.make_async_copy(src.at[layer, e], dst.at[s], wsem.at[s])
                for src, dst in ((w1_hbm, w1b), (w3_hbm, w3b), (w2_hbm, w2b))]

    @pl.when(i == 0)
    def _():
        buf[...] = jnp.zeros_like(buf)
        gather(0, 0)
        for cp in weight_copies(te_ref[0], 0):
            cp.start(priority=BULK_DMA_PRIORITY)

    @pl.when(i < n_used)
    def _():
        ws = ws_ref[i]

        @pl.when(first_ref[i] == 1)
        def _():
            for cp in weight_copies(te_ref[i], ws):
                cp.wait()

            @pl.when(nxt_ref[i] >= 0)
            def _():
                for cp in weight_copies(nxt_ref[i], 1 - ws):
                    cp.start(priority=BULK_DMA_PRIORITY)

        gathered(i, slot)

        @pl.when(i + 1 < n_used)
        def _():
            gather(i + 1, 1 - slot)

        xs = buf[slot].astype(BF16)
        h1 = jnp.dot(xs, w1b[ws].astype(BF16), preferred_element_type=F32)
        h3 = jnp.dot(xs, w3b[ws].astype(BF16), preferred_element_type=F32)
        hid = (h1 * _sigmoid(h1)) * h3
        o_ref[...] = jnp.dot(hid.astype(BF16), w2b[ws].astype(BF16), preferred_element_type=F32)

    @pl.when(i >= n_used)
    def _():
        o_ref[...] = jnp.zeros_like(o_ref)


def _experts(plan, x1, w1, w3, w2, layer):
    r = plan[0].shape[0]
    d = x1.shape[1]
    de = w1.shape[-1]
    n_pref = len(plan)
    return pl.pallas_call(
        functools.partial(_expert_kernel, layer=layer),
        grid_spec=pltpu.PrefetchScalarGridSpec(
            num_scalar_prefetch=n_pref,
            grid=(r // MOE_TM,),
            in_specs=[pl.BlockSpec(memory_space=pl.ANY)] * 4,
            out_specs=pl.BlockSpec((MOE_TM, d), lambda i, *_: (i, 0)),
            scratch_shapes=[pltpu.VMEM((2, MOE_TM, d), F32),
                            pltpu.VMEM((2, d, de), F32), pltpu.VMEM((2, d, de), F32),
                            pltpu.VMEM((2, de, d), F32),
                            pltpu.SemaphoreType.DMA((2,)), pltpu.SemaphoreType.DMA((2,))],
        ),
        out_shape=jax.ShapeDtypeStruct((r, d), F32),
        compiler_params=_cparams("arbitrary"),
        name="moe_experts",
    )(*plan, x1, w1, w3, w2)


def _combine_kernel(pos_ref, x1_ref, route_ref, g_ref, b_ref, ys_hbm, o_ref, buf, sem, *, alpha):
    i = pl.program_id(0)
    slot = lax.rem(i, 2)

    def gather(tile, s):
        base = tile * MOE_TM

        def body(r, c):
            for k in range(2):
                _row_copy(ys_hbm, pos_ref[2 * (base + r) + k], buf.at[s, k], r, sem.at[s]).start()
            return c

        lax.fori_loop(0, MOE_TM, body, 0, unroll=ROW_DMA_UNROLL // 2)

    @pl.when(i == 0)
    def _():
        gather(0, 0)

    for k in range(2):
        pltpu.make_async_copy(ys_hbm.at[pl.ds(0, MOE_TM), :], buf.at[slot, k], sem.at[slot]).wait()

    @pl.when(i + 1 < pl.num_programs(0))
    def _():
        gather(i + 1, 1 - slot)

    route = route_ref[...]
    y = alpha * x1_ref[...] + (route[:, 0:1] * buf[slot, 0] + route[:, 1:2] * buf[slot, 1])
    o_ref[...] = _layer_norm(y, g_ref[...], b_ref[...])


def _combine(pos, x1, route, ln_g, ln_b, ys, *, alpha):
    t, d = x1.shape
    return pl.pallas_call(
        functools.partial(_combine_kernel, alpha=alpha),
        grid_spec=pltpu.PrefetchScalarGridSpec(
            num_scalar_prefetch=1,
            grid=(t // MOE_TM,),
            in_specs=[pl.BlockSpec((MOE_TM, d), lambda i, pos: (i, 0)),
                      pl.BlockSpec((MOE_TM, ROUTE_LANES), lambda i, pos: (i, 0)),
                      pl.BlockSpec((1, d), lambda i, pos: (0, 0)),
                      pl.BlockSpec((1, d), lambda i, pos: (0, 0)),
                      pl.BlockSpec(memory_space=pl.ANY)],
            out_specs=pl.BlockSpec((MOE_TM, d), lambda i, pos: (i, 0)),
            scratch_shapes=[pltpu.VMEM((2, 2, MOE_TM, d), F32), pltpu.SemaphoreType.DMA((2,))],
        ),
        out_shape=jax.ShapeDtypeStruct((t, d), F32),
        compiler_params=_cparams("arbitrary"),
        name="moe_combine",
    )(pos, x1, route, ln_g.reshape(1, d), ln_b.reshape(1, d), ys)


def _moe_plan(route, t):
    flat_e = route[:, 2:4].astype(I32).reshape(-1)
    n = flat_e.shape[0]
    n_rows = n + N_EXPERTS * MOE_TM
    n_tiles = n_rows // MOE_TM
    onehot = (flat_e[:, None] == jnp.arange(N_EXPERTS, dtype=I32)[None, :]).astype(I32)
    running = jnp.cumsum(onehot, axis=0)
    rank = jnp.sum(onehot * running, axis=1) - 1
    counts = running[-1]
    tiles_per = (counts + MOE_TM - 1) // MOE_TM
    tile_end = jnp.cumsum(tiles_per)
    grp_row0 = (tile_end - tiles_per) * MOE_TM
    pos = jnp.sum(onehot * grp_row0[None, :], axis=1) + rank
    row_token = jnp.zeros((n_rows,), I32).at[pos].set(jnp.arange(n, dtype=I32) // 2, unique_indices=True)
    n_used = tile_end[-1]
    tile_ids = jnp.minimum(jnp.arange(n_tiles, dtype=I32), n_used - 1)
    tile_expert = jnp.sum((tile_end[None, :] <= tile_ids[:, None]).astype(I32), axis=1)
    tile_expert = jnp.minimum(tile_expert, N_EXPERTS - 1)
    first = jnp.concatenate([jnp.ones((1,), I32), (tile_expert[1:] != tile_expert[:-1]).astype(I32)])
    w_slot = (jnp.cumsum(first) - 1) % 2
    e_ids = jnp.arange(N_EXPERTS, dtype=I32)
    later = jnp.logical_and(e_ids[None, :] > e_ids[:, None], tiles_per[None, :] > 0)
    next_e = jnp.min(jnp.where(later, e_ids[None, :], N_EXPERTS), axis=1)
    next_e = jnp.where(next_e == N_EXPERTS, -1, next_e)
    tile_of = (tile_expert[:, None] == e_ids[None, :]).astype(I32)
    nxt = jnp.sum(tile_of * next_e[None, :], axis=1)
    t_ids = jnp.arange(n_tiles, dtype=I32)
    rows_left = jnp.sum(tile_of * counts[None, :], axis=1) - (t_ids * MOE_TM - jnp.sum(tile_of * grp_row0[None, :], axis=1))
    rows_valid = jnp.where(t_ids < n_used, jnp.clip(rows_left, 0, MOE_TM), 0)
    n_groups = (rows_valid + ROW_DMA_UNROLL - 1) // ROW_DMA_UNROLL
    plan = (row_token, tile_expert, first, nxt.astype(I32), w_slot.astype(I32), n_groups.astype(I32),
            n_used.reshape(1).astype(I32))
    return plan, pos


def kernel(x, mem, w_mem_kv, w_in, g_cq, g_ckv, g_kidx, b_kidx, w_uq, w_uqi, w_ukv,
           w_up_a, w_up_b, w_up_c, w_gate, b_gate, w_o, ln1_g, ln1_b,
           w_grp, b_grp, w_rt, b_rt, w1, w3, w2, ln2_g, ln2_b):
    bsz, t, d = x.shape
    assert bsz == 1 and t % TQ == 0
    depth = w_in.shape[0]
    alpha = (2 * depth) ** 0.25
    scale = HEAD_DIM ** -0.5 * LOG2E
    top_k = min(TOPK_A_MAX, t // 4)
    n_mem = mem.shape[1]
    tm = min(t, 512)

    slopes = LOG2E * 2.0 ** (-8.0 * jnp.arange(1, N_ALIBI + 1, dtype=F32) / N_ALIBI)
    slopes_a, slopes_b = slopes[0::2], slopes[1::2]

    mkv = _mm(mem[0], w_mem_kv.astype(BF16), out_dtype=BF16, tm=n_mem, tn=W_C)
    mk = mkv[:, :W_C]
    mvt = mkv[:, W_C:].T

    xl = x[0]
    for l in range(depth):
        wl = w_in[l]
        o = 0
        parts = []
        for width in (Q_RANK_A, KV_RANK_A, IDX_DIM, N_IDX_HEADS, W_B, W_B, W_B, W_C):
            parts.append(wl[:, o:o + width])
            o += width
        w_cq, w_ckv, w_ki, w_wi, w_qb, w_kb, w_vb, w_qc = parts
        n_small = Q_RANK_A + KV_RANK_A + IDX_DIM + N_IDX_HEADS
        pad = (-n_small) % LANES
        w_p1 = jnp.concatenate([w_cq, w_ckv, w_ki, w_wi * N_IDX_HEADS ** -0.5,
                                jnp.zeros((d, pad), F32)], axis=1).astype(BF16)
        w_p2 = jnp.concatenate([w_qb * scale, w_kb, w_vb, w_qc * scale], axis=1).astype(BF16)
        p1 = _mm(xl, w_p1, out_dtype=F32, tm=tm, tn=w_p1.shape[1])
        p2 = _mm(xl, w_p2, out_dtype=BF16, tm=tm, tn=w_p2.shape[1] // 2)
        assert Q_RANK_A % KV_RANK_A == 0 and (Q_RANK_A + KV_RANK_A) % LANES == 0

        w_q = jnp.concatenate([w_uq[l] * scale, w_uqi[l] * IDX_DIM ** -0.5], axis=1).astype(BF16)
        qq = _mm(p1, w_q, out_dtype=BF16, tm=tm, tn=w_q.shape[1], norm_g=g_cq[l])
        kv = _mm(p1, w_ukv[l].astype(BF16), out_dtype=BF16, tm=tm, tn=2 * W_A, norm_g=g_ckv[l],
                 x_col_block=Q_RANK_A // KV_RANK_A)
        kidx, wi_t = _kidx_norm(p1, (Q_RANK_A + KV_RANK_A) // LANES, g_kidx[l], b_kidx[l])
        qi_r = (qq[:, W_A:].reshape(t // TQ, TQ, N_IDX_HEADS, IDX_DIM).transpose(0, 2, 1, 3)
                .reshape(t // TQ, N_IDX_HEADS * TQ, IDX_DIM))
        o_a = _dsa(slopes_a, qi_r, wi_t, kidx, qq, kv, kv[:, W_A:].T, top_k)
        o_b = _moba(slopes_b, p2, p2[:, 2 * W_B:3 * W_B].T)
        o_c = _mem_attn(p2[:, 3 * W_B:], mk, mvt)

        merged = _merge(xl, w_gate[l].astype(BF16), b_gate[l], o_a, o_b, o_c,
                        w_up_a[l].astype(BF16), w_up_b[l].astype(BF16), w_up_c[l].astype(BF16),
                        tm=min(t, 1024), tn=512)

        n_route = N_GROUPS + N_EXPERTS
        w_r = jnp.concatenate([w_grp[l], w_rt[l]], axis=1)
        w_r_hi = w_r.astype(BF16)
        w_r_lo = (w_r - w_r_hi.astype(F32)).astype(BF16)
        lane_pad = jnp.zeros((d, ROUTE_LANES // 2 - n_route), BF16)
        w_r2 = jnp.concatenate([w_r_hi, lane_pad, w_r_lo, lane_pad], axis=1)
        b_r = jnp.concatenate([b_grp[l], b_rt[l],
                               jnp.zeros((ROUTE_LANES - n_route,), F32)]).reshape(1, ROUTE_LANES)
        x1, route = _oproj(xl, merged, w_o[l].astype(BF16), ln1_g[l], ln1_b[l], w_r2, b_r,
                           alpha=alpha, tm=tm)

        plan, pos = _moe_plan(route, t)
        ys = _experts(plan, x1, w1, w3, w2, l)
        xl = _combine(pos, x1, route, ln2_g[l], ln2_b[l], ys, alpha=alpha)
    return xl[None]
```

```python
import functools

import jax
import jax.numpy as jnp
from jax import lax
from jax.experimental import pallas as pl
from jax.experimental.pallas import tpu as pltpu

HEAD_DIM = 128
N_HEADS_A = 6
N_HEADS_B = 6
N_HEADS_C = 4
W_A = N_HEADS_A * HEAD_DIM
W_B = N_HEADS_B * HEAD_DIM
W_C = N_HEADS_C * HEAD_DIM
Q_RANK_A = 512
KV_RANK_A = 256
N_IDX_HEADS = 16
IDX_DIM = 64
TOPK_A_MAX = 256
MOBA_BLOCK = 256
MOBA_TOPK = 3
N_ALIBI = N_HEADS_A + N_HEADS_B
N_GROUPS = 4
EXPERTS_PER_GROUP = 8
N_EXPERTS = N_GROUPS * EXPERTS_PER_GROUP
D_EXPERT = 512
NORM_EPS = 1e-5
NEG = -1e30
LOG2E = 1.4426950408889634
INT_MIN = -(2 ** 31)

LANES = 128
SUBLANES = 8
PACKED_ROWS = 2 * SUBLANES
FINE_BITS = 18
TQ = 256
CK = 256
ROUTE_LANES = 128
MOE_TM = 256
ROW_DMA_UNROLL = 8
BULK_DMA_PRIORITY = 1
VMEM_LIMIT = 56 * 1024 * 1024

F32 = jnp.float32
BF16 = jnp.bfloat16
I32 = jnp.int32

_NT = (((1,), (1,)), ((), ()))


def _cparams(*sem):
    return pltpu.CompilerParams(dimension_semantics=sem, vmem_limit_bytes=VMEM_LIMIT)


def _resident(shape, col_block=0):
    index = (0,) * (len(shape) - 1) + (col_block,)
    return pl.BlockSpec(shape, lambda *_: index, pipeline_mode=pl.Buffered(1))


def _mm_kernel(*refs, has_norm, has_bias):
    x_ref, w_ref = refs[0], refs[1]
    k = 2
    x = x_ref[...]
    if has_norm:
        g_ref = refs[k]
        k += 1
        xf = x.astype(F32)
        x = xf * lax.rsqrt(jnp.mean(xf * xf, axis=-1, keepdims=True) + NORM_EPS) * g_ref[...]
    acc = jnp.dot(x.astype(BF16), w_ref[...], preferred_element_type=F32)
    if has_bias:
        acc = acc + refs[k][...]
        k += 1
    o_ref = refs[k]
    o_ref[...] = acc.astype(o_ref.dtype)


def _mm(x, w, *, out_dtype, tm, tn, norm_g=None, bias=None, x_col_block=0):
    m = x.shape[0]
    kdim, n = w.shape
    assert m % tm == 0 and n % tn == 0
    in_specs = [pl.BlockSpec((tm, kdim), lambda i, j: (i, x_col_block)),
                pl.BlockSpec((kdim, tn), lambda i, j: (0, j))]
    args = [x, w]
    if norm_g is not None:
        in_specs.append(pl.BlockSpec((1, kdim), lambda i, j: (0, 0)))
        args.append(norm_g.reshape(1, kdim).astype(F32))
    if bias is not None:
        in_specs.append(pl.BlockSpec((1, tn), lambda i, j: (0, j)))
        args.append(bias.reshape(1, n).astype(F32))
    return pl.pallas_call(
        functools.partial(_mm_kernel, has_norm=norm_g is not None, has_bias=bias is not None),
        grid=(m // tm, n // tn),
        in_specs=in_specs,
        out_specs=pl.BlockSpec((tm, tn), lambda i, j: (i, j)),
        out_shape=jax.ShapeDtypeStruct((m, n), out_dtype),
        compiler_params=_cparams("parallel", "parallel"),
        name="mm",
    )(*args)


def _kidx_kernel(x_ref, g_ref, b_ref, k_ref, w_ref):
    blk = x_ref[...]
    x = blk[:, :IDX_DIM]
    mu = jnp.mean(x, axis=-1, keepdims=True)
    xc = x - mu
    var = jnp.mean(xc * xc, axis=-1, keepdims=True)
    k_ref[...] = (xc * lax.rsqrt(var + NORM_EPS) * g_ref[...] + b_ref[...]).astype(k_ref.dtype)
    w_ref[...] = blk.T[IDX_DIM:IDX_DIM + N_IDX_HEADS, :]


def _kidx_norm(p1, col_block, g, b):
    t = p1.shape[0]
    tm = min(t, 1024)
    return pl.pallas_call(
        _kidx_kernel,
        grid=(t // tm,),
        in_specs=[pl.BlockSpec((tm, LANES), lambda i: (i, col_block)),
                  pl.BlockSpec((1, IDX_DIM), lambda i: (0, 0)),
                  pl.BlockSpec((1, IDX_DIM), lambda i: (0, 0))],
        out_specs=[pl.BlockSpec((tm, IDX_DIM), lambda i: (i, 0)),
                   pl.BlockSpec((N_IDX_HEADS, tm), lambda i: (0, i))],
        out_shape=[jax.ShapeDtypeStruct((t, IDX_DIM), BF16),
                   jax.ShapeDtypeStruct((N_IDX_HEADS, t), F32)],
        compiler_params=_cparams("parallel"),
        name="kidx_norm",
    )(p1, g.reshape(1, IDX_DIM), b.reshape(1, IDX_DIM))


def _head(h):
    return slice(h * HEAD_DIM, (h + 1) * HEAD_DIM)


def _attn_state(n_heads):
    return ([pltpu.VMEM((CK, TQ), F32) for _ in range(2 * n_heads)]
            + [pltpu.VMEM((HEAD_DIM, TQ), F32) for _ in range(n_heads)]
            + [pltpu.VMEM((1, TQ), F32) for _ in range(2 * n_heads)])


def _attn_reset(state, n_heads):
    state = state[2 * n_heads:]
    for h in range(n_heads):
        state[h][...] = jnp.zeros((HEAD_DIM, TQ), F32)
        state[n_heads + h][...] = jnp.full((1, TQ), NEG, F32)
        state[2 * n_heads + h][...] = jnp.zeros((1, TQ), F32)


def _attn_run(state, n_heads, i, q_ref, k_ref, consume):
    def scores(j, slot):
        start = pl.multiple_of(j * CK, CK)
        for h in range(n_heads):
            state[slot * n_heads + h][...] = lax.dot_general(
                k_ref[pl.ds(start, CK), _head(h)], q_ref[:, _head(h)], _NT,
                preferred_element_type=F32)

    scores(0, 0)

    def pair(p, c):
        j = 2 * p
        scores(j + 1, 1)
        consume(j, 0, False)
        scores(j + 2, 0)
        consume(j + 1, 1, False)
        return c

    lax.fori_loop(0, i // 2, pair, 0)

    @pl.when(i % 2 == 0)
    def _():
        consume(i, 0, True)

    @pl.when(i % 2 == 1)
    def _():
        scores(i, 1)
        consume(i - 1, 0, False)
        consume(i, 1, True)


def _attn_update(state, n_heads, h, slot, vt_blk, t_bias, c):
    s_ref, state = state[slot * n_heads + h], state[2 * n_heads:]
    acc_ref, m_ref, l_ref = state[h], state[n_heads + h], state[2 * n_heads + h]
    t = s_ref[...] + t_bias
    m_old = m_ref[...]
    m_new = jnp.maximum(m_old, jnp.max(t, axis=0, keepdims=True) + c)
    alpha = jnp.exp2(m_old - m_new)
    p = jnp.exp2(t - (m_new - c))
    l_ref[...] = alpha * l_ref[...] + jnp.sum(p, axis=0, keepdims=True)
    acc_ref[...] = alpha * acc_ref[...] + jnp.dot(vt_blk, p.astype(BF16), preferred_element_type=F32)
    m_ref[...] = m_new


def _attn_finish(state, n_heads, h):
    state = state[2 * n_heads:]
    return (state[h][...] * (1.0 / state[2 * n_heads + h][...])).T


def _key_offsets():
    return lax.broadcasted_iota(I32, (CK, TQ), 0), lax.broadcasted_iota(I32, (CK, TQ), 1)


def _from_sortable(code):
    return pltpu.bitcast(code ^ ((code >> 31) & 0x7FFFFFFF), F32)


def _dsa_kernel(slopes_ref, qi_ref, wi_ref, kidx_ref, qa_ref, ka_ref, vat_ref, o_ref,
                keys_ref, half_ref, sb_ref, *state, top_k, n_idx_bits):
    i = pl.program_id(0)
    rows, cols = _key_offsets()
    causal = rows <= cols
    nh = N_HEADS_A

    @pl.when(i == 0)
    def _():
        rows_f = rows.astype(F32)
        for h in range(nh):
            sb_ref[h] = slopes_ref[h] * rows_f

    def score_chunk(j, diagonal):
        start = pl.multiple_of(j * CK, CK)
        kc = kidx_ref[pl.ds(start, CK), :]
        acc = jnp.zeros((CK, TQ), F32)
        for h in range(N_IDX_HEADS):
            z = lax.dot_general(kc, qi_ref[0, h * TQ:(h + 1) * TQ, :], _NT,
                                preferred_element_type=F32)
            acc = acc + wi_ref[h:h + 1, :] * jnp.maximum(z, 0.0)
        if diagonal:
            acc = jnp.where(causal, acc, NEG)
        keys_ref[pl.ds(start, CK), :] = acc
        half_ref[pl.ds(start, CK), :] = acc.astype(BF16)

    def score_body(j, c):
        score_chunk(j, False)
        return c

    lax.fori_loop(0, i, score_body, 0)
    score_chunk(i, True)

    def count32(pred):
        def body(j, cnt):
            start = pl.multiple_of(j * CK, CK)
            hit = jnp.where(pred(keys_ref[pl.ds(start, CK), :], rows + j * CK), 1, 0)
            return cnt + jnp.sum(hit.reshape(CK // SUBLANES, SUBLANES, TQ), axis=0)
        cnt = lax.fori_loop(0, i + 1, body, jnp.zeros((SUBLANES, TQ), I32))
        return jnp.sum(cnt, axis=0, keepdims=True)

    def count16(cand_bf):
        def body(j, cnt):
            blk = half_ref[pl.ds(pl.multiple_of(j * CK, CK), CK), :]
            hit = jnp.where(blk >= cand_bf, jnp.int16(1), jnp.int16(0))
            for r in range(0, CK, PACKED_ROWS):
                cnt = cnt + hit[r:r + PACKED_ROWS, :]
            return cnt
        cnt = lax.fori_loop(0, i + 1, body, jnp.zeros((PACKED_ROWS, TQ), jnp.int16))
        return jnp.sum(cnt.astype(I32), axis=0, keepdims=True)

    def coarse_bit(b, carry):
        c16, n16 = carry
        cand = c16 + jnp.left_shift(jnp.int32(1), 15 - b)
        bits = cand ^ ((cand >> 31) & 0x7FFF)
        n_cand = count16(pltpu.bitcast(bits.astype(jnp.int16), BF16))
        ok = n_cand >= top_k
        return jnp.where(ok, cand, c16), jnp.where(ok, n_cand, n16)

    c16, n16 = lax.fori_loop(0, 16, coarse_bit, (jnp.full((1, TQ), -(2 ** 15), I32),
                                                  jnp.zeros((1, TQ), I32) + (i + 1) * CK))

    base = (c16 - 1) << 16

    def fine_bit(b, carry):
        off, n_code = carry
        cand = off + jnp.left_shift(jnp.int32(1), FINE_BITS - 1 - b)
        cand_f = _from_sortable(base + cand)
        n_cand = count32(lambda score, idx: score >= cand_f)
        ok = n_cand >= top_k
        return jnp.where(ok, cand, off), jnp.where(ok, n_cand, n_code)

    off, n_ge = lax.fori_loop(0, FINE_BITS, fine_bit, (jnp.zeros((1, TQ), I32), n16))
    thr = _from_sortable(base + off)

    @pl.when(jnp.max(n_ge) > top_k)
    def _():
        keep = top_k - count32(lambda score, idx: score > thr)

        def idx_bit(b, last):
            cand = last + jnp.left_shift(jnp.int32(1), n_idx_bits - 1 - b)
            below = count32(lambda score, idx: jnp.logical_and(score == thr, idx < cand))
            return jnp.where(below < keep, cand, last)

        last = lax.fori_loop(0, n_idx_bits, idx_bit, jnp.zeros((1, TQ), I32))

        def drop(j, c):
            start = pl.multiple_of(j * CK, CK)
            score = keys_ref[pl.ds(start, CK), :]
            late_tie = jnp.logical_and(score == thr, rows + j * CK > last)
            keys_ref[pl.ds(start, CK), :] = jnp.where(late_tie, -jnp.inf, score)
            return c

        lax.fori_loop(0, i + 1, drop, 0)

    _attn_reset(state, nh)

    def consume(j, slot, diagonal):
        start = pl.multiple_of(j * CK, CK)
        sel = keys_ref[pl.ds(start, CK), :] >= thr
        if diagonal:
            sel = jnp.logical_and(sel, causal)
        mask_bias = jnp.where(sel, 0.0, NEG)
        off = ((j - i) * CK).astype(F32)
        for h in range(nh):
            _attn_update(state, nh, h, slot, vat_ref[_head(h), pl.ds(start, CK)],
                         sb_ref[h] + mask_bias, slopes_ref[h] * off)

    _attn_run(state, nh, i, qa_ref, ka_ref, consume)
    for h in range(nh):
        o_ref[:, _head(h)] = _attn_finish(state, nh, h).astype(o_ref.dtype)


def _dsa(slopes, qi_r, wi_t, kidx, qa, ka, vat, top_k):
    w, t = vat.shape
    nh = w // HEAD_DIM
    return pl.pallas_call(
        functools.partial(_dsa_kernel, top_k=top_k, n_idx_bits=max(1, (t - 1).bit_length())),
        grid_spec=pltpu.PrefetchScalarGridSpec(
            num_scalar_prefetch=0,
            grid=(t // TQ,),
            in_specs=[
                pl.BlockSpec(memory_space=pltpu.SMEM),
                pl.BlockSpec((1, N_IDX_HEADS * TQ, IDX_DIM), lambda i: (i, 0, 0)),
                pl.BlockSpec((N_IDX_HEADS, TQ), lambda i: (0, i)),
                _resident(kidx.shape),
                pl.BlockSpec((TQ, w), lambda i: (i, 0)),
                _resident((t, w)),
                _resident(vat.shape),
            ],
            out_specs=pl.BlockSpec((TQ, w), lambda i: (i, 0)),
            scratch_shapes=[pltpu.VMEM((t, TQ), F32), pltpu.VMEM((t, TQ), BF16),
                            pltpu.VMEM((nh, CK, TQ), F32)] + _attn_state(nh),
        ),
        out_shape=jax.ShapeDtypeStruct((t, w), BF16),
        compiler_params=_cparams("arbitrary"),
        name="dsa",
    )(slopes, qi_r, wi_t, kidx, qa, ka, vat)


def _moba_kernel(slopes_ref, qb_ref, kb_ref, vbt_ref, o_ref, kmean_ref, sel_ref, sb_ref, *state, n_kb):
    i = pl.program_id(0)
    rows, cols = _key_offsets()
    nh = N_HEADS_B

    @pl.when(i == 0)
    def _():
        rows_f = rows.astype(F32)
        for h in range(nh):
            sb_ref[h] = slopes_ref[h] * rows_f
            for n in range(n_kb):
                blk = kb_ref[n * CK:(n + 1) * CK, _head(h)].astype(F32)
                kmean_ref[h, n:n + 1, :] = jnp.mean(blk, axis=0, keepdims=True)

    blk_id = lax.broadcasted_iota(I32, (n_kb, TQ), 0)
    for h in range(nh):
        q_h = qb_ref[:, _head(h)]
        km = kmean_ref[h]
        km_hi = km.astype(BF16)
        km_lo = (km - km_hi.astype(F32)).astype(BF16)
        gate = (lax.dot_general(km_hi, q_h, _NT, preferred_element_type=F32)
                + lax.dot_general(km_lo, q_h, _NT, preferred_element_type=F32))
        gate = jnp.where(blk_id < i, gate, NEG)
        sel = jnp.full((n_kb, TQ), NEG, F32)
        for _ in range(MOBA_TOPK):
            best = jnp.max(gate, axis=0, keepdims=True)
            first = jnp.min(jnp.where(gate == best, blk_id, n_kb), axis=0, keepdims=True)
            pick = blk_id == first
            sel = jnp.where(pick, 0.0, sel)
            gate = jnp.where(pick, -jnp.inf, gate)
        sel_ref[h] = jnp.where(blk_id < i, sel, NEG)

    _attn_reset(state, nh)

    def consume(j, slot, diagonal):
        start = pl.multiple_of(j * CK, CK)
        off = ((j - i) * CK).astype(F32)
        for h in range(nh):
            if diagonal:
                t_bias = jnp.where(rows <= cols, sb_ref[h], NEG)
            else:
                t_bias = sb_ref[h] + sel_ref[h, pl.ds(j, 1), :]
            _attn_update(state, nh, h, slot, vbt_ref[_head(h), pl.ds(start, CK)],
                         t_bias, slopes_ref[h] * off)

    _attn_run(state, nh, i, qb_ref, kb_ref, consume)
    for h in range(nh):
        o_ref[:, _head(h)] = _attn_finish(state, nh, h).astype(o_ref.dtype)


def _moba(slopes, qkv, vbt):
    w, t = vbt.shape
    nh = w // HEAD_DIM
    assert t % MOBA_BLOCK == 0 and TQ == MOBA_BLOCK and CK == MOBA_BLOCK
    n_kb = t // MOBA_BLOCK
    return pl.pallas_call(
        functools.partial(_moba_kernel, n_kb=n_kb),
        grid_spec=pltpu.PrefetchScalarGridSpec(
            num_scalar_prefetch=0,
            grid=(t // TQ,),
            in_specs=[
                pl.BlockSpec(memory_space=pltpu.SMEM),
                pl.BlockSpec((TQ, w), lambda i: (i, 0)),
                _resident((t, w), col_block=1),
                _resident(vbt.shape),
            ],
            out_specs=pl.BlockSpec((TQ, w), lambda i: (i, 0)),
            scratch_shapes=[pltpu.VMEM((nh, n_kb, HEAD_DIM), F32), pltpu.VMEM((nh, n_kb, TQ), F32),
                            pltpu.VMEM((nh, CK, TQ), F32)] + _attn_state(nh),
        ),
        out_shape=jax.ShapeDtypeStruct((t, w), BF16),
        compiler_params=_cparams("arbitrary"),
        name="moba",
    )(slopes, qkv, qkv, vbt)


def _mem_kernel(qc_ref, mk_ref, mvt_ref, o_ref):
    for h in range(N_HEADS_C):
        s = lax.dot_general(mk_ref[:, _head(h)], qc_ref[:, _head(h)], _NT,
                            preferred_element_type=F32)
        m = jnp.max(s, axis=0, keepdims=True)
        p = jnp.exp2(s - m)
        l = jnp.sum(p, axis=0, keepdims=True)
        acc = jnp.dot(mvt_ref[_head(h), :], p.astype(BF16), preferred_element_type=F32)
        o_ref[:, _head(h)] = (acc * (1.0 / l)).T.astype(o_ref.dtype)


def _mem_attn(qc, mk, mvt):
    t, w = qc.shape
    return pl.pallas_call(
        _mem_kernel,
        grid=(t // TQ,),
        in_specs=[pl.BlockSpec((TQ, w), lambda i: (i, 0)),
                  _resident(mk.shape), _resident(mvt.shape)],
        out_specs=pl.BlockSpec((TQ, w), lambda i: (i, 0)),
        out_shape=jax.ShapeDtypeStruct((t, w), BF16),
        compiler_params=_cparams("parallel"),
        name="mem_attn",
    )(qc, mk, mvt)


def _sigmoid(x):
    return 1.0 / (1.0 + jnp.exp(-x))


def _merge_kernel(x_ref, wga_ref, wgb_ref, wgc_ref, bga_ref, bgb_ref, bgc_ref,
                  oa_ref, ob_ref, oc_ref, wua_ref, wub_ref, wuc_ref, o_ref):
    xb = x_ref[...].astype(BF16)

    def branch(wg_ref, bg_ref, oo_ref, wu_ref):
        g = _sigmoid(jnp.dot(xb, wg_ref[...], preferred_element_type=F32) + bg_ref[...])
        return g * jnp.dot(oo_ref[...], wu_ref[...], preferred_element_type=F32)

    acc = branch(wga_ref, bga_ref, oa_ref, wua_ref)
    acc = acc + branch(wgb_ref, bgb_ref, ob_ref, wub_ref)
    acc = acc + branch(wgc_ref, bgc_ref, oc_ref, wuc_ref)
    o_ref[...] = acc.astype(o_ref.dtype)


def _merge(x, w_gate, b_gate, o_a, o_b, o_c, w_up_a, w_up_b, w_up_c, *, tm, tn):
    t, d = x.shape
    nb = d // tn
    b_gate = b_gate.reshape(1, 3 * d)

    def wg(k):
        return pl.BlockSpec((d, tn), lambda i, j, k=k: (0, j + k * nb))

    def bg(k):
        return pl.BlockSpec((1, tn), lambda i, j, k=k: (0, j + k * nb))

    def act(w):
        return pl.BlockSpec((tm, w), lambda i, j: (i, 0))

    def wu(w):
        return pl.BlockSpec((w, tn), lambda i, j: (0, j))

    return pl.pallas_call(
        _merge_kernel,
        grid=(t // tm, nb),
        in_specs=[pl.BlockSpec((tm, d), lambda i, j: (i, 0)),
                  wg(0), wg(1), wg(2), bg(0), bg(1), bg(2),
                  act(W_A), act(W_B), act(W_C), wu(W_A), wu(W_B), wu(W_C)],
        out_specs=pl.BlockSpec((tm, tn), lambda i, j: (i, j)),
        out_shape=jax.ShapeDtypeStruct((t, d), BF16),
        compiler_params=_cparams("parallel", "parallel"),
        name="merge",
    )(x, w_gate, w_gate, w_gate, b_gate, b_gate, b_gate, o_a, o_b, o_c, w_up_a, w_up_b, w_up_c)


def _layer_norm(y, g, b):
    mu = jnp.mean(y, axis=-1, keepdims=True)
    yc = y - mu
    var = jnp.mean(yc * yc, axis=-1, keepdims=True)
    return yc * lax.rsqrt(var + NORM_EPS) * g + b


def _route(logits):
    lane = lax.broadcasted_iota(I32, logits.shape, 1)
    is_g = lane < N_GROUPS
    gl = jnp.where(is_g, logits, -jnp.inf)
    gmax = jnp.max(gl, axis=-1, keepdims=True)
    g_sel = jnp.min(jnp.where(gl == gmax, lane, ROUTE_LANES), axis=-1, keepdims=True)
    p_g = 1.0 / jnp.sum(jnp.where(is_g, jnp.exp(gl - gmax), 0.0), axis=-1, keepdims=True)
    e_id = lane - N_GROUPS
    in_grp = jnp.logical_and(e_id >= g_sel * EXPERTS_PER_GROUP, e_id < (g_sel + 1) * EXPERTS_PER_GROUP)
    el = jnp.where(in_grp, logits, -jnp.inf)
    emax = jnp.max(el, axis=-1, keepdims=True)
    ex = jnp.where(in_grp, jnp.exp(el - emax), 0.0)
    pe = ex / jnp.sum(ex, axis=-1, keepdims=True)
    pe = jnp.where(in_grp, pe, -1.0)
    p1 = jnp.max(pe, axis=-1, keepdims=True)
    i1 = jnp.min(jnp.where(pe == p1, e_id, ROUTE_LANES), axis=-1, keepdims=True)
    pe2 = jnp.where(e_id == i1, -1.0, pe)
    p2 = jnp.max(pe2, axis=-1, keepdims=True)
    i2 = jnp.min(jnp.where(pe2 == p2, e_id, ROUTE_LANES), axis=-1, keepdims=True)
    denom = p1 + p2
    w1 = p_g * (p1 / denom)
    w2 = p_g * (p2 / denom)
    return jnp.where(lane == 0, w1,
                     jnp.where(lane == 1, w2,
                               jnp.where(lane == 2, i1.astype(F32),
                                         jnp.where(lane == 3, i2.astype(F32), 0.0))))


def _oproj_kernel(x_ref, m_ref, wo_ref, g_ref, b_ref, wr_ref, br_ref, x1_ref, r_ref, *, alpha):
    y = alpha * x_ref[...] + jnp.dot(m_ref[...], wo_ref[...], preferred_element_type=F32)
    x1 = _layer_norm(y, g_ref[...], b_ref[...])
    x1_ref[...] = x1
    x_hi = x1.astype(BF16)
    x_lo = (x1 - x_hi.astype(F32)).astype(BF16)
    parts = (jnp.dot(x_hi, wr_ref[...], preferred_element_type=F32)
             + jnp.dot(x_lo, wr_ref[...], preferred_element_type=F32))
    logits = parts + pltpu.roll(parts, ROUTE_LANES // 2, axis=1) + br_ref[...]
    r_ref[...] = _route(logits)


def _oproj(x, merged, w_o, ln_g, ln_b, w_r, b_r, *, alpha, tm):
    t, d = x.shape
    return pl.pallas_call(
        functools.partial(_oproj_kernel, alpha=alpha),
        grid=(t // tm,),
        in_specs=[pl.BlockSpec((tm, d), lambda i: (i, 0)),
                  pl.BlockSpec((tm, d), lambda i: (i, 0)),
                  _resident(w_o.shape),
                  pl.BlockSpec((1, d), lambda i: (0, 0)),
                  pl.BlockSpec((1, d), lambda i: (0, 0)),
                  _resident(w_r.shape),
                  pl.BlockSpec((1, ROUTE_LANES), lambda i: (0, 0))],
        out_specs=[pl.BlockSpec((tm, d), lambda i: (i, 0)),
                   pl.BlockSpec((tm, ROUTE_LANES), lambda i: (i, 0))],
        out_shape=[jax.ShapeDtypeStruct((t, d), F32),
                   jax.ShapeDtypeStruct((t, ROUTE_LANES), F32)],
        compiler_params=_cparams("parallel"),
        name="oproj",
    )(x, merged, w_o, ln_g.reshape(1, d), ln_b.reshape(1, d), w_r, b_r)


def _row_copy(src_hbm, row, dst, k, sem):
    return pltpu.make_async_copy(src_hbm.at[pl.ds(row, 1), :], dst.at[pl.ds(k, 1), :], sem)


def _expert_kernel(tok_ref, te_ref, first_ref, nxt_ref, ws_ref, ng_ref, nu_ref, x_hbm, w1_hbm, w3_hbm, w2_hbm,
                   o_ref, buf, w1b, w3b, w2b, sem, wsem, *, layer):
    i = pl.program_id(0)
    n_used = nu_ref[0]
    slot = lax.rem(i, 2)

    def gather(tile, s):
        base = tile * MOE_TM

        def body(g, c):
            r0 = pl.multiple_of(g * ROW_DMA_UNROLL, ROW_DMA_UNROLL)
            for k in range(ROW_DMA_UNROLL):
                _row_copy(x_hbm, tok_ref[base + r0 + k], buf.at[s], r0 + k, sem.at[s]).start()
            return c

        lax.fori_loop(0, ng_ref[tile], body, 0)

    def gathered(tile, s):
        def body(g, c):
            r0 = pl.multiple_of(g * ROW_DMA_UNROLL, ROW_DMA_UNROLL)
            pltpu.make_async_copy(x_hbm.at[pl.ds(0, ROW_DMA_UNROLL), :],
                                  buf.at[s, pl.ds(r0, ROW_DMA_UNROLL), :], sem.at[s]).wait()
            return c

        lax.fori_loop(0, ng_ref[tile], body, 0)

    def weight_copies(e, s):
        return [pltpu.make_async_copy(src.at[layer, e], dst.at[s], wsem.at[s])
                for src, dst in ((w1_hbm, w1b), (w3_hbm, w3b), (w2_hbm, w2b))]

    @pl.when(i == 0)
    def _():
        buf[...] = jnp.zeros_like(buf)
        gather(0, 0)
        for cp in weight_copies(te_ref[0], 0):
            cp.start(priority=BULK_DMA_PRIORITY)

    @pl.when(i < n_used)
    def _():
        ws = ws_ref[i]

        @pl.when(first_ref[i] == 1)
        def _():
            for cp in weight_copies(te_ref[i], ws):
                cp.wait()

            @pl.when(nxt_ref[i] >= 0)
            def _():
                for cp in weight_copies(nxt_ref[i], 1 - ws):
                    cp.start(priority=BULK_DMA_PRIORITY)

        gathered(i, slot)

        @pl.when(i + 1 < n_used)
        def _():
            gather(i + 1, 1 - slot)

        xs = buf[slot].astype(BF16)
        h1 = jnp.dot(xs, w1b[ws].astype(BF16), preferred_element_type=F32)
        h3 = jnp.dot(xs, w3b[ws].astype(BF16), preferred_element_type=F32)
        hid = (h1 * _sigmoid(h1)) * h3
        o_ref[...] = jnp.dot(hid.astype(BF16), w2b[ws].astype(BF16), preferred_element_type=F32)

    @pl.when(i >= n_used)
    def _():
        o_ref[...] = jnp.zeros_like(o_ref)


def _experts(plan, x1, w1, w3, w2, layer):
    r = plan[0].shape[0]
    d = x1.shape[1]
    de = w1.shape[-1]
    n_pref = len(plan)
    return pl.pallas_call(
        functools.partial(_expert_kernel, layer=layer),
        grid_spec=pltpu.PrefetchScalarGridSpec(
            num_scalar_prefetch=n_pref,
            grid=(r // MOE_TM,),
            in_specs=[pl.BlockSpec(memory_space=pl.ANY)] * 4,
            out_specs=pl.BlockSpec((MOE_TM, d), lambda i, *_: (i, 0)),
            scratch_shapes=[pltpu.VMEM((2, MOE_TM, d), F32),
                            pltpu.VMEM((2, d, de), F32), pltpu.VMEM((2, d, de), F32),
                            pltpu.VMEM((2, de, d), F32),
                            pltpu.SemaphoreType.DMA((2,)), pltpu.SemaphoreType.DMA((2,))],
        ),
        out_shape=jax.ShapeDtypeStruct((r, d), F32),
        compiler_params=_cparams("arbitrary"),
        name="moe_experts",
    )(*plan, x1, w1, w3, w2)


def _combine_kernel(pos_ref, x1_ref, route_ref, g_ref, b_ref, ys_hbm, o_ref, buf, sem, *, alpha):
    i = pl.program_id(0)
    slot = lax.rem(i, 2)

    def gather(tile, s):
        base = tile * MOE_TM

        def body(r, c):
            for k in range(2):
                _row_copy(ys_hbm, pos_ref[2 * (base + r) + k], buf.at[s, k], r, sem.at[s]).start()
            return c

        lax.fori_loop(0, MOE_TM, body, 0, unroll=ROW_DMA_UNROLL // 2)

    @pl.when(i == 0)
    def _():
        gather(0, 0)

    for k in range(2):
        pltpu.make_async_copy(ys_hbm.at[pl.ds(0, MOE_TM), :], buf.at[slot, k], sem.at[slot]).wait()

    @pl.when(i + 1 < pl.num_programs(0))
    def _():
        gather(i + 1, 1 - slot)

    route = route_ref[...]
    y = alpha * x1_ref[...] + (route[:, 0:1] * buf[slot, 0] + route[:, 1:2] * buf[slot, 1])
    o_ref[...] = _layer_norm(y, g_ref[...], b_ref[...])


def _combine(pos, x1, route, ln_g, ln_b, ys, *, alpha):
    t, d = x1.shape
    return pl.pallas_call(
        functools.partial(_combine_kernel, alpha=alpha),
        grid_spec=pltpu.PrefetchScalarGridSpec(
            num_scalar_prefetch=1,
            grid=(t // MOE_TM,),
            in_specs=[pl.BlockSpec((MOE_TM, d), lambda i, pos: (i, 0)),
                      pl.BlockSpec((MOE_TM, ROUTE_LANES), lambda i, pos: (i, 0)),
                      pl.BlockSpec((1, d), lambda i, pos: (0, 0)),
                      pl.BlockSpec((1, d), lambda i, pos: (0, 0)),
                      pl.BlockSpec(memory_space=pl.ANY)],
            out_specs=pl.BlockSpec((MOE_TM, d), lambda i, pos: (i, 0)),
            scratch_shapes=[pltpu.VMEM((2, 2, MOE_TM, d), F32), pltpu.SemaphoreType.DMA((2,))],
        ),
        out_shape=jax.ShapeDtypeStruct((t, d), F32),
        compiler_params=_cparams("arbitrary"),
        name="moe_combine",
    )(pos, x1, route, ln_g.reshape(1, d), ln_b.reshape(1, d), ys)


def _moe_plan(route, t):
    flat_e = route[:, 2:4].astype(I32).reshape(-1)
    n = flat_e.shape[0]
    n_rows = n + N_EXPERTS * MOE_TM
    n_tiles = n_rows // MOE_TM
    onehot = (flat_e[:, None] == jnp.arange(N_EXPERTS, dtype=I32)[None, :]).astype(I32)
    running = jnp.cumsum(onehot, axis=0)
    rank = jnp.sum(onehot * running, axis=1) - 1
    counts = running[-1]
    tiles_per = (counts + MOE_TM - 1) // MOE_TM
    tile_end = jnp.cumsum(tiles_per)
    grp_row0 = (tile_end - tiles_per) * MOE_TM
    pos = jnp.sum(onehot * grp_row0[None, :], axis=1) + rank
    row_token = jnp.zeros((n_rows,), I32).at[pos].set(jnp.arange(n, dtype=I32) // 2, unique_indices=True)
    n_used = tile_end[-1]
    tile_ids = jnp.minimum(jnp.arange(n_tiles, dtype=I32), n_used - 1)
    tile_expert = jnp.sum((tile_end[None, :] <= tile_ids[:, None]).astype(I32), axis=1)
    tile_expert = jnp.minimum(tile_expert, N_EXPERTS - 1)
    first = jnp.concatenate([jnp.ones((1,), I32), (tile_expert[1:] != tile_expert[:-1]).astype(I32)])
    w_slot = (jnp.cumsum(first) - 1) % 2
    e_ids = jnp.arange(N_EXPERTS, dtype=I32)
    later = jnp.logical_and(e_ids[None, :] > e_ids[:, None], tiles_per[None, :] > 0)
    next_e = jnp.min(jnp.where(later, e_ids[None, :], N_EXPERTS), axis=1)
    next_e = jnp.where(next_e == N_EXPERTS, -1, next_e)
    tile_of = (tile_expert[:, None] == e_ids[None, :]).astype(I32)
    nxt = jnp.sum(tile_of * next_e[None, :], axis=1)
    t_ids = jnp.arange(n_tiles, dtype=I32)
    rows_left = jnp.sum(tile_of * counts[None, :], axis=1) - (t_ids * MOE_TM - jnp.sum(tile_of * grp_row0[None, :], axis=1))
    rows_valid = jnp.where(t_ids < n_used, jnp.clip(rows_left, 0, MOE_TM), 0)
    n_groups = (rows_valid + ROW_DMA_UNROLL - 1) // ROW_DMA_UNROLL
    plan = (row_token, tile_expert, first, nxt.astype(I32), w_slot.astype(I32), n_groups.astype(I32),
            n_used.reshape(1).astype(I32))
    return plan, pos


def kernel(x, mem, w_mem_kv, w_in, g_cq, g_ckv, g_kidx, b_kidx, w_uq, w_uqi, w_ukv,
           w_up_a, w_up_b, w_up_c, w_gate, b_gate, w_o, ln1_g, ln1_b,
           w_grp, b_grp, w_rt, b_rt, w1, w3, w2, ln2_g, ln2_b):
    bsz, t, d = x.shape
    assert bsz == 1 and t % TQ == 0
    depth = w_in.shape[0]
    alpha = (2 * depth) ** 0.25
    scale = HEAD_DIM ** -0.5 * LOG2E
    top_k = min(TOPK_A_MAX, t // 4)
    n_mem = mem.shape[1]
    tm = min(t, 512)

    slopes = LOG2E * 2.0 ** (-8.0 * jnp.arange(1, N_ALIBI + 1, dtype=F32) / N_ALIBI)
    slopes_a, slopes_b = slopes[0::2], slopes[1::2]

    mkv = _mm(mem[0], w_mem_kv.astype(BF16), out_dtype=BF16, tm=n_mem, tn=W_C)
    mk = mkv[:, :W_C]
    mvt = mkv[:, W_C:].T

    xl = x[0]
    for l in range(depth):
        wl = w_in[l]
        o = 0
        parts = []
        for width in (Q_RANK_A, KV_RANK_A, IDX_DIM, N_IDX_HEADS, W_B, W_B, W_B, W_C):
            parts.append(wl[:, o:o + width])
            o += width
        w_cq, w_ckv, w_ki, w_wi, w_qb, w_kb, w_vb, w_qc = parts
        n_small = Q_RANK_A + KV_RANK_A + IDX_DIM + N_IDX_HEADS
        pad = (-n_small) % LANES
        w_p1 = jnp.concatenate([w_cq, w_ckv, w_ki, w_wi * N_IDX_HEADS ** -0.5,
                                jnp.zeros((d, pad), F32)], axis=1).astype(BF16)
        w_p2 = jnp.concatenate([w_qb * scale, w_kb, w_vb, w_qc * scale], axis=1).astype(BF16)
        p1 = _mm(xl, w_p1, out_dtype=F32, tm=tm, tn=w_p1.shape[1])
        p2 = _mm(xl, w_p2, out_dtype=BF16, tm=tm, tn=w_p2.shape[1] // 2)
        assert Q_RANK_A % KV_RANK_A == 0 and (Q_RANK_A + KV_RANK_A) % LANES == 0

        w_q = jnp.concatenate([w_uq[l] * scale, w_uqi[l] * IDX_DIM ** -0.5], axis=1).astype(BF16)
        qq = _mm(p1, w_q, out_dtype=BF16, tm=tm, tn=w_q.shape[1], norm_g=g_cq[l])
        kv = _mm(p1, w_ukv[l].astype(BF16), out_dtype=BF16, tm=tm, tn=2 * W_A, norm_g=g_ckv[l],
                 x_col_block=Q_RANK_A // KV_RANK_A)
        kidx, wi_t = _kidx_norm(p1, (Q_RANK_A + KV_RANK_A) // LANES, g_kidx[l], b_kidx[l])
        qi_r = (qq[:, W_A:].reshape(t // TQ, TQ, N_IDX_HEADS, IDX_DIM).transpose(0, 2, 1, 3)
                .reshape(t // TQ, N_IDX_HEADS * TQ, IDX_DIM))
        o_a = _dsa(slopes_a, qi_r, wi_t, kidx, qq, kv, kv[:, W_A:].T, top_k)
        o_b = _moba(slopes_b, p2, p2[:, 2 * W_B:3 * W_B].T)
        o_c = _mem_attn(p2[:, 3 * W_B:], mk, mvt)

        merged = _merge(xl, w_gate[l].astype(BF16), b_gate[l], o_a, o_b, o_c,
                        w_up_a[l].astype(BF16), w_up_b[l].astype(BF16), w_up_c[l].astype(BF16),
                        tm=min(t, 1024), tn=512)

        n_route = N_GROUPS + N_EXPERTS
        w_r = jnp.concatenate([w_grp[l], w_rt[l]], axis=1)
        w_r_hi = w_r.astype(BF16)
        w_r_lo = (w_r - w_r_hi.astype(F32)).astype(BF16)
        lane_pad = jnp.zeros((d, ROUTE_LANES // 2 - n_route), BF16)
        w_r2 = jnp.concatenate([w_r_hi, lane_pad, w_r_lo, lane_pad], axis=1)
        b_r = jnp.concatenate([b_grp[l], b_rt[l],
                               jnp.zeros((ROUTE_LANES - n_route,), F32)]).reshape(1, ROUTE_LANES)
        x1, route = _oproj(xl, merged, w_o[l].astype(BF16), ln1_g[l], ln1_b[l], w_r2, b_r,
                           alpha=alpha, tm=tm)

        plan, pos = _moe_plan(route, t)
        ys = _experts(plan, x1, w1, w3, w2, l)
        xl = _combine(pos, x1, route, ln2_g[l], ln2_b[l], ys, alpha=alpha)
    return xl[None]
```

```python
import functools

import jax
import jax.numpy as jnp
from jax import lax
from jax.experimental import pallas as pl
from jax.experimental.pallas import tpu as pltpu

HEAD_DIM = 128
N_HEADS_A = 6
N_HEADS_B = 6
N_HEADS_C = 4
W_A = N_HEADS_A * HEAD_DIM
W_B = N_HEADS_B * HEAD_DIM
W_C = N_HEADS_C * HEAD_DIM
Q_RANK_A = 512
KV_RANK_A = 256
N_IDX_HEADS = 16
IDX_DIM = 64
TOPK_A_MAX = 256
MOBA_BLOCK = 256
MOBA_TOPK = 3
N_ALIBI = N_HEADS_A + N_HEADS_B
N_GROUPS = 4
EXPERTS_PER_GROUP = 8
N_EXPERTS = N_GROUPS * EXPERTS_PER_GROUP
D_EXPERT = 512
NORM_EPS = 1e-5
NEG = -1e30
LOG2E = 1.4426950408889634
INT_MIN = -(2 ** 31)

LANES = 128
SUBLANES = 8
PACKED_ROWS = 2 * SUBLANES
FINE_BITS = 18
TQ = 256
CK = 256
ROUTE_LANES = 128
MOE_TM = 256
ROW_DMA_UNROLL = 8
BULK_DMA_PRIORITY = 1
VMEM_LIMIT = 56 * 1024 * 1024

F32 = jnp.float32
BF16 = jnp.bfloat16
I32 = jnp.int32

_NT = (((1,), (1,)), ((), ()))


def _cparams(*sem):
    return pltpu.CompilerParams(dimension_semantics=sem, vmem_limit_bytes=VMEM_LIMIT)


def _resident(shape, col_block=0):
    index = (0,) * (len(shape) - 1) + (col_block,)
    return pl.BlockSpec(shape, lambda *_: index, pipeline_mode=pl.Buffered(1))


def _mm_kernel(*refs, has_norm, has_bias):
    x_ref, w_ref = refs[0], refs[1]
    k = 2
    x = x_ref[...]
    if has_norm:
        g_ref = refs[k]
        k += 1
        xf = x.astype(F32)
        x = xf * lax.rsqrt(jnp.mean(xf * xf, axis=-1, keepdims=True) + NORM_EPS) * g_ref[...]
    acc = jnp.dot(x.astype(BF16), w_ref[...], preferred_element_type=F32)
    if has_bias:
        acc = acc + refs[k][...]
        k += 1
    o_ref = refs[k]
    o_ref[...] = acc.astype(o_ref.dtype)


def _mm(x, w, *, out_dtype, tm, tn, norm_g=None, bias=None, x_col_block=0):
    m = x.shape[0]
    kdim, n = w.shape
    assert m % tm == 0 and n % tn == 0
    in_specs = [pl.BlockSpec((tm, kdim), lambda i, j: (i, x_col_block)),
                pl.BlockSpec((kdim, tn), lambda i, j: (0, j))]
    args = [x, w]
    if norm_g is not None:
        in_specs.append(pl.BlockSpec((1, kdim), lambda i, j: (0, 0)))
        args.append(norm_g.reshape(1, kdim).astype(F32))
    if bias is not None:
        in_specs.append(pl.BlockSpec((1, tn), lambda i, j: (0, j)))
        args.append(bias.reshape(1, n).astype(F32))
    return pl.pallas_call(
        functools.partial(_mm_kernel, has_norm=norm_g is not None, has_bias=bias is not None),
        grid=(m // tm, n // tn),
        in_specs=in_specs,
        out_specs=pl.BlockSpec((tm, tn), lambda i, j: (i, j)),
        out_shape=jax.ShapeDtypeStruct((m, n), out_dtype),
        compiler_params=_cparams("parallel", "parallel"),
        name="mm",
    )(*args)


def _kidx_kernel(x_ref, g_ref, b_ref, k_ref, w_ref):
    blk = x_ref[...]
    x = blk[:, :IDX_DIM]
    mu = jnp.mean(x, axis=-1, keepdims=True)
    xc = x - mu
    var = jnp.mean(xc * xc, axis=-1, keepdims=True)
    k_ref[...] = (xc * lax.rsqrt(var + NORM_EPS) * g_ref[...] + b_ref[...]).astype(k_ref.dtype)
    w_ref[...] = blk.T[IDX_DIM:IDX_DIM + N_IDX_HEADS, :]


def _kidx_norm(p1, col_block, g, b):
    t = p1.shape[0]
    tm = min(t, 1024)
    return pl.pallas_call(
        _kidx_kernel,
        grid=(t // tm,),
        in_specs=[pl.BlockSpec((tm, LANES), lambda i: (i, col_block)),
                  pl.BlockSpec((1, IDX_DIM), lambda i: (0, 0)),
                  pl.BlockSpec((1, IDX_DIM), lambda i: (0, 0))],
        out_specs=[pl.BlockSpec((tm, IDX_DIM), lambda i: (i, 0)),
                   pl.BlockSpec((N_IDX_HEADS, tm), lambda i: (0, i))],
        out_shape=[jax.ShapeDtypeStruct((t, IDX_DIM), BF16),
                   jax.ShapeDtypeStruct((N_IDX_HEADS, t), F32)],
        compiler_params=_cparams("parallel"),
        name="kidx_norm",
    )(p1, g.reshape(1, IDX_DIM), b.reshape(1, IDX_DIM))


def _head(h):
    return slice(h * HEAD_DIM, (h + 1) * HEAD_DIM)


def _attn_state(n_heads):
    return ([pltpu.VMEM((CK, TQ), F32) for _ in range(2 * n_heads)]
            + [pltpu.VMEM((HEAD_DIM, TQ), F32) for _ in range(n_heads)]
            + [pltpu.VMEM((1, TQ), F32) for _ in range(2 * n_heads)])


def _attn_reset(state, n_heads):
    state = state[2 * n_heads:]
    for h in range(n_heads):
        state[h][...] = jnp.zeros((HEAD_DIM, TQ), F32)
        state[n_heads + h][...] = jnp.full((1, TQ), NEG, F32)
        state[2 * n_heads + h][...] = jnp.zeros((1, TQ), F32)


def _attn_run(state, n_heads, i, q_ref, k_ref, consume):
    def scores(j, slot):
        start = pl.multiple_of(j * CK, CK)
        for h in range(n_heads):
            state[slot * n_heads + h][...] = lax.dot_general(
                k_ref[pl.ds(start, CK), _head(h)], q_ref[:, _head(h)], _NT,
                preferred_element_type=F32)

    scores(0, 0)

    def pair(p, c):
        j = 2 * p
        scores(j + 1, 1)
        consume(j, 0, False)
        scores(j + 2, 0)
        consume(j + 1, 1, False)
        return c

    lax.fori_loop(0, i // 2, pair, 0)

    @pl.when(i % 2 == 0)
    def _():
        consume(i, 0, True)

    @pl.when(i % 2 == 1)
    def _():
        scores(i, 1)
        consume(i - 1, 0, False)
        consume(i, 1, True)


def _attn_update(state, n_heads, h, slot, vt_blk, t_bias, c):
    s_ref, state = state[slot * n_heads + h], state[2 * n_heads:]
    acc_ref, m_ref, l_ref = state[h], state[n_heads + h], state[2 * n_heads + h]
    t = s_ref[...] + t_bias
    m_old = m_ref[...]
    m_new = jnp.maximum(m_old, jnp.max(t, axis=0, keepdims=True) + c)
    alpha = jnp.exp2(m_old - m_new)
    p = jnp.exp2(t - (m_new - c))
    l_ref[...] = alpha * l_ref[...] + jnp.sum(p, axis=0, keepdims=True)
    acc_ref[...] = alpha * acc_ref[...] + jnp.dot(vt_blk, p.astype(BF16), preferred_element_type=F32)
    m_ref[...] = m_new


def _attn_finish(state, n_heads, h):
    state = state[2 * n_heads:]
    return (state[h][...] * (1.0 / state[2 * n_heads + h][...])).T


def _key_offsets():
    return lax.broadcasted_iota(I32, (CK, TQ), 0), lax.broadcasted_iota(I32, (CK, TQ), 1)


def _from_sortable(code):
    return pltpu.bitcast(code ^ ((code >> 31) & 0x7FFFFFFF), F32)


def _dsa_kernel(slopes_ref, qi_ref, wi_ref, kidx_ref, qa_ref, ka_ref, vat_ref, o_ref,
                keys_ref, half_ref, sb_ref, *state, top_k, n_idx_bits):
    i = pl.program_id(0)
    rows, cols = _key_offsets()
    causal = rows <= cols
    nh = N_HEADS_A

    @pl.when(i == 0)
    def _():
        rows_f = rows.astype(F32)
        for h in range(nh):
            sb_ref[h] = slopes_ref[h] * rows_f

    def score_chunk(j, diagonal):
        start = pl.multiple_of(j * CK, CK)
        kc = kidx_ref[pl.ds(start, CK), :]
        acc = jnp.zeros((CK, TQ), F32)
        for h in range(N_IDX_HEADS):
            z = lax.dot_general(kc, qi_ref[0, h * TQ:(h + 1) * TQ, :], _NT,
                                preferred_element_type=F32)
            acc = acc + wi_ref[h:h + 1, :] * jnp.maximum(z, 0.0)
        if diagonal:
            acc = jnp.where(causal, acc, NEG)
        keys_ref[pl.ds(start, CK), :] = acc
        half_ref[pl.ds(start, CK), :] = acc.astype(BF16)

    def score_body(j, c):
        score_chunk(j, False)
        return c

    lax.fori_loop(0, i, score_body, 0)
    score_chunk(i, True)

    def count32(pred):
        def body(j, cnt):
            start = pl.multiple_of(j * CK, CK)
            hit = jnp.where(pred(keys_ref[pl.ds(start, CK), :], rows + j * CK), 1, 0)
            return cnt + jnp.sum(hit.reshape(CK // SUBLANES, SUBLANES, TQ), axis=0)
        cnt = lax.fori_loop(0, i + 1, body, jnp.zeros((SUBLANES, TQ), I32))
        return jnp.sum(cnt, axis=0, keepdims=True)

    def count16(cand_bf):
        def body(j, cnt):
            blk = half_ref[pl.ds(pl.multiple_of(j * CK, CK), CK), :]
            hit = jnp.where(blk >= cand_bf, jnp.int16(1), jnp.int16(0))
            for r in range(0, CK, PACKED_ROWS):
                cnt = cnt + hit[r:r + PACKED_ROWS, :]
            return cnt
        cnt = lax.fori_loop(0, i + 1, body, jnp.zeros((PACKED_ROWS, TQ), jnp.int16))
        return jnp.sum(cnt.astype(I32), axis=0, keepdims=True)

    def coarse_bit(b, carry):
        c16, n16 = carry
        cand = c16 + jnp.left_shift(jnp.int32(1), 15 - b)
        bits = cand ^ ((cand >> 31) & 0x7FFF)
        n_cand = count16(pltpu.bitcast(bits.astype(jnp.int16), BF16))
        ok = n_cand >= top_k
        return jnp.where(ok, cand, c16), jnp.where(ok, n_cand, n16)

    c16, n16 = lax.fori_loop(0, 16, coarse_bit, (jnp.full((1, TQ), -(2 ** 15), I32),
                                                  jnp.zeros((1, TQ), I32) + (i + 1) * CK))

    base = (c16 - 1) << 16

    def fine_bit(b, carry):
        off, n_code = carry
        cand = off + jnp.left_shift(jnp.int32(1), FINE_BITS - 1 - b)
        cand_f = _from_sortable(base + cand)
        n_cand = count32(lambda score, idx: score >= cand_f)
        ok = n_cand >= top_k
        return jnp.where(ok, cand, off), jnp.where(ok, n_cand, n_code)

    off, n_ge = lax.fori_loop(0, FINE_BITS, fine_bit, (jnp.zeros((1, TQ), I32), n16))
    thr = _from_sortable(base + off)

    @pl.when(jnp.max(n_ge) > top_k)
    def _():
        keep = top_k - count32(lambda score, idx: score > thr)

        def idx_bit(b, last):
            cand = last + jnp.left_shift(jnp.int32(1), n_idx_bits - 1 - b)
            below = count32(lambda score, idx: jnp.logical_and(score == thr, idx < cand))
            return jnp.where(below < keep, cand, last)

        last = lax.fori_loop(0, n_idx_bits, idx_bit, jnp.zeros((1, TQ), I32))

        def drop(j, c):
            start = pl.multiple_of(j * CK, CK)
            score = keys_ref[pl.ds(start, CK), :]
            late_tie = jnp.logical_and(score == thr, rows + j * CK > last)
            keys_ref[pl.ds(start, CK), :] = jnp.where(late_tie, -jnp.inf, score)
            return c

        lax.fori_loop(0, i + 1, drop, 0)

    _attn_reset(state, nh)

    def consume(j, slot, diagonal):
        start = pl.multiple_of(j * CK, CK)
        sel = keys_ref[pl.ds(start, CK), :] >= thr
        if diagonal:
            sel = jnp.logical_and(sel, causal)
        mask_bias = jnp.where(sel, 0.0, NEG)
        off = ((j - i) * CK).astype(F32)
        for h in range(nh):
            _attn_update(state, nh, h, slot, vat_ref[_head(h), pl.ds(start, CK)],
                         sb_ref[h] + mask_bias, slopes_ref[h] * off)

    _attn_run(state, nh, i, qa_ref, ka_ref, consume)
    for h in range(nh):
        o_ref[:, _head(h)] = _attn_finish(state, nh, h).astype(o_ref.dtype)


def _dsa(slopes, qi_r, wi_t, kidx, qa, ka, vat, top_k):
    w, t = vat.shape
    nh = w // HEAD_DIM
    return pl.pallas_call(
        functools.partial(_dsa_kernel, top_k=top_k, n_idx_bits=max(1, (t - 1).bit_length())),
        grid_spec=pltpu.PrefetchScalarGridSpec(
            num_scalar_prefetch=0,
            grid=(t // TQ,),
            in_specs=[
                pl.BlockSpec(memory_space=pltpu.SMEM),
                pl.BlockSpec((1, N_IDX_HEADS * TQ, IDX_DIM), lambda i: (i, 0, 0)),
                pl.BlockSpec((N_IDX_HEADS, TQ), lambda i: (0, i)),
                _resident(kidx.shape),
                pl.BlockSpec((TQ, w), lambda i: (i, 0)),
                _resident((t, w)),
                _resident(vat.shape),
            ],
            out_specs=pl.BlockSpec((TQ, w), lambda i: (i, 0)),
            scratch_shapes=[pltpu.VMEM((t, TQ), F32), pltpu.VMEM((t, TQ), BF16),
                            pltpu.VMEM((nh, CK, TQ), F32)] + _attn_state(nh),
        ),
        out_shape=jax.ShapeDtypeStruct((t, w), BF16),
        compiler_params=_cparams("arbitrary"),
        name="dsa",
    )(slopes, qi_r, wi_t, kidx, qa, ka, vat)


def _moba_kernel(slopes_ref, qb_ref, kb_ref, vbt_ref, o_ref, kmean_ref, sel_ref, sb_ref, *state, n_kb):
    i = pl.program_id(0)
    rows, cols = _key_offsets()
    nh = N_HEADS_B

    @pl.when(i == 0)
    def _():
        rows_f = rows.astype(F32)
        for h in range(nh):
            sb_ref[h] = slopes_ref[h] * rows_f
            for n in range(n_kb):
                blk = kb_ref[n * CK:(n + 1) * CK, _head(h)].astype(F32)
                kmean_ref[h, n:n + 1, :] = jnp.mean(blk, axis=0, keepdims=True)

    blk_id = lax.broadcasted_iota(I32, (n_kb, TQ), 0)
    for h in range(nh):
        q_h = qb_ref[:, _head(h)]
        km = kmean_ref[h]
        km_hi = km.astype(BF16)
        km_lo = (km - km_hi.astype(F32)).astype(BF16)
        gate = (lax.dot_general(km_hi, q_h, _NT, preferred_element_type=F32)
                + lax.dot_general(km_lo, q_h, _NT, preferred_element_type=F32))
        gate = jnp.where(blk_id < i, gate, NEG)
        sel = jnp.full((n_kb, TQ), NEG, F32)
        for _ in range(MOBA_TOPK):
            best = jnp.max(gate, axis=0, keepdims=True)
            first = jnp.min(jnp.where(gate == best, blk_id, n_kb), axis=0, keepdims=True)
            pick = blk_id == first
            sel = jnp.where(pick, 0.0, sel)
            gate = jnp.where(pick, -jnp.inf, gate)
        sel_ref[h] = jnp.where(blk_id < i, sel, NEG)

    _attn_reset(state, nh)

    def consume(j, slot, diagonal):
        start = pl.multiple_of(j * CK, CK)
        off = ((j - i) * CK).astype(F32)
        for h in range(nh):
            if diagonal:
                t_bias = jnp.where(rows <= cols, sb_ref[h], NEG)
            else:
                t_bias = sb_ref[h] + sel_ref[h, pl.ds(j, 1), :]
            _attn_update(state, nh, h, slot, vbt_ref[_head(h), pl.ds(start, CK)],
                         t_bias, slopes_ref[h] * off)

    _attn_run(state, nh, i, qb_ref, kb_ref, consume)
    for h in range(nh):
        o_ref[:, _head(h)] = _attn_finish(state, nh, h).astype(o_ref.dtype)


def _moba(slopes, qkv, vbt):
    w, t = vbt.shape
    nh = w // HEAD_DIM
    assert t % MOBA_BLOCK == 0 and TQ == MOBA_BLOCK and CK == MOBA_BLOCK
    n_kb = t // MOBA_BLOCK
    return pl.pallas_call(
        functools.partial(_moba_kernel, n_kb=n_kb),
        grid_spec=pltpu.PrefetchScalarGridSpec(
            num_scalar_prefetch=0,
            grid=(t // TQ,),
            in_specs=[
                pl.BlockSpec(memory_space=pltpu.SMEM),
                pl.BlockSpec((TQ, w), lambda i: (i, 0)),
                _resident((t, w), col_block=1),
                _resident(vbt.shape),
            ],
            out_specs=pl.BlockSpec((TQ, w), lambda i: (i, 0)),
            scratch_shapes=[pltpu.VMEM((nh, n_kb, HEAD_DIM), F32), pltpu.VMEM((nh, n_kb, TQ), F32),
                            pltpu.VMEM((nh, CK, TQ), F32)] + _attn_state(nh),
        ),
        out_shape=jax.ShapeDtypeStruct((t, w), BF16),
        compiler_params=_cparams("arbitrary"),
        name="moba",
    )(slopes, qkv, qkv, vbt)


def _mem_kernel(qc_ref, mk_ref, mvt_ref, o_ref):
    for h in range(N_HEADS_C):
        s = lax.dot_general(mk_ref[:, _head(h)], qc_ref[:, _head(h)], _NT,
                            preferred_element_type=F32)
        m = jnp.max(s, axis=0, keepdims=True)
        p = jnp.exp2(s - m)
        l = jnp.sum(p, axis=0, keepdims=True)
        acc = jnp.dot(mvt_ref[_head(h), :], p.astype(BF16), preferred_element_type=F32)
        o_ref[:, _head(h)] = (acc * (1.0 / l)).T.astype(o_ref.dtype)


def _mem_attn(qc, mk, mvt):
    t, w = qc.shape
    return pl.pallas_call(
        _mem_kernel,
        grid=(t // TQ,),
        in_specs=[pl.BlockSpec((TQ, w), lambda i: (i, 0)),
                  _resident(mk.shape), _resident(mvt.shape)],
        out_specs=pl.BlockSpec((TQ, w), lambda i: (i, 0)),
        out_shape=jax.ShapeDtypeStruct((t, w), BF16),
        compiler_params=_cparams("parallel"),
        name="mem_attn",
    )(qc, mk, mvt)


def _sigmoid(x):
    return 1.0 / (1.0 + jnp.exp(-x))


def _merge_kernel(x_ref, wga_ref, wgb_ref, wgc_ref, bga_ref, bgb_ref, bgc_ref,
                  oa_ref, ob_ref, oc_ref, wua_ref, wub_ref, wuc_ref, o_ref):
    xb = x_ref[...].astype(BF16)

    def branch(wg_ref, bg_ref, oo_ref, wu_ref):
        g = _sigmoid(jnp.dot(xb, wg_ref[...], preferred_element_type=F32) + bg_ref[...])
        return g * jnp.dot(oo_ref[...], wu_ref[...], preferred_element_type=F32)

    acc = branch(wga_ref, bga_ref, oa_ref, wua_ref)
    acc = acc + branch(wgb_ref, bgb_ref, ob_ref, wub_ref)
    acc = acc + branch(wgc_ref, bgc_ref, oc_ref, wuc_ref)
    o_ref[...] = acc.astype(o_ref.dtype)


def _merge(x, w_gate, b_gate, o_a, o_b, o_c, w_up_a, w_up_b, w_up_c, *, tm, tn):
    t, d = x.shape
    nb = d // tn
    b_gate = b_gate.reshape(1, 3 * d)

    def wg(k):
        return pl.BlockSpec((d, tn), lambda i, j, k=k: (0, j + k * nb))

    def bg(k):
        return pl.BlockSpec((1, tn), lambda i, j, k=k: (0, j + k * nb))

    def act(w):
        return pl.BlockSpec((tm, w), lambda i, j: (i, 0))

    def wu(w):
        return pl.BlockSpec((w, tn), lambda i, j: (0, j))

    return pl.pallas_call(
        _merge_kernel,
        grid=(t // tm, nb),
        in_specs=[pl.BlockSpec((tm, d), lambda i, j: (i, 0)),
                  wg(0), wg(1), wg(2), bg(0), bg(1), bg(2),
                  act(W_A), act(W_B), act(W_C), wu(W_A), wu(W_B), wu(W_C)],
        out_specs=pl.BlockSpec((tm, tn), lambda i, j: (i, j)),
        out_shape=jax.ShapeDtypeStruct((t, d), BF16),
        compiler_params=_cparams("parallel", "parallel"),
        name="merge",
    )(x, w_gate, w_gate, w_gate, b_gate, b_gate, b_gate, o_a, o_b, o_c, w_up_a, w_up_b, w_up_c)


def _layer_norm(y, g, b):
    mu = jnp.mean(y, axis=-1, keepdims=True)
    yc = y - mu
    var = jnp.mean(yc * yc, axis=-1, keepdims=True)
    return yc * lax.rsqrt(var + NORM_EPS) * g + b


def _route(logits):
    lane = lax.broadcasted_iota(I32, logits.shape, 1)
    is_g = lane < N_GROUPS
    gl = jnp.where(is_g, logits, -jnp.inf)
    gmax = jnp.max(gl, axis=-1, keepdims=True)
    g_sel = jnp.min(jnp.where(gl == gmax, lane, ROUTE_LANES), axis=-1, keepdims=True)
    p_g = 1.0 / jnp.sum(jnp.where(is_g, jnp.exp(gl - gmax), 0.0), axis=-1, keepdims=True)
    e_id = lane - N_GROUPS
    in_grp = jnp.logical_and(e_id >= g_sel * EXPERTS_PER_GROUP, e_id < (g_sel + 1) * EXPERTS_PER_GROUP)
    el = jnp.where(in_grp, logits, -jnp.inf)
    emax = jnp.max(el, axis=-1, keepdims=True)
    ex = jnp.where(in_grp, jnp.exp(el - emax), 0.0)
    pe = ex / jnp.sum(ex, axis=-1, keepdims=True)
    pe = jnp.where(in_grp, pe, -1.0)
    p1 = jnp.max(pe, axis=-1, keepdims=True)
    i1 = jnp.min(jnp.where(pe == p1, e_id, ROUTE_LANES), axis=-1, keepdims=True)
    pe2 = jnp.where(e_id == i1, -1.0, pe)
    p2 = jnp.max(pe2, axis=-1, keepdims=True)
    i2 = jnp.min(jnp.where(pe2 == p2, e_id, ROUTE_LANES), axis=-1, keepdims=True)
    denom = p1 + p2
    w1 = p_g * (p1 / denom)
    w2 = p_g * (p2 / denom)
    return jnp.where(lane == 0, w1,
                     jnp.where(lane == 1, w2,
                               jnp.where(lane == 2, i1.astype(F32),
                                         jnp.where(lane == 3, i2.astype(F32), 0.0))))


def _oproj_kernel(x_ref, m_ref, wo_ref, g_ref, b_ref, wr_ref, br_ref, x1_ref, r_ref, *, alpha):
    y = alpha * x_ref[...] + jnp.dot(m_ref[...], wo_ref[...], preferred_element_type=F32)
    x1 = _layer_norm(y, g_ref[...], b_ref[...])
    x1_ref[...] = x1
    x_hi = x1.astype(BF16)
    x_lo = (x1 - x_hi.astype(F32)).astype(BF16)
    parts = (jnp.dot(x_hi, wr_ref[...], preferred_element_type=F32)
             + jnp.dot(x_lo, wr_ref[...], preferred_element_type=F32))
    logits = parts + pltpu.roll(parts, ROUTE_LANES // 2, axis=1) + br_ref[...]
    r_ref[...] = _route(logits)


def _oproj(x, merged, w_o, ln_g, ln_b, w_r, b_r, *, alpha, tm):
    t, d = x.shape
    return pl.pallas_call(
        functools.partial(_oproj_kernel, alpha=alpha),
        grid=(t // tm,),
        in_specs=[pl.BlockSpec((tm, d), lambda i: (i, 0)),
                  pl.BlockSpec((tm, d), lambda i: (i, 0)),
                  _resident(w_o.shape),
                  pl.BlockSpec((1, d), lambda i: (0, 0)),
                  pl.BlockSpec((1, d), lambda i: (0, 0)),
                  _resident(w_r.shape),
                  pl.BlockSpec((1, ROUTE_LANES), lambda i: (0, 0))],
        out_specs=[pl.BlockSpec((tm, d), lambda i: (i, 0)),
                   pl.BlockSpec((tm, ROUTE_LANES), lambda i: (i, 0))],
        out_shape=[jax.ShapeDtypeStruct((t, d), F32),
                   jax.ShapeDtypeStruct((t, ROUTE_LANES), F32)],
        compiler_params=_cparams("parallel"),
        name="oproj",
    )(x, merged, w_o, ln_g.reshape(1, d), ln_b.reshape(1, d), w_r, b_r)


def _row_copy(src_hbm, row, dst, k, sem):
    return pltpu.make_async_copy(src_hbm.at[pl.ds(row, 1), :], dst.at[pl.ds(k, 1), :], sem)


def _expert_kernel(tok_ref, te_ref, first_ref, nxt_ref, ws_ref, ng_ref, nu_ref, x_hbm, w1_hbm, w3_hbm, w2_hbm,
                   o_ref, buf, w1b, w3b, w2b, sem, wsem, *, layer):
    i = pl.program_id(0)
    n_used = nu_ref[0]
    slot = lax.rem(i, 2)

    def gather(tile, s):
        base = tile * MOE_TM

        def body(g, c):
            r0 = pl.multiple_of(g * ROW_DMA_UNROLL, ROW_DMA_UNROLL)
            for k in range(ROW_DMA_UNROLL):
                _row_copy(x_hbm, tok_ref[base + r0 + k], buf.at[s], r0 + k, sem.at[s]).start()
            return c

        lax.fori_loop(0, ng_ref[tile], body, 0)

    def gathered(tile, s):
        def body(g, c):
            r0 = pl.multiple_of(g * ROW_DMA_UNROLL, ROW_DMA_UNROLL)
            pltpu.make_async_copy(x_hbm.at[pl.ds(0, ROW_DMA_UNROLL), :],
                                  buf.at[s, pl.ds(r0, ROW_DMA_UNROLL), :], sem.at[s]).wait()
            return c

        lax.fori_loop(0, ng_ref[tile], body, 0)

    def weight_copies(e, s):
        return [pltpu.make_async_copy(src.at[layer, e], dst.at[s], wsem.at[s])
                for src, dst in ((w1_hbm, w1b), (w3_hbm, w3b), (w2_hbm, w2b))]

    @pl.when(i == 0)
    def _():
        buf[...] = jnp.zeros_like(buf)
        gather(0, 0)
        for cp in weight_copies(te_ref[0], 0):
            cp.start(priority=BULK_DMA_PRIORITY)

    @pl.when(i < n_used)
    def _():
        ws = ws_ref[i]

        @pl.when(first_ref[i] == 1)
        def _():
            for cp in weight_copies(te_ref[i], ws):
                cp.wait()

            @pl.when(nxt_ref[i] >= 0)
            def _():
                for cp in weight_copies(nxt_ref[i], 1 - ws):
                    cp.start(priority=BULK_DMA_PRIORITY)

        gathered(i, slot)

        @pl.when(i + 1 < n_used)
        def _():
            gather(i + 1, 1 - slot)

        xs = buf[slot].astype(BF16)
        h1 = jnp.dot(xs, w1b[ws].astype(BF16), preferred_element_type=F32)
        h3 = jnp.dot(xs, w3b[ws].astype(BF16), preferred_element_type=F32)
        hid = (h1 * _sigmoid(h1)) * h3
        o_ref[...] = jnp.dot(hid.astype(BF16), w2b[ws].astype(BF16), preferred_element_type=F32)

    @pl.when(i >= n_used)
    def _():
        o_ref[...] = jnp.zeros_like(o_ref)


def _experts(plan, x1, w1, w3, w2, layer):
    r = plan[0].shape[0]
    d = x1.shape[1]
    de = w1.shape[-1]
    n_pref = len(plan)
    return pl.pallas_call(
        functools.partial(_expert_kernel, layer=layer),
        grid_spec=pltpu.PrefetchScalarGridSpec(
            num_scalar_prefetch=n_pref,
            grid=(r // MOE_TM,),
            in_specs=[pl.BlockSpec(memory_space=pl.ANY)] * 4,
            out_specs=pl.BlockSpec((MOE_TM, d), lambda i, *_: (i, 0)),
            scratch_shapes=[pltpu.VMEM((2, MOE_TM, d), F32),
                            pltpu.VMEM((2, d, de), F32), pltpu.VMEM((2, d, de), F32),
                            pltpu.VMEM((2, de, d), F32),
                            pltpu.SemaphoreType.DMA((2,)), pltpu.SemaphoreType.DMA((2,))],
        ),
        out_shape=jax.ShapeDtypeStruct((r, d), F32),
        compiler_params=_cparams("arbitrary"),
        name="moe_experts",
    )(*plan, x1, w1, w3, w2)


def _combine_kernel(pos_ref, x1_ref, route_ref, g_ref, b_ref, ys_hbm, o_ref, buf, sem, *, alpha):
    i = pl.program_id(0)
    slot = lax.rem(i, 2)

    def gather(tile, s):
        base = tile * MOE_TM

        def body(r, c):
            for k in range(2):
                _row_copy(ys_hbm, pos_ref[2 * (base + r) + k], buf.at[s, k], r, sem.at[s]).start(priority=k)
            return c

        lax.fori_loop(0, MOE_TM, body, 0, unroll=ROW_DMA_UNROLL // 2)

    @pl.when(i == 0)
    def _():
        gather(0, 0)

    for k in range(2):
        pltpu.make_async_copy(ys_hbm.at[pl.ds(0, MOE_TM), :], buf.at[slot, k], sem.at[slot]).wait()

    @pl.when(i + 1 < pl.num_programs(0))
    def _():
        gather(i + 1, 1 - slot)

    route = route_ref[...]
    y = alpha * x1_ref[...] + (route[:, 0:1] * buf[slot, 0] + route[:, 1:2] * buf[slot, 1])
    o_ref[...] = _layer_norm(y, g_ref[...], b_ref[...])


def _combine(pos, x1, route, ln_g, ln_b, ys, *, alpha):
    t, d = x1.shape
    return pl.pallas_call(
        functools.partial(_combine_kernel, alpha=alpha),
        grid_spec=pltpu.PrefetchScalarGridSpec(
            num_scalar_prefetch=1,
            grid=(t // MOE_TM,),
            in_specs=[pl.BlockSpec((MOE_TM, d), lambda i, pos: (i, 0)),
                      pl.BlockSpec((MOE_TM, ROUTE_LANES), lambda i, pos: (i, 0)),
                      pl.BlockSpec((1, d), lambda i, pos: (0, 0)),
                      pl.BlockSpec((1, d), lambda i, pos: (0, 0)),
                      pl.BlockSpec(memory_space=pl.ANY)],
            out_specs=pl.BlockSpec((MOE_TM, d), lambda i, pos: (i, 0)),
            scratch_shapes=[pltpu.VMEM((2, 2, MOE_TM, d), F32), pltpu.SemaphoreType.DMA((2,))],
        ),
        out_shape=jax.ShapeDtypeStruct((t, d), F32),
        compiler_params=_cparams("arbitrary"),
        name="moe_combine",
    )(pos, x1, route, ln_g.reshape(1, d), ln_b.reshape(1, d), ys)


def _moe_plan(route, t):
    flat_e = route[:, 2:4].astype(I32).reshape(-1)
    n = flat_e.shape[0]
    n_rows = n + N_EXPERTS * MOE_TM
    n_tiles = n_rows // MOE_TM
    onehot = (flat_e[:, None] == jnp.arange(N_EXPERTS, dtype=I32)[None, :]).astype(I32)
    running = jnp.cumsum(onehot, axis=0)
    rank = jnp.sum(onehot * running, axis=1) - 1
    counts = running[-1]
    tiles_per = (counts + MOE_TM - 1) // MOE_TM
    tile_end = jnp.cumsum(tiles_per)
    grp_row0 = (tile_end - tiles_per) * MOE_TM
    pos = jnp.sum(onehot * grp_row0[None, :], axis=1) + rank
    row_token = jnp.zeros((n_rows,), I32).at[pos].set(jnp.arange(n, dtype=I32) // 2, unique_indices=True)
    n_used = tile_end[-1]
    tile_ids = jnp.minimum(jnp.arange(n_tiles, dtype=I32), n_used - 1)
    tile_expert = jnp.sum((tile_end[None, :] <= tile_ids[:, None]).astype(I32), axis=1)
    tile_expert = jnp.minimum(tile_expert, N_EXPERTS - 1)
    first = jnp.concatenate([jnp.ones((1,), I32), (tile_expert[1:] != tile_expert[:-1]).astype(I32)])
    w_slot = (jnp.cumsum(first) - 1) % 2
    e_ids = jnp.arange(N_EXPERTS, dtype=I32)
    later = jnp.logical_and(e_ids[None, :] > e_ids[:, None], tiles_per[None, :] > 0)
    next_e = jnp.min(jnp.where(later, e_ids[None, :], N_EXPERTS), axis=1)
    next_e = jnp.where(next_e == N_EXPERTS, -1, next_e)
    tile_of = (tile_expert[:, None] == e_ids[None, :]).astype(I32)
    nxt = jnp.sum(tile_of * next_e[None, :], axis=1)
    t_ids = jnp.arange(n_tiles, dtype=I32)
    rows_left = jnp.sum(tile_of * counts[None, :], axis=1) - (t_ids * MOE_TM - jnp.sum(tile_of * grp_row0[None, :], axis=1))
    rows_valid = jnp.where(t_ids < n_used, jnp.clip(rows_left, 0, MOE_TM), 0)
    n_groups = (rows_valid + ROW_DMA_UNROLL - 1) // ROW_DMA_UNROLL
    plan = (row_token, tile_expert, first, nxt.astype(I32), w_slot.astype(I32), n_groups.astype(I32),
            n_used.reshape(1).astype(I32))
    return plan, pos


def kernel(x, mem, w_mem_kv, w_in, g_cq, g_ckv, g_kidx, b_kidx, w_uq, w_uqi, w_ukv,
           w_up_a, w_up_b, w_up_c, w_gate, b_gate, w_o, ln1_g, ln1_b,
           w_grp, b_grp, w_rt, b_rt, w1, w3, w2, ln2_g, ln2_b):
    bsz, t, d = x.shape
    assert bsz == 1 and t % TQ == 0
    depth = w_in.shape[0]
    alpha = (2 * depth) ** 0.25
    scale = HEAD_DIM ** -0.5 * LOG2E
    top_k = min(TOPK_A_MAX, t // 4)
    n_mem = mem.shape[1]
    tm = min(t, 512)

    slopes = LOG2E * 2.0 ** (-8.0 * jnp.arange(1, N_ALIBI + 1, dtype=F32) / N_ALIBI)
    slopes_a, slopes_b = slopes[0::2], slopes[1::2]

    mkv = _mm(mem[0], w_mem_kv.astype(BF16), out_dtype=BF16, tm=n_mem, tn=W_C)
    mk = mkv[:, :W_C]
    mvt = mkv[:, W_C:].T

    xl = x[0]
    for l in range(depth):
        wl = w_in[l]
        o = 0
        parts = []
        for width in (Q_RANK_A, KV_RANK_A, IDX_DIM, N_IDX_HEADS, W_B, W_B, W_B, W_C):
            parts.append(wl[:, o:o + width])
            o += width
        w_cq, w_ckv, w_ki, w_wi, w_qb, w_kb, w_vb, w_qc = parts
        n_small = Q_RANK_A + KV_RANK_A + IDX_DIM + N_IDX_HEADS
        pad = (-n_small) % LANES
        w_p1 = jnp.concatenate([w_cq, w_ckv, w_ki, w_wi * N_IDX_HEADS ** -0.5,
                                jnp.zeros((d, pad), F32)], axis=1).astype(BF16)
        w_p2 = jnp.concatenate([w_qb * scale, w_kb, w_vb, w_qc * scale], axis=1).astype(BF16)
        p1 = _mm(xl, w_p1, out_dtype=F32, tm=tm, tn=w_p1.shape[1])
        p2 = _mm(xl, w_p2, out_dtype=BF16, tm=tm, tn=w_p2.shape[1] // 2)
        assert Q_RANK_A % KV_RANK_A == 0 and (Q_RANK_A + KV_RANK_A) % LANES == 0

        w_q = jnp.concatenate([w_uq[l] * scale, w_uqi[l] * IDX_DIM ** -0.5], axis=1).astype(BF16)
        qq = _mm(p1, w_q, out_dtype=BF16, tm=tm, tn=w_q.shape[1], norm_g=g_cq[l])
        kv = _mm(p1, w_ukv[l].astype(BF16), out_dtype=BF16, tm=tm, tn=2 * W_A, norm_g=g_ckv[l],
                 x_col_block=Q_RANK_A // KV_RANK_A)
        kidx, wi_t = _kidx_norm(p1, (Q_RANK_A + KV_RANK_A) // LANES, g_kidx[l], b_kidx[l])
        qi_r = (qq[:, W_A:].reshape(t // TQ, TQ, N_IDX_HEADS, IDX_DIM).transpose(0, 2, 1, 3)
                .reshape(t // TQ, N_IDX_HEADS * TQ, IDX_DIM))
        o_a = _dsa(slopes_a, qi_r, wi_t, kidx, qq, kv, kv[:, W_A:].T, top_k)
        o_b = _moba(slopes_b, p2, p2[:, 2 * W_B:3 * W_B].T)
        o_c = _mem_attn(p2[:, 3 * W_B:], mk, mvt)

        merged = _merge(xl, w_gate[l].astype(BF16), b_gate[l], o_a, o_b, o_c,
                        w_up_a[l].astype(BF16), w_up_b[l].astype(BF16), w_up_c[l].astype(BF16),
                        tm=min(t, 1024), tn=512)

        n_route = N_GROUPS + N_EXPERTS
        w_r = jnp.concatenate([w_grp[l], w_rt[l]], axis=1)
        w_r_hi = w_r.astype(BF16)
        w_r_lo = (w_r - w_r_hi.astype(F32)).astype(BF16)
        lane_pad = jnp.zeros((d, ROUTE_LANES // 2 - n_route), BF16)
        w_r2 = jnp.concatenate([w_r_hi, lane_pad, w_r_lo, lane_pad], axis=1)
        b_r = jnp.concatenate([b_grp[l], b_rt[l],
                               jnp.zeros((ROUTE_LANES - n_route,), F32)]).reshape(1, ROUTE_LANES)
        x1, route = _oproj(xl, merged, w_o[l].astype(BF16), ln1_g[l], ln1_b[l], w_r2, b_r,
                           alpha=alpha, tm=tm)

        plan, pos = _moe_plan(route, t)
        ys = _experts(plan, x1, w1, w3, w2, l)
        xl = _combine(pos, x1, route, ln2_g[l], ln2_b[l], ys, alpha=alpha)
    return xl[None]
```
